```python
import jax
import jax.numpy as jnp
from jax import lax
import numpy as np


D_MODEL = 1024
BATCH = 16
SEQ = 2048
DEPTH = 2

GRID_W = 64
CTX_LEN = 256
HEAD_DIM = 64
N_HEADS_A = 8
N_KV_A = 2
N_HEADS_B = 8
N_KV_B = 2
MIX_WIDTH = (N_HEADS_A + N_HEADS_B) * HEAD_DIM
Q_BLOCK = 128
WINDOW = 128
D_FF = 4 * D_MODEL
ROPE_THETA = 10000.0
EPS = 1e-6
NEG_BIG = -1e30
N_MOD = 6
COL_SIZES = (N_HEADS_A * HEAD_DIM, N_KV_A * HEAD_DIM, N_KV_A * HEAD_DIM,
             N_HEADS_B * HEAD_DIM, N_KV_B * HEAD_DIM, N_KV_B * HEAD_DIM)
IN_COLS = sum(COL_SIZES)
SPLITS = tuple(int(s) for s in np.cumsum(COL_SIZES)[:-1])

kernel_name = 'hybrid_dit_gqa_window_sink_block'


def rms_norm(x, g):
    xf = x.astype(jnp.float32)
    y = xf * lax.rsqrt(jnp.mean(xf * xf, axis=-1, keepdims=True) + EPS)
    return (y * g.astype(jnp.float32)).astype(x.dtype)


def modulate(x, shift, scale):
    return x * (1 + scale) + shift


def axial_rope_tables(n_tok):
    rows = n_tok // GRID_W
    row_ids = jnp.repeat(jnp.arange(rows, dtype=jnp.int32), GRID_W).astype(jnp.float32)
    col_ids = jnp.tile(jnp.arange(GRID_W, dtype=jnp.int32), rows).astype(jnp.float32)
    axis_dim = HEAD_DIM // 2
    inv = ROPE_THETA ** (-jnp.arange(0, axis_dim, 2, dtype=jnp.float32) / axis_dim)
    ang_r = row_ids[:, None] * inv[None, :]
    ang_c = col_ids[:, None] * inv[None, :]
    return (jnp.cos(ang_r), jnp.sin(ang_r), jnp.cos(ang_c), jnp.sin(ang_c))


def _rotate(a, cos, sin):
    a1, a2 = jnp.split(a, 2, axis=-1)
    return jnp.concatenate([a1 * cos - a2 * sin, a2 * cos + a1 * sin], axis=-1)


def apply_axial_rope(t, rope):
    cos_r, sin_r, cos_c, sin_c = rope
    tf = t.astype(jnp.float32)
    tr, tc = jnp.split(tf, 2, axis=-1)
    return jnp.concatenate([_rotate(tr, cos_r, sin_r), _rotate(tc, cos_c, sin_c)], axis=-1).astype(t.dtype)


def to_gqa(t, n_heads, n_kv):
    b, n, _ = t.shape
    return t.reshape(b, n, n_kv, n_heads // n_kv, HEAD_DIM).transpose(0, 2, 3, 1, 4)


def to_kv(t, n_kv):
    b, n, _ = t.shape
    return t.reshape(b, n, n_kv, HEAD_DIM).transpose(0, 2, 1, 3)


def merge_heads(o):
    b, hk, g, n, d = o.shape
    return o.transpose(0, 3, 1, 2, 4).reshape(b, n, hk * g * d)


def multi_source_attention(q, ks, vs, biases, sink):
    scale = HEAD_DIM ** -0.5
    logits = [jnp.einsum('bhgqd,bhkd->bhgqk', q, k, preferred_element_type=jnp.float32) * scale + bias
              for k, bias in zip(ks, biases)]
    sizes = [l.shape[-1] for l in logits]
    if sink is not None:
        s = sink.astype(jnp.float32)[None, :, :, None, None]
        logits.append(jnp.broadcast_to(s, logits[0].shape[:-1] + (1,)))
    p = jax.nn.softmax(jnp.concatenate(logits, axis=-1), axis=-1)
    out = None
    off = 0
    for v, n in zip(vs, sizes):
        term = jnp.einsum('bhgqk,bhkd->bhgqd', p[..., off:off + n].astype(v.dtype), v)
        out = term if out is None else out + term
        off += n
    return out


def dense_latent_attention(q, k_lat, v_lat, k_ctx, v_ctx):
    b, hk, g, s, d = q.shape
    nblk = s // Q_BLOCK
    qb = jnp.moveaxis(q.reshape(b, hk, g, nblk, Q_BLOCK, d), 3, 0)

    def one_block(qi):
        return multi_source_attention(qi, (k_ctx, k_lat), (v_ctx, v_lat), (0.0, 0.0), None)

    o = lax.map(one_block, qb)
    return jnp.moveaxis(o, 0, 3).reshape(b, hk, g, s, d)


def banded_blocks(t):
    b, hk, s, d = t.shape
    nblk = s // Q_BLOCK
    tp = jnp.pad(t, ((0, 0), (0, 0), (Q_BLOCK, Q_BLOCK), (0, 0))).reshape(b, hk, nblk + 2, Q_BLOCK, d)
    band = jnp.concatenate([tp[:, :, :-2], tp[:, :, 1:-1], tp[:, :, 2:]], axis=3)
    return jnp.moveaxis(band, 2, 0)


def window_bias(s):
    nblk = s // Q_BLOCK
    i = jnp.arange(nblk, dtype=jnp.int32)[:, None, None]
    r = jnp.arange(Q_BLOCK, dtype=jnp.int32)[None, :, None]
    j = jnp.arange(3 * Q_BLOCK, dtype=jnp.int32)[None, None, :]
    qpos = i * Q_BLOCK + r
    kpos = (i - 1) * Q_BLOCK + j
    valid = (jnp.abs(kpos - qpos) <= WINDOW) & (kpos >= 0) & (kpos < s)
    return jnp.where(valid, 0.0, NEG_BIG).astype(jnp.float32)


def windowed_latent_attention(q, k_lat, v_lat, k_ctx, v_ctx, sink):
    b, hk, g, s, d = q.shape
    nblk = s // Q_BLOCK
    qb = jnp.moveaxis(q.reshape(b, hk, g, nblk, Q_BLOCK, d), 3, 0)
    kb = banded_blocks(k_lat)
    vb = banded_blocks(v_lat)
    bias = window_bias(s)

    def one_block(args):
        qi, ki, vi, bi = args
        return multi_source_attention(qi, (ki, k_ctx), (vi, v_ctx), (bi[None, None, None], 0.0), sink)

    o = lax.map(one_block, (qb, kb, vb, bias))
    return jnp.moveaxis(o, 0, 3).reshape(b, hk, g, s, d)


def sq_relu_mlp(u, w_up, w_down):
    return jnp.square(jax.nn.relu(u @ w_up)) @ w_down


def setup_inputs(seed: int = 0) -> dict:
    key = jax.random.key(seed)
    ks = jax.random.split(key, 18)
    f32 = jnp.float32
    nrm = lambda k, shape, s: jax.random.normal(k, shape, f32) * s
    return {
        'x': nrm(ks[0], (BATCH, SEQ, D_MODEL), 1.0),
        'c': nrm(ks[1], (BATCH, D_MODEL), 1.0),
        'ctx': nrm(ks[2], (BATCH, CTX_LEN, D_MODEL), 1.0),
        'c_ctx': nrm(ks[3], (D_MODEL,), 1.0),
        'w_ada': nrm(ks[4], (DEPTH, D_MODEL, N_MOD * D_MODEL), 0.02),
        'b_ada': nrm(ks[5], (DEPTH, N_MOD * D_MODEL), 0.01),
        'g_pre_mix': 1.0 + nrm(ks[6], (DEPTH, D_MODEL), 0.05),
        'g_post_mix': 1.0 + nrm(ks[7], (DEPTH, D_MODEL), 0.05),
        'g_pre_mlp': 1.0 + nrm(ks[8], (DEPTH, D_MODEL), 0.05),
        'g_post_mlp': 1.0 + nrm(ks[9], (DEPTH, D_MODEL), 0.05),
        'w_in': nrm(ks[10], (DEPTH, D_MODEL, IN_COLS), D_MODEL ** -0.5),
        'q_norm': 1.0 + nrm(ks[11], (DEPTH, HEAD_DIM), 0.05),
        'k_norm': 1.0 + nrm(ks[12], (DEPTH, HEAD_DIM), 0.05),
        'sink': nrm(ks[13], (DEPTH, N_HEADS_B), 0.5),
        'w_out': nrm(ks[14], (DEPTH, MIX_WIDTH, D_MODEL), MIX_WIDTH ** -0.5),
        'w_up': nrm(ks[15], (DEPTH, D_MODEL, D_FF), D_MODEL ** -0.5),
        'w_down': nrm(ks[16], (DEPTH, D_FF, D_MODEL), D_FF ** -0.5),
    }


def reference(x, c, ctx, c_ctx, w_ada, b_ada, g_pre_mix, g_post_mix, g_pre_mlp, g_post_mlp,
              w_in, q_norm, k_norm, sink, w_out, w_up, w_down):
    b, s, d = x.shape
    rope = axial_rope_tables(s)
    silu_c = jax.nn.silu(c)
    silu_cc = jax.nn.silu(c_ctx)
    h, hc = x, ctx
    for l in range(DEPTH):
        last = l == DEPTH - 1
        mod = (silu_c @ w_ada[l] + b_ada[l]).reshape(b, N_MOD, 1, d)
        mod_c = (silu_cc @ w_ada[l] + b_ada[l]).reshape(N_MOD, 1, 1, d)
        sh_a, sc_a, g_a, sh_m, sc_m, g_m = [mod[:, i] for i in range(N_MOD)]
        csh_a, csc_a, cg_a, csh_m, csc_m, cg_m = [mod_c[i] for i in range(N_MOD)]

        u = modulate(rms_norm(h, g_pre_mix[l]), sh_a, sc_a)
        uc = modulate(rms_norm(hc, g_pre_mix[l]), csh_a, csc_a)
        qa, ka, va, qb, kb, vb = jnp.split(u @ w_in[l], SPLITS, axis=-1)
        qac, kac, vac, qbc, kbc, vbc = jnp.split(uc @ w_in[l], SPLITS, axis=-1)

        qa = apply_axial_rope(rms_norm(to_gqa(qa, N_HEADS_A, N_KV_A), q_norm[l]), rope)
        ka = apply_axial_rope(rms_norm(to_kv(ka, N_KV_A), k_norm[l]), rope)
        va = to_kv(va, N_KV_A)
        kac = rms_norm(to_kv(kac, N_KV_A), k_norm[l])
        vac = to_kv(vac, N_KV_A)
        qb = apply_axial_rope(to_gqa(qb, N_HEADS_B, N_KV_B), rope)
        kb = apply_axial_rope(to_kv(kb, N_KV_B), rope)
        vb = to_kv(vb, N_KV_B)
        kbc = to_kv(kbc, N_KV_B)
        vbc = to_kv(vbc, N_KV_B)
        sink_l = sink[l].reshape(N_KV_B, N_HEADS_B // N_KV_B)

        oa = dense_latent_attention(qa, ka, va, kac, vac)
        ob = windowed_latent_attention(qb, kb, vb, kbc, vbc, sink_l)
        mix = jnp.concatenate([merge_heads(oa), merge_heads(ob)], axis=-1) @ w_out[l]
        h = h + g_a * rms_norm(mix, g_post_mix[l])

        if not last:
            qac = rms_norm(to_gqa(qac, N_HEADS_A, N_KV_A), q_norm[l])
            oac = multi_source_attention(qac, (kac,), (vac,), (0.0,), None)
            obc = multi_source_attention(to_gqa(qbc, N_HEADS_B, N_KV_B), (kbc,), (vbc,), (0.0,), sink_l)
            mixc = jnp.concatenate([merge_heads(oac), merge_heads(obc)], axis=-1) @ w_out[l]
            hc = hc + cg_a * rms_norm(mixc, g_post_mix[l])

        y = sq_relu_mlp(modulate(rms_norm(h, g_pre_mlp[l]), sh_m, sc_m), w_up[l], w_down[l])
        h = h + g_m * rms_norm(y, g_post_mlp[l])
        if not last:
            yc = sq_relu_mlp(modulate(rms_norm(hc, g_pre_mlp[l]), csh_m, csc_m), w_up[l], w_down[l])
            hc = hc + cg_m * rms_norm(yc, g_post_mlp[l])
    return h
```

```python
import functools

import jax
import jax.numpy as jnp
from jax import lax
from jax.experimental import pallas as pl
from jax.experimental.pallas import tpu as pltpu

F32 = jnp.float32
BF16 = jnp.bfloat16

D_MODEL = 1024
HEAD_DIM = 64
N_HEADS = 8
N_KV = 2
GROUP = N_HEADS // N_KV
Q_COLS = N_HEADS * HEAD_DIM
KV_COLS = N_KV * HEAD_DIM
IN_COLS = 2 * (Q_COLS + 2 * KV_COLS)
D_FF = 4 * D_MODEL
GRID_W = 64
WINDOW = 128
Q_BLOCK = 128
ROPE_THETA = 10000.0
EPS = 1e-6
NEG_BIG = -1e30
N_MOD = 6
ROPE_SHIFT = HEAD_DIM // 4

LANES = 128
TOK_TILE = 256
KEY_CHUNK = 256
FF_CHUNK = 1024
ADA_ROWS = 32
ADA_COLS = 1536
VMEM_LIMIT = 56 * 1024 * 1024


def _rms(x, g):
    return x * lax.rsqrt(jnp.mean(x * x, axis=-1, keepdims=True) + EPS) * g


def _ada_kernel(c_ref, w_ref, b_ref, o_ref):
    c = c_ref[...]
    a = c * jax.nn.sigmoid(c)
    w = w_ref[...]
    a_hi = a.astype(BF16)
    a_lo = (a - a_hi.astype(F32)).astype(BF16)
    w_hi = w.astype(BF16)
    w_lo = (w - w_hi.astype(F32)).astype(BF16)
    acc = jnp.dot(a_hi, w_hi, preferred_element_type=F32)
    acc += jnp.dot(a_lo, w_hi, preferred_element_type=F32)
    acc += jnp.dot(a_hi, w_lo, preferred_element_type=F32)
    o_ref[...] = acc + b_ref[...]


def _ada(cc, w_ada, b_ada):
    depth = w_ada.shape[0]
    n_out = w_ada.shape[2]
    return pl.pallas_call(
        _ada_kernel,
        grid=(depth, n_out // ADA_COLS),
        in_specs=[
            pl.BlockSpec((ADA_ROWS, D_MODEL), lambda l, j: (0, 0)),
            pl.BlockSpec((None, D_MODEL, ADA_COLS), lambda l, j: (l, 0, j)),
            pl.BlockSpec((None, 1, ADA_COLS), lambda l, j: (l, 0, j)),
        ],
        out_specs=pl.BlockSpec((None, ADA_ROWS, ADA_COLS), lambda l, j: (l, 0, j)),
        out_shape=jax.ShapeDtypeStruct((depth, ADA_ROWS, n_out), F32),
        compiler_params=pltpu.CompilerParams(
            dimension_semantics=("arbitrary", "arbitrary"), vmem_limit_bytes=VMEM_LIMIT),
        name="ada",
    )(cc, w_ada, b_ada.reshape(depth, 1, n_out))


def _proj_kernel(h_ref, mod_ref, gpre_ref, w_ref, gq_ref, gk_ref, eq_ref, ek_ref,
                 cos_ref, sa_ref, sb_ref,
                 qa_ref, kda_ref, vta_ref, qb_ref, kdb_ref, vtb_ref):
    x = h_ref[...]
    shift = mod_ref[0:1, :]
    scale = mod_ref[1:2, :]
    u = (_rms(x, gpre_ref[...]) * (1 + scale) + shift).astype(BF16)
    z = jnp.dot(u, w_ref[...], preferred_element_type=F32)

    cos = cos_ref[...]
    sa = sa_ref[...]
    sb = sb_ref[...]

    def rope(t):
        up = pltpu.roll(t, LANES - ROPE_SHIFT, 1)
        dn = pltpu.roll(t, ROPE_SHIFT, 1)
        return t * cos + up * sa + dn * sb

    def head_norm(t, e_ref, g_ref):
        ss = jnp.dot((t * t).astype(BF16), e_ref[...], preferred_element_type=F32)
        return t * lax.rsqrt(ss * (1.0 / HEAD_DIM) + EPS) * g_ref[...]

    lo = lax.broadcasted_iota(jnp.int32, (x.shape[0], LANES), 1) < HEAD_DIM

    def store_q(q, q_ref):
        for j in range(Q_COLS // LANES):
            sl = slice(j * LANES, (j + 1) * LANES)
            q_ref[:, sl] = (rope(q[:, sl]) * (HEAD_DIM ** -0.5)).astype(BF16)

    def store_k(k, kd_ref):
        k = rope(k)
        r = pltpu.roll(k, HEAD_DIM, 1)
        kd_ref[:, 0:LANES] = jnp.where(lo, k, r).astype(BF16)
        kd_ref[:, LANES:2 * LANES] = jnp.where(lo, r, k).astype(BF16)

    c0 = 0
    qa = head_norm(z[:, c0:c0 + Q_COLS], eq_ref, gq_ref)
    store_q(qa, qa_ref)
    c0 += Q_COLS
    ka = head_norm(z[:, c0:c0 + KV_COLS], ek_ref, gk_ref)
    store_k(ka, kda_ref)
    c0 += KV_COLS
    vta_ref[...] = z[:, c0:c0 + KV_COLS].T.astype(BF16)
    c0 += KV_COLS
    store_q(z[:, c0:c0 + Q_COLS], qb_ref)
    c0 += Q_COLS
    store_k(z[:, c0:c0 + KV_COLS], kdb_ref)
    c0 += KV_COLS
    vtb_ref[...] = z[:, c0:c0 + KV_COLS].T.astype(BF16)


def _proj(h_all, mods, g_pre, w_in, gq, gk, eq, ek, cos_t, sa_t, sb_t):
    bsz, n_tok, _ = h_all.shape
    grid = (bsz, n_tok // TOK_TILE)
    const = lambda b, t: (0, 0)
    tok = lambda b, t: (b, t, 0)
    q_shape = jax.ShapeDtypeStruct((bsz, n_tok, Q_COLS), BF16)
    kd_shape = jax.ShapeDtypeStruct((bsz, n_tok, 2 * KV_COLS), BF16)
    vt_shape = jax.ShapeDtypeStruct((bsz, KV_COLS, n_tok), BF16)
    q_spec = pl.BlockSpec((None, TOK_TILE, Q_COLS), tok)
    kd_spec = pl.BlockSpec((None, TOK_TILE, 2 * KV_COLS), tok)
    vt_spec = pl.BlockSpec((None, KV_COLS, TOK_TILE), lambda b, t: (b, 0, t))
    rope_spec = pl.BlockSpec((TOK_TILE, LANES), lambda b, t: (t, 0))
    return pl.pallas_call(
        _proj_kernel,
        grid=grid,
        in_specs=[
            pl.BlockSpec((None, TOK_TILE, D_MODEL), tok),
            pl.BlockSpec((None, None, N_MOD, D_MODEL), lambda b, t: (b, jnp.minimum(t, 1), 0, 0)),
            pl.BlockSpec((1, D_MODEL), const),
            pl.BlockSpec((D_MODEL, IN_COLS), const),
            pl.BlockSpec((1, Q_COLS), const),
            pl.BlockSpec((1, KV_COLS), const),
            pl.BlockSpec((Q_COLS, Q_COLS), const),
            pl.BlockSpec((KV_COLS, KV_COLS), const),
            rope_spec, rope_spec, rope_spec,
        ],
        out_specs=[q_spec, kd_spec, vt_spec, q_spec, kd_spec, vt_spec],
        out_shape=[q_shape, kd_shape, vt_shape, q_shape, kd_shape, vt_shape],
        compiler_params=pltpu.CompilerParams(
            dimension_semantics=("parallel", "parallel"), vmem_limit_bytes=VMEM_LIMIT),
        name="proj",
    )(h_all, mods, g_pre, w_in, gq, gk, eq, ek, cos_t, sa_t, sb_t)


def _attn_kernel(*refs, n_ctx, n_full, band, use_sink, n_tok):
    if use_sink:
        q_ref, kd_ref, vt_ref, sink_ref, o_ref = refs
    else:
        q_ref, kd_ref, vt_ref, o_ref = refs
        sink_ref = None
    nq = GROUP * Q_BLOCK

    q = q_ref[...]
    lo = lax.broadcasted_iota(jnp.int32, (Q_BLOCK, LANES), 1) < HEAD_DIM
    zero = jnp.zeros((Q_BLOCK, LANES), BF16)
    parts = []
    for c in range(GROUP * HEAD_DIM // LANES):
        qc = q[:, c * LANES:(c + 1) * LANES]
        parts.append(jnp.where(lo, qc, zero))
        parts.append(jnp.where(lo, zero, qc))
    qm = jnp.concatenate(parts, axis=0)

    def step(carry, k_c, vt_c, bias=None):
        m, l, acc = carry
        s = lax.dot_general(k_c, qm, (((1,), (1,)), ((), ())),
                            preferred_element_type=F32)
        if bias is not None:
            s = s + bias
        m_new = jnp.maximum(m, jnp.max(s, axis=0, keepdims=True))
        alpha = jnp.exp(m - m_new)
        p = jnp.exp(s - m_new)
        l = alpha * l + jnp.sum(p, axis=0, keepdims=True)
        acc = alpha * acc + jnp.dot(vt_c, p.astype(BF16), preferred_element_type=F32)
        return m_new, l, acc

    if use_sink:
        carry = (sink_ref[...], jnp.ones((1, nq), F32), jnp.zeros((HEAD_DIM, nq), F32))
    else:
        carry = (jnp.full((1, nq), -jnp.inf, F32), jnp.zeros((1, nq), F32),
                 jnp.zeros((HEAD_DIM, nq), F32))

    for c0 in range(0, n_full, KEY_CHUNK):
        carry = step(carry, kd_ref[c0:c0 + KEY_CHUNK, :], vt_ref[:, c0:c0 + KEY_CHUNK])

    if band:
        i = pl.program_id(2)
        n_band = Q_BLOCK + 2 * WINDOW
        start = jnp.minimum(n_ctx + (i - 1) * Q_BLOCK, n_tok - n_band)
        start = pl.multiple_of(start, LANES)
        kj = lax.broadcasted_iota(jnp.int32, (n_band, nq), 0)
        qr = lax.broadcasted_iota(jnp.int32, (n_band, nq), 1) & (Q_BLOCK - 1)
        kpos = kj + (start - n_ctx)
        dist = kpos - (qr + i * Q_BLOCK)
        valid = (jnp.abs(dist) <= WINDOW) & (kpos >= 0)
        bias = jnp.where(valid, 0.0, NEG_BIG).astype(F32)
        carry = step(carry, kd_ref[pl.ds(start, n_band), :], vt_ref[:, pl.ds(start, n_band)], bias)

    _, l, acc = carry
    o_t = acc / l
    o_t = jnp.concatenate([o_t[:, g * Q_BLOCK:(g + 1) * Q_BLOCK] for g in range(GROUP)], axis=0)
    o_ref[...] = o_t.T.astype(BF16)


def _attention(q, kd, vt, sink_rows, *, q_blk_off, n_qblk, n_ctx, n_full, band, name):
    bsz, n_tok, _ = q.shape
    use_sink = sink_rows is not None
    in_specs = [
        pl.BlockSpec((None, Q_BLOCK, GROUP * HEAD_DIM), lambda b, h, i: (b, q_blk_off + i, h)),
        pl.BlockSpec((None, n_tok, 2 * HEAD_DIM), lambda b, h, i: (b, 0, h)),
        pl.BlockSpec((None, HEAD_DIM, n_tok), lambda b, h, i: (b, h, 0)),
    ]
    args = [q, kd, vt]
    if use_sink:
        in_specs.append(pl.BlockSpec((None, 1, GROUP * Q_BLOCK), lambda b, h, i: (h, 0, 0)))
        args.append(sink_rows)
    kern = functools.partial(_attn_kernel, n_ctx=n_ctx, n_full=n_full, band=band,
                             use_sink=use_sink, n_tok=n_tok)
    return pl.pallas_call(
        kern,
        grid=(bsz, N_KV, n_qblk),
        in_specs=in_specs,
        out_specs=pl.BlockSpec((None, Q_BLOCK, GROUP * HEAD_DIM), lambda b, h, i: (b, i, h)),
        out_shape=jax.ShapeDtypeStruct((bsz, n_qblk * Q_BLOCK, Q_COLS), BF16),
        compiler_params=pltpu.CompilerParams(
            dimension_semantics=("parallel", "parallel", "arbitrary"), vmem_limit_bytes=VMEM_LIMIT),
        name=name,
    )(*args)


def _post_kernel(h_ref, ma_ref, mb_ref, mod_ref, wout_ref, gpm_ref, gpre_ref, wup_ref, wdn_ref,
                 gpl_ref, o_ref):
    h = h_ref[...]
    gate_a = mod_ref[2:3, :]
    shift_m = mod_ref[3:4, :]
    scale_m = mod_ref[4:5, :]
    gate_m = mod_ref[5:6, :]
    mix = jnp.concatenate([ma_ref[...], mb_ref[...]], axis=1)
    mix = jnp.dot(mix, wout_ref[...], preferred_element_type=F32)
    h = h + gate_a * _rms(mix, gpm_ref[...])
    u = (_rms(h, gpre_ref[...]) * (1 + scale_m) + shift_m).astype(BF16)
    y = jnp.zeros_like(h)
    for c0 in range(0, D_FF, FF_CHUNK):
        a = jnp.dot(u, wup_ref[:, c0:c0 + FF_CHUNK], preferred_element_type=F32)
        a = jnp.square(jnp.maximum(a, 0.0)).astype(BF16)
        y = y + jnp.dot(a, wdn_ref[c0:c0 + FF_CHUNK, :], preferred_element_type=F32)
    o_ref[...] = h + gate_m * _rms(y, gpl_ref[...])


def _post(h_all, mix_a, mix_b, mods, w_out, g_post_mix, g_pre_mlp, w_up, w_down, g_post_mlp,
          *, tile_off, n_tiles):
    bsz = h_all.shape[0]
    const = lambda b, t: (0, 0)
    resident = functools.partial(pl.BlockSpec, index_map=const, pipeline_mode=pl.Buffered(1))
    rel = lambda b, t: (b, t, 0)
    return pl.pallas_call(
        _post_kernel,
        grid=(bsz, n_tiles),
        in_specs=[
            pl.BlockSpec((None, TOK_TILE, D_MODEL), lambda b, t: (b, tile_off + t, 0)),
            pl.BlockSpec((None, TOK_TILE, Q_COLS), rel),
            pl.BlockSpec((None, TOK_TILE, Q_COLS), rel),
            pl.BlockSpec((None, None, N_MOD, D_MODEL),
                         lambda b, t: (b, jnp.minimum(tile_off + t, 1), 0, 0)),
            resident((D_MODEL, D_MODEL)),
            pl.BlockSpec((1, D_MODEL), const),
            pl.BlockSpec((1, D_MODEL), const),
            resident((D_MODEL, D_FF)),
            resident((D_FF, D_MODEL)),
            pl.BlockSpec((1, D_MODEL), const),
        ],
        out_specs=pl.BlockSpec((None, TOK_TILE, D_MODEL), rel),
        out_shape=jax.ShapeDtypeStruct((bsz, n_tiles * TOK_TILE, D_MODEL), F32),
        compiler_params=pltpu.CompilerParams(
            dimension_semantics=("parallel", "parallel"), vmem_limit_bytes=VMEM_LIMIT),
        name="post",
    )(h_all, mix_a, mix_b, mods, w_out, g_post_mix, g_pre_mlp, w_up, w_down, g_post_mlp)


def _rope_tables(n_ctx, n_lat):
    rows = n_lat // GRID_W
    row_ids = jnp.repeat(jnp.arange(rows, dtype=jnp.int32), GRID_W).astype(F32)
    col_ids = jnp.tile(jnp.arange(GRID_W, dtype=jnp.int32), rows).astype(F32)
    axis_dim = HEAD_DIM // 2
    inv = ROPE_THETA ** (-jnp.arange(0, axis_dim, 2, dtype=F32) / axis_dim)
    ang_r = row_ids[:, None] * inv[None, :]
    ang_c = col_ids[:, None] * inv[None, :]
    cos_r, sin_r, cos_c, sin_c = jnp.cos(ang_r), jnp.sin(ang_r), jnp.cos(ang_c), jnp.sin(ang_c)
    zero = jnp.zeros_like(sin_r)
    cos_h = jnp.concatenate([cos_r, cos_r, cos_c, cos_c], axis=-1)
    sa_h = jnp.concatenate([-sin_r, zero, -sin_c, zero], axis=-1)
    sb_h = jnp.concatenate([zero, sin_r, zero, sin_c], axis=-1)

    def full(t, ctx_value):
        t = jnp.tile(t, (1, LANES // HEAD_DIM))
        return jnp.concatenate([jnp.full((n_ctx, LANES), ctx_value, F32), t], axis=0)

    return full(cos_h, 1.0), full(sa_h, 0.0), full(sb_h, 0.0)


def _head_ones(n_cols):
    head = jnp.arange(n_cols, dtype=jnp.int32) // HEAD_DIM
    return (head[:, None] == head[None, :]).astype(BF16)


def kernel(x, c, ctx, c_ctx, w_ada, b_ada, g_pre_mix, g_post_mix, g_pre_mlp, g_post_mlp,
           w_in, q_norm, k_norm, sink, w_out, w_up, w_down):
    bsz, n_lat, d = x.shape
    n_ctx = ctx.shape[1]
    depth = w_in.shape[0]
    n_tok = n_ctx + n_lat
    assert d == D_MODEL and n_ctx == TOK_TILE and n_lat % TOK_TILE == 0 and bsz + 1 <= ADA_ROWS
    assert w_in.shape[2] == IN_COLS and w_up.shape[2] == D_FF

    cc = jnp.concatenate([c, c_ctx[None, :], jnp.zeros((ADA_ROWS - bsz - 1, d), F32)], axis=0)
    mod_all = _ada(cc, w_ada, b_ada)
    mod_all = mod_all.reshape(depth, ADA_ROWS, N_MOD, d)
    mod_ctx = jnp.broadcast_to(mod_all[:, bsz:bsz + 1], (depth, bsz, N_MOD, d))
    mods = jnp.stack([mod_ctx, mod_all[:, :bsz]], axis=2)

    cos_t, sa_t, sb_t = _rope_tables(n_ctx, n_lat)
    eq = _head_ones(Q_COLS)
    ek = _head_ones(KV_COLS)
    row = lambda v: v.reshape(1, -1)

    w_in_b = w_in.astype(BF16)
    w_out_b = w_out.astype(BF16)
    w_up_b = w_up.astype(BF16)
    w_down_b = w_down.astype(BF16)

    ctx_tiles = n_ctx // TOK_TILE
    ctx_qblk = n_ctx // Q_BLOCK
    lat_qblk = n_lat // Q_BLOCK

    h_all = jnp.concatenate([ctx, x], axis=1)
    for l in range(depth):
        last = l == depth - 1
        gq = row(jnp.tile(q_norm[l], N_HEADS))
        gk = row(jnp.tile(k_norm[l], N_KV))
        sink_rows = jnp.repeat(sink[l].reshape(N_KV, 1, GROUP), Q_BLOCK, axis=2)

        qa, kda, vta, qb, kdb, vtb = _proj(h_all, mods[l], row(g_pre_mix[l]), w_in_b[l], gq, gk,
                                           eq, ek, cos_t, sa_t, sb_t)
        oa = _attention(qa, kda, vta, None, q_blk_off=ctx_qblk, n_qblk=lat_qblk, n_ctx=n_ctx,
                        n_full=n_tok, band=False, name="attn_global")
        ob = _attention(qb, kdb, vtb, sink_rows, q_blk_off=ctx_qblk, n_qblk=lat_qblk, n_ctx=n_ctx,
                        n_full=n_ctx, band=True, name="attn_window")
        if not last:
            oac = _attention(qa, kda, vta, None, q_blk_off=0, n_qblk=ctx_qblk, n_ctx=n_ctx,
                             n_full=n_ctx, band=False, name="attn_ctx_global")
            obc = _attention(qb, kdb, vtb, sink_rows, q_blk_off=0, n_qblk=ctx_qblk, n_ctx=n_ctx,
                             n_full=n_ctx, band=False, name="attn_ctx_sink")
            oa = jnp.concatenate([oac, oa], axis=1)
            ob = jnp.concatenate([obc, ob], axis=1)
            tile_off, n_tiles = 0, n_tok // TOK_TILE
        else:
            tile_off, n_tiles = ctx_tiles, n_lat // TOK_TILE
        h_all = _post(h_all, oa, ob, mods[l], w_out_b[l], row(g_post_mix[l]), row(g_pre_mlp[l]),
                      w_up_b[l], w_down_b[l], row(g_post_mlp[l]), tile_off=tile_off, n_tiles=n_tiles)
    return h_all
```

```python
import functools

import jax
import jax.numpy as jnp
from jax import lax
from jax.experimental import pallas as pl
from jax.experimental.pallas import tpu as pltpu

F32 = jnp.float32
BF16 = jnp.bfloat16

D_MODEL = 1024
HEAD_DIM = 64
N_HEADS = 8
N_KV = 2
GROUP = N_HEADS // N_KV
Q_COLS = N_HEADS * HEAD_DIM
KV_COLS = N_KV * HEAD_DIM
IN_COLS = 2 * (Q_COLS + 2 * KV_COLS)
D_FF = 4 * D_MODEL
GRID_W = 64
WINDOW = 128
Q_BLOCK = 128
ROPE_THETA = 10000.0
EPS = 1e-6
NEG_BIG = -1e30
N_MOD = 6
ROPE_SHIFT = HEAD_DIM // 4
LOG2_E = 1.4426950408889634
Q_SCALE = HEAD_DIM ** -0.5 * LOG2_E

LANES = 128
TOK_TILE = 256
KEY_CHUNK = 256
FF_CHUNK = 1024
ADA_ROWS = 32
ADA_COLS = 1536
VMEM_LIMIT = 56 * 1024 * 1024


def _rms(x, g):
    return x * lax.rsqrt(jnp.mean(x * x, axis=-1, keepdims=True) + EPS) * g


def _ada_kernel(c_ref, w_ref, b_ref, o_ref):
    c = c_ref[...]
    a = c * jax.nn.sigmoid(c)
    w = w_ref[...]
    a_hi = a.astype(BF16)
    a_lo = (a - a_hi.astype(F32)).astype(BF16)
    w_hi = w.astype(BF16)
    w_lo = (w - w_hi.astype(F32)).astype(BF16)
    acc = jnp.dot(a_hi, w_hi, preferred_element_type=F32)
    acc += jnp.dot(a_lo, w_hi, preferred_element_type=F32)
    acc += jnp.dot(a_hi, w_lo, preferred_element_type=F32)
    o_ref[...] = acc + b_ref[...]


def _ada(cc, w_ada, b_ada):
    depth = w_ada.shape[0]
    n_out = w_ada.shape[2]
    return pl.pallas_call(
        _ada_kernel,
        grid=(depth, n_out // ADA_COLS),
        in_specs=[
            pl.BlockSpec((ADA_ROWS, D_MODEL), lambda l, j: (0, 0)),
            pl.BlockSpec((None, D_MODEL, ADA_COLS), lambda l, j: (l, 0, j)),
            pl.BlockSpec((None, 1, ADA_COLS), lambda l, j: (l, 0, j)),
        ],
        out_specs=pl.BlockSpec((None, ADA_ROWS, ADA_COLS), lambda l, j: (l, 0, j)),
        out_shape=jax.ShapeDtypeStruct((depth, ADA_ROWS, n_out), F32),
        compiler_params=pltpu.CompilerParams(
            dimension_semantics=("arbitrary", "arbitrary"), vmem_limit_bytes=VMEM_LIMIT),
        name="ada",
    )(cc, w_ada, b_ada.reshape(depth, 1, n_out))


def _proj_kernel(h_ref, mod_ref, gpre_ref, w_ref, gq_ref, gk_ref, eq_ref, ek_ref,
                 cos_ref, sa_ref, sb_ref,
                 qa_ref, kda_ref, vta_ref, qb_ref, kdb_ref, vtb_ref):
    x = h_ref[...]
    shift = mod_ref[0:1, :]
    scale = mod_ref[1:2, :]
    u = (_rms(x, gpre_ref[...]) * (1 + scale) + shift).astype(BF16)
    z = jnp.dot(u, w_ref[...], preferred_element_type=F32)

    cos = cos_ref[...]
    sa = sa_ref[...]
    sb = sb_ref[...]

    def rope(t):
        up = pltpu.roll(t, LANES - ROPE_SHIFT, 1)
        dn = pltpu.roll(t, ROPE_SHIFT, 1)
        return t * cos + up * sa + dn * sb

    def head_norm(t, e_ref, g_ref):
        ss = jnp.dot((t * t).astype(BF16), e_ref[...], preferred_element_type=F32)
        return t * lax.rsqrt(ss * (1.0 / HEAD_DIM) + EPS) * g_ref[...]

    lo = lax.broadcasted_iota(jnp.int32, (x.shape[0], LANES), 1) < HEAD_DIM

    def store_q(q, q_ref):
        for j in range(Q_COLS // LANES):
            sl = slice(j * LANES, (j + 1) * LANES)
            q_ref[:, sl] = (rope(q[:, sl]) * Q_SCALE).astype(BF16)

    def store_k(k, kd_ref):
        k = rope(k)
        r = pltpu.roll(k, HEAD_DIM, 1)
        kd_ref[:, 0:LANES] = jnp.where(lo, k, r).astype(BF16)
        kd_ref[:, LANES:2 * LANES] = jnp.where(lo, r, k).astype(BF16)

    c0 = 0
    qa = head_norm(z[:, c0:c0 + Q_COLS], eq_ref, gq_ref)
    store_q(qa, qa_ref)
    c0 += Q_COLS
    ka = head_norm(z[:, c0:c0 + KV_COLS], ek_ref, gk_ref)
    store_k(ka, kda_ref)
    c0 += KV_COLS
    vta_ref[...] = z[:, c0:c0 + KV_COLS].T.astype(BF16)
    c0 += KV_COLS
    store_q(z[:, c0:c0 + Q_COLS], qb_ref)
    c0 += Q_COLS
    store_k(z[:, c0:c0 + KV_COLS], kdb_ref)
    c0 += KV_COLS
    vtb_ref[...] = z[:, c0:c0 + KV_COLS].T.astype(BF16)


def _proj(h_all, mods, g_pre, w_in, gq, gk, eq, ek, cos_t, sa_t, sb_t):
    bsz, n_tok, _ = h_all.shape
    grid = (bsz, n_tok // TOK_TILE)
    const = lambda b, t: (0, 0)
    tok = lambda b, t: (b, t, 0)
    q_shape = jax.ShapeDtypeStruct((bsz, n_tok, Q_COLS), BF16)
    kd_shape = jax.ShapeDtypeStruct((bsz, n_tok, 2 * KV_COLS), BF16)
    vt_shape = jax.ShapeDtypeStruct((bsz, KV_COLS, n_tok), BF16)
    q_spec = pl.BlockSpec((None, TOK_TILE, Q_COLS), tok)
    kd_spec = pl.BlockSpec((None, TOK_TILE, 2 * KV_COLS), tok)
    vt_spec = pl.BlockSpec((None, KV_COLS, TOK_TILE), lambda b, t: (b, 0, t))
    rope_spec = pl.BlockSpec((TOK_TILE, LANES), lambda b, t: (t, 0))
    return pl.pallas_call(
        _proj_kernel,
        grid=grid,
        in_specs=[
            pl.BlockSpec((None, TOK_TILE, D_MODEL), tok),
            pl.BlockSpec((None, None, N_MOD, D_MODEL), lambda b, t: (b, jnp.minimum(t, 1), 0, 0)),
            pl.BlockSpec((1, D_MODEL), const),
            pl.BlockSpec((D_MODEL, IN_COLS), const),
            pl.BlockSpec((1, Q_COLS), const),
            pl.BlockSpec((1, KV_COLS), const),
            pl.BlockSpec((Q_COLS, Q_COLS), const),
            pl.BlockSpec((KV_COLS, KV_COLS), const),
            rope_spec, rope_spec, rope_spec,
        ],
        out_specs=[q_spec, kd_spec, vt_spec, q_spec, kd_spec, vt_spec],
        out_shape=[q_shape, kd_shape, vt_shape, q_shape, kd_shape, vt_shape],
        compiler_params=pltpu.CompilerParams(
            dimension_semantics=("parallel", "parallel"), vmem_limit_bytes=VMEM_LIMIT),
        name="proj",
    )(h_all, mods, g_pre, w_in, gq, gk, eq, ek, cos_t, sa_t, sb_t)


def _attn_kernel(*refs, n_ctx, n_full, band, use_sink, n_tok):
    if use_sink:
        qcur_ref, qnext_ref, kd_ref, vt_ref, sink_ref, o_ref, s_a, s_b, mx_a, mx_b = refs
    else:
        qcur_ref, qnext_ref, kd_ref, vt_ref, o_ref, s_a, s_b, mx_a, mx_b = refs
        sink_ref = None
    nq = GROUP * Q_BLOCK
    n_band = Q_BLOCK + 2 * WINDOW if band else 0
    chunks = [(c0, min(c0 + KEY_CHUNK, n_full), False) for c0 in range(0, n_full, KEY_CHUNK)]
    if band:
        chunks.append((n_full, n_full + n_band, True))
    lo = lax.broadcasted_iota(jnp.int32, (Q_BLOCK, LANES), 1) < HEAD_DIM
    zero = jnp.zeros((Q_BLOCK, LANES), BF16)
    nt = (((1,), (1,)), ((), ()))

    def band_start(blk):
        start = jnp.minimum(n_ctx + (blk - 1) * Q_BLOCK, n_tok - n_band)
        return pl.multiple_of(start, LANES)

    def rows8(x):
        return x.reshape(x.shape[0] // 8, 8, nq)

    def extend_q(q):
        parts = []
        for c in range(GROUP * HEAD_DIM // LANES):
            qc = q[:, c * LANES:(c + 1) * LANES]
            parts.append(jnp.where(lo, qc, zero))
            parts.append(jnp.where(lo, zero, qc))
        return jnp.concatenate(parts, axis=0)

    def scores_chunk(qm, blk, chunk, s_ref, mx):
        c0, c1, in_band = chunk
        if in_band:
            start = band_start(blk)
            kj = lax.broadcasted_iota(jnp.int32, (n_band, nq), 0)
            qr = lax.broadcasted_iota(jnp.int32, (n_band, nq), 1) & (Q_BLOCK - 1)
            kpos = kj + (start - n_ctx)
            dist = kpos - (qr + blk * Q_BLOCK)
            valid = (jnp.abs(dist) <= WINDOW) & (kpos >= 0)
            bias = jnp.where(valid, 0.0, NEG_BIG).astype(F32)
            s = lax.dot_general(kd_ref[pl.ds(start, n_band), :], qm, nt,
                                preferred_element_type=F32) + bias
        else:
            s = lax.dot_general(kd_ref[c0:c1, :], qm, nt, preferred_element_type=F32)
        s_ref[c0:c1, :] = s
        cm = jnp.max(rows8(s), axis=0)
        return cm if mx is None else jnp.maximum(mx, cm)

    def finish_chunk(blk, chunk, s_ref, m, l8, acc):
        c0, c1, in_band = chunk
        p = jnp.exp2(s_ref[c0:c1, :] - m)
        l8 = l8 + jnp.sum(rows8(p), axis=0)
        if in_band:
            vt_c = vt_ref[:, pl.ds(band_start(blk), n_band)]
        else:
            vt_c = vt_ref[:, c0:c1]
        return l8, acc + jnp.dot(vt_c, p.astype(BF16), preferred_element_type=F32)

    def finish_and_scores(blk, s_ref, mx_ref, q_next, s_next, mx_next):
        m = jnp.max(mx_ref[...], axis=0, keepdims=True)
        if use_sink:
            m = jnp.maximum(m, sink_ref[...])
        l8 = jnp.zeros((8, nq), F32)
        acc = jnp.zeros((HEAD_DIM, nq), F32)
        qm = extend_q(q_next)
        mx = None
        for chunk in chunks:
            l8, acc = finish_chunk(blk, chunk, s_ref, m, l8, acc)
            mx = scores_chunk(qm, blk + 1, chunk, s_next, mx)
        mx_next[...] = mx
        l = jnp.sum(l8, axis=0, keepdims=True)
        if use_sink:
            l = l + jnp.exp2(sink_ref[...] - m)
        o_t = acc / l
        o_t = jnp.concatenate([o_t[:, g * Q_BLOCK:(g + 1) * Q_BLOCK] for g in range(GROUP)], axis=0)
        return o_t.T.astype(BF16)

    blk0 = 2 * pl.program_id(2)

    @pl.when(pl.program_id(2) == 0)
    def _():
        qm = extend_q(qcur_ref[0:Q_BLOCK, :])
        mx = None
        for chunk in chunks:
            mx = scores_chunk(qm, blk0, chunk, s_a, mx)
        mx_a[...] = mx

    o_ref[0:Q_BLOCK, :] = finish_and_scores(blk0, s_a, mx_a, qcur_ref[Q_BLOCK:2 * Q_BLOCK, :], s_b, mx_b)
    o_ref[Q_BLOCK:2 * Q_BLOCK, :] = finish_and_scores(blk0 + 1, s_b, mx_b, qnext_ref[...], s_a, mx_a)


def _attention(q, kd, vt, sink_rows, *, q_blk_off, n_qblk, n_ctx, n_full, band, name):
    bsz, n_tok, _ = q.shape
    assert n_qblk % 2 == 0 and q_blk_off % 2 == 0
    use_sink = sink_rows is not None
    n_keys = n_full + (Q_BLOCK + 2 * WINDOW if band else 0)
    nq = GROUP * Q_BLOCK
    in_specs = [
        pl.BlockSpec((None, 2 * Q_BLOCK, GROUP * HEAD_DIM), lambda b, h, j: (b, q_blk_off // 2 + j, h)),
        pl.BlockSpec((None, Q_BLOCK, GROUP * HEAD_DIM),
                     lambda b, h, j: (b, q_blk_off + jnp.minimum(2 * j + 2, n_qblk - 1), h)),
        pl.BlockSpec((None, n_tok, 2 * HEAD_DIM), lambda b, h, j: (b, 0, h)),
        pl.BlockSpec((None, HEAD_DIM, n_tok), lambda b, h, j: (b, h, 0)),
    ]
    args = [q, q, kd, vt]
    if use_sink:
        in_specs.append(pl.BlockSpec((None, 1, nq), lambda b, h, j: (h, 0, 0)))
        args.append(sink_rows)
    kern = functools.partial(_attn_kernel, n_ctx=n_ctx, n_full=n_full, band=band,
                             use_sink=use_sink, n_tok=n_tok)
    return pl.pallas_call(
        kern,
        grid=(bsz, N_KV, n_qblk // 2),
        in_specs=in_specs,
        out_specs=pl.BlockSpec((None, 2 * Q_BLOCK, GROUP * HEAD_DIM), lambda b, h, j: (b, j, h)),
        out_shape=jax.ShapeDtypeStruct((bsz, n_qblk * Q_BLOCK, Q_COLS), BF16),
        scratch_shapes=[pltpu.VMEM((n_keys, nq), F32), pltpu.VMEM((n_keys, nq), F32),
                        pltpu.VMEM((8, nq), F32), pltpu.VMEM((8, nq), F32)],
        compiler_params=pltpu.CompilerParams(
            dimension_semantics=("parallel", "parallel", "arbitrary"), vmem_limit_bytes=VMEM_LIMIT),
        name=name,
    )(*args)


def _post_kernel(h_ref, ma_ref, mb_ref, mod_ref, wout_ref, gpm_ref, gpre_ref, wup_ref, wdn_ref,
                 gpl_ref, o_ref):
    h = h_ref[...]
    gate_a = mod_ref[2:3, :]
    shift_m = mod_ref[3:4, :]
    scale_m = mod_ref[4:5, :]
    gate_m = mod_ref[5:6, :]
    mix = jnp.concatenate([ma_ref[...], mb_ref[...]], axis=1)
    mix = jnp.dot(mix, wout_ref[...], preferred_element_type=F32)
    h = h + gate_a * _rms(mix, gpm_ref[...])
    u = (_rms(h, gpre_ref[...]) * (1 + scale_m) + shift_m).astype(BF16)
    y = jnp.zeros_like(h)
    for c0 in range(0, D_FF, FF_CHUNK):
        a = jnp.dot(u, wup_ref[:, c0:c0 + FF_CHUNK], preferred_element_type=F32)
        a = jnp.square(jnp.maximum(a, 0.0)).astype(BF16)
        y = y + jnp.dot(a, wdn_ref[c0:c0 + FF_CHUNK, :], preferred_element_type=F32)
    o_ref[...] = h + gate_m * _rms(y, gpl_ref[...])


def _post(h_all, mix_a, mix_b, mods, w_out, g_post_mix, g_pre_mlp, w_up, w_down, g_post_mlp,
          *, tile_off, n_tiles):
    bsz = h_all.shape[0]
    const = lambda b, t: (0, 0)
    resident = functools.partial(pl.BlockSpec, index_map=const, pipeline_mode=pl.Buffered(1))
    rel = lambda b, t: (b, t, 0)
    return pl.pallas_call(
        _post_kernel,
        grid=(bsz, n_tiles),
        in_specs=[
            pl.BlockSpec((None, TOK_TILE, D_MODEL), lambda b, t: (b, tile_off + t, 0)),
            pl.BlockSpec((None, TOK_TILE, Q_COLS), rel),
            pl.BlockSpec((None, TOK_TILE, Q_COLS), rel),
            pl.BlockSpec((None, None, N_MOD, D_MODEL),
                         lambda b, t: (b, jnp.minimum(tile_off + t, 1), 0, 0)),
            resident((D_MODEL, D_MODEL)),
            pl.BlockSpec((1, D_MODEL), const),
            pl.BlockSpec((1, D_MODEL), const),
            resident((D_MODEL, D_FF)),
            resident((D_FF, D_MODEL)),
            pl.BlockSpec((1, D_MODEL), const),
        ],
        out_specs=pl.BlockSpec((None, TOK_TILE, D_MODEL), rel),
        out_shape=jax.ShapeDtypeStruct((bsz, n_tiles * TOK_TILE, D_MODEL), F32),
        compiler_params=pltpu.CompilerParams(
            dimension_semantics=("parallel", "parallel"), vmem_limit_bytes=VMEM_LIMIT),
        name="post",
    )(h_all, mix_a, mix_b, mods, w_out, g_post_mix, g_pre_mlp, w_up, w_down, g_post_mlp)


def _rope_tables(n_ctx, n_lat):
    rows = n_lat // GRID_W
    row_ids = jnp.repeat(jnp.arange(rows, dtype=jnp.int32), GRID_W).astype(F32)
    col_ids = jnp.tile(jnp.arange(GRID_W, dtype=jnp.int32), rows).astype(F32)
    axis_dim = HEAD_DIM // 2
    inv = ROPE_THETA ** (-jnp.arange(0, axis_dim, 2, dtype=F32) / axis_dim)
    ang_r = row_ids[:, None] * inv[None, :]
    ang_c = col_ids[:, None] * inv[None, :]
    cos_r, sin_r, cos_c, sin_c = jnp.cos(ang_r), jnp.sin(ang_r), jnp.cos(ang_c), jnp.sin(ang_c)
    zero = jnp.zeros_like(sin_r)
    cos_h = jnp.concatenate([cos_r, cos_r, cos_c, cos_c], axis=-1)
    sa_h = jnp.concatenate([-sin_r, zero, -sin_c, zero], axis=-1)
    sb_h = jnp.concatenate([zero, sin_r, zero, sin_c], axis=-1)

    def full(t, ctx_value):
        t = jnp.tile(t, (1, LANES // HEAD_DIM))
        return jnp.concatenate([jnp.full((n_ctx, LANES), ctx_value, F32), t], axis=0)

    return full(cos_h, 1.0), full(sa_h, 0.0), full(sb_h, 0.0)


def _head_ones(n_cols):
    head = jnp.arange(n_cols, dtype=jnp.int32) // HEAD_DIM
    return (head[:, None] == head[None, :]).astype(BF16)


def kernel(x, c, ctx, c_ctx, w_ada, b_ada, g_pre_mix, g_post_mix, g_pre_mlp, g_post_mlp,
           w_in, q_norm, k_norm, sink, w_out, w_up, w_down):
    bsz, n_lat, d = x.shape
    n_ctx = ctx.shape[1]
    depth = w_in.shape[0]
    n_tok = n_ctx + n_lat
    assert d == D_MODEL and n_ctx == TOK_TILE and n_lat % TOK_TILE == 0 and bsz + 1 <= ADA_ROWS
    assert w_in.shape[2] == IN_COLS and w_up.shape[2] == D_FF

    cc = jnp.concatenate([c, c_ctx[None, :], jnp.zeros((ADA_ROWS - bsz - 1, d), F32)], axis=0)
    mod_all = _ada(cc, w_ada, b_ada)
    mod_all = mod_all.reshape(depth, ADA_ROWS, N_MOD, d)
    mod_ctx = jnp.broadcast_to(mod_all[:, bsz:bsz + 1], (depth, bsz, N_MOD, d))
    mods = jnp.stack([mod_ctx, mod_all[:, :bsz]], axis=2)

    cos_t, sa_t, sb_t = _rope_tables(n_ctx, n_lat)
    eq = _head_ones(Q_COLS)
    ek = _head_ones(KV_COLS)
    row = lambda v: v.reshape(1, -1)

    w_in_b = w_in.astype(BF16)
    w_out_b = w_out.astype(BF16)
    w_up_b = w_up.astype(BF16)
    w_down_b = w_down.astype(BF16)

    ctx_tiles = n_ctx // TOK_TILE
    ctx_qblk = n_ctx // Q_BLOCK
    lat_qblk = n_lat // Q_BLOCK

    h_all = jnp.concatenate([ctx, x], axis=1)
    for l in range(depth):
        last = l == depth - 1
        gq = row(jnp.tile(q_norm[l], N_HEADS))
        gk = row(jnp.tile(k_norm[l], N_KV))
        sink_rows = jnp.repeat(sink[l].reshape(N_KV, 1, GROUP) * LOG2_E, Q_BLOCK, axis=2)

        qa, kda, vta, qb, kdb, vtb = _proj(h_all, mods[l], row(g_pre_mix[l]), w_in_b[l], gq, gk,
                                           eq, ek, cos_t, sa_t, sb_t)
        oa = _attention(qa, kda, vta, None, q_blk_off=ctx_qblk, n_qblk=lat_qblk, n_ctx=n_ctx,
                        n_full=n_tok, band=False, name="attn_global")
        ob = _attention(qb, kdb, vtb, sink_rows, q_blk_off=ctx_qblk, n_qblk=lat_qblk, n_ctx=n_ctx,
                        n_full=n_ctx, band=True, name="attn_window")
        if not last:
            oac = _attention(qa, kda, vta, None, q_blk_off=0, n_qblk=ctx_qblk, n_ctx=n_ctx,
                             n_full=n_ctx, band=False, name="attn_ctx_global")
            obc = _attention(qb, kdb, vtb, sink_rows, q_blk_off=0, n_qblk=ctx_qblk, n_ctx=n_ctx,
                             n_full=n_ctx, band=False, name="attn_ctx_sink")
            oa = jnp.concatenate([oac, oa], axis=1)
            ob = jnp.concatenate([obc, ob], axis=1)
            tile_off, n_tiles = 0, n_tok // TOK_TILE
        else:
            tile_off, n_tiles = ctx_tiles, n_lat // TOK_TILE
        h_all = _post(h_all, oa, ob, mods[l], w_out_b[l], row(g_post_mix[l]), row(g_pre_mlp[l]),
                      w_up_b[l], w_down_b[l], row(g_post_mlp[l]), tile_off=tile_off, n_tiles=n_tiles)
    return h_all
```

```python
import functools

import jax
import jax.numpy as jnp
from jax import lax
from jax.experimental import pallas as pl
from jax.experimental.pallas import tpu as pltpu

F32 = jnp.float32
BF16 = jnp.bfloat16

D_MODEL = 1024
HEAD_DIM = 64
N_HEADS = 8
N_KV = 2
GROUP = N_HEADS // N_KV
Q_COLS = N_HEADS * HEAD_DIM
KV_COLS = N_KV * HEAD_DIM
IN_COLS = 2 * (Q_COLS + 2 * KV_COLS)
D_FF = 4 * D_MODEL
GRID_W = 64
WINDOW = 128
Q_BLOCK = 128
ROPE_THETA = 10000.0
EPS = 1e-6
NEG_BIG = -1e30
N_MOD = 6
ROPE_SHIFT = HEAD_DIM // 4
LOG2_E = 1.4426950408889634
Q_SCALE = HEAD_DIM ** -0.5 * LOG2_E

LANES = 128
TOK_TILE = 256
VT_ROWS = HEAD_DIM + 16
KEY_CHUNK = 768
Q_PER_STEP = 2
LOGIT_BOUND_LIMIT = 48.0
BOUND_SLACK = 1.01
FF_CHUNK = 1024
ADA_ROWS = 32
ADA_COLS = 1536
VMEM_LIMIT = 56 * 1024 * 1024


def _rms(x, g):
    return x * lax.rsqrt(jnp.mean(x * x, axis=-1, keepdims=True) + EPS) * g


def _ada_kernel(c_ref, w_ref, b_ref, o_ref):
    c = c_ref[...]
    a = c * jax.nn.sigmoid(c)
    w = w_ref[...]
    a_hi = a.astype(BF16)
    a_lo = (a - a_hi.astype(F32)).astype(BF16)
    w_hi = w.astype(BF16)
    w_lo = (w - w_hi.astype(F32)).astype(BF16)
    acc = jnp.dot(a_hi, w_hi, preferred_element_type=F32)
    acc += jnp.dot(a_lo, w_hi, preferred_element_type=F32)
    acc += jnp.dot(a_hi, w_lo, preferred_element_type=F32)
    o_ref[...] = acc + b_ref[...]


def _ada(cc, w_ada, b_ada):
    depth = w_ada.shape[0]
    n_out = w_ada.shape[2]
    return pl.pallas_call(
        _ada_kernel,
        grid=(depth, n_out // ADA_COLS),
        in_specs=[
            pl.BlockSpec((ADA_ROWS, D_MODEL), lambda l, j: (0, 0)),
            pl.BlockSpec((None, D_MODEL, ADA_COLS), lambda l, j: (l, 0, j)),
            pl.BlockSpec((None, 1, ADA_COLS), lambda l, j: (l, 0, j)),
        ],
        out_specs=pl.BlockSpec((None, ADA_ROWS, ADA_COLS), lambda l, j: (l, 0, j)),
        out_shape=jax.ShapeDtypeStruct((depth, ADA_ROWS, n_out), F32),
        compiler_params=pltpu.CompilerParams(
            dimension_semantics=("arbitrary", "arbitrary"), vmem_limit_bytes=VMEM_LIMIT),
        name="ada",
    )(cc, w_ada, b_ada.reshape(depth, 1, n_out))


def _proj_kernel(h_ref, mod_ref, gpre_ref, w_ref, gq_ref, gk_ref, eq_ref, ek_ref,
                 cos_ref, sa_ref, sb_ref,
                 qa_ref, kda_ref, vta_ref, qb_ref, kdb_ref, vtb_ref):
    x = h_ref[...]
    shift = mod_ref[0:1, :]
    scale = mod_ref[1:2, :]
    u = (_rms(x, gpre_ref[...]) * (1 + scale) + shift).astype(BF16)
    z = jnp.dot(u, w_ref[...], preferred_element_type=F32)

    cos = cos_ref[...]
    sa = sa_ref[...]
    sb = sb_ref[...]

    def rope(t):
        up = pltpu.roll(t, LANES - ROPE_SHIFT, 1)
        dn = pltpu.roll(t, ROPE_SHIFT, 1)
        return t * cos + up * sa + dn * sb

    def head_norm(t, e_ref, g_ref):
        ss = jnp.dot((t * t).astype(BF16), e_ref[...], preferred_element_type=F32)
        return t * lax.rsqrt(ss * (1.0 / HEAD_DIM) + EPS) * g_ref[...]

    lo = lax.broadcasted_iota(jnp.int32, (x.shape[0], LANES), 1) < HEAD_DIM

    def store_q(q, q_ref):
        for j in range(Q_COLS // LANES):
            sl = slice(j * LANES, (j + 1) * LANES)
            q_ref[:, sl] = (rope(q[:, sl]) * Q_SCALE).astype(BF16)

    def store_k(k, kd_ref):
        k = rope(k)
        r = pltpu.roll(k, HEAD_DIM, 1)
        kd_ref[:, 0:LANES] = jnp.where(lo, k, r).astype(BF16)
        kd_ref[:, LANES:2 * LANES] = jnp.where(lo, r, k).astype(BF16)

    def store_vt(v, vt_ref):
        v_t = v.T.astype(BF16)
        ones = jnp.ones((VT_ROWS - HEAD_DIM, v.shape[0]), BF16)
        for h in range(N_KV):
            vt_ref[h, 0:HEAD_DIM, :] = v_t[h * HEAD_DIM:(h + 1) * HEAD_DIM, :]
            vt_ref[h, HEAD_DIM:VT_ROWS, :] = ones

    c0 = 0
    qa = head_norm(z[:, c0:c0 + Q_COLS], eq_ref, gq_ref)
    store_q(qa, qa_ref)
    c0 += Q_COLS
    ka = head_norm(z[:, c0:c0 + KV_COLS], ek_ref, gk_ref)
    store_k(ka, kda_ref)
    c0 += KV_COLS
    store_vt(z[:, c0:c0 + KV_COLS], vta_ref)
    c0 += KV_COLS
    store_q(z[:, c0:c0 + Q_COLS], qb_ref)
    c0 += Q_COLS
    store_k(z[:, c0:c0 + KV_COLS], kdb_ref)
    c0 += KV_COLS
    store_vt(z[:, c0:c0 + KV_COLS], vtb_ref)


def _proj(h_all, mods, g_pre, w_in, gq, gk, eq, ek, cos_t, sa_t, sb_t):
    bsz, n_tok, _ = h_all.shape
    grid = (bsz, n_tok // TOK_TILE)
    const = lambda b, t: (0, 0)
    tok = lambda b, t: (b, t, 0)
    q_shape = jax.ShapeDtypeStruct((bsz, n_tok, Q_COLS), BF16)
    kd_shape = jax.ShapeDtypeStruct((bsz, n_tok, 2 * KV_COLS), BF16)
    vt_shape = jax.ShapeDtypeStruct((bsz, N_KV, VT_ROWS, n_tok), BF16)
    q_spec = pl.BlockSpec((None, TOK_TILE, Q_COLS), tok)
    kd_spec = pl.BlockSpec((None, TOK_TILE, 2 * KV_COLS), tok)
    vt_spec = pl.BlockSpec((None, N_KV, VT_ROWS, TOK_TILE), lambda b, t: (b, 0, 0, t))
    rope_spec = pl.BlockSpec((TOK_TILE, LANES), lambda b, t: (t, 0))
    return pl.pallas_call(
        _proj_kernel,
        grid=grid,
        in_specs=[
            pl.BlockSpec((None, TOK_TILE, D_MODEL), tok),
            pl.BlockSpec((None, None, N_MOD, D_MODEL), lambda b, t: (b, jnp.minimum(t, 1), 0, 0)),
            pl.BlockSpec((1, D_MODEL), const),
            pl.BlockSpec((D_MODEL, IN_COLS), const),
            pl.BlockSpec((1, Q_COLS), const),
            pl.BlockSpec((1, KV_COLS), const),
            pl.BlockSpec((Q_COLS, Q_COLS), const),
            pl.BlockSpec((KV_COLS, KV_COLS), const),
            rope_spec, rope_spec, rope_spec,
        ],
        out_specs=[q_spec, kd_spec, vt_spec, q_spec, kd_spec, vt_spec],
        out_shape=[q_shape, kd_shape, vt_shape, q_shape, kd_shape, vt_shape],
        compiler_params=pltpu.CompilerParams(
            dimension_semantics=("parallel", "parallel"), vmem_limit_bytes=VMEM_LIMIT),
        name="proj",
    )(h_all, mods, g_pre, w_in, gq, gk, eq, ek, cos_t, sa_t, sb_t)


def _band_start(blk, n_ctx, n_tok):
    return jnp.minimum(n_ctx + (blk - 1) * Q_BLOCK, n_tok - (Q_BLOCK + 2 * WINDOW))


def _window_bias_variants(n_ctx, n_tok, n_lat_blk):
    n_band = Q_BLOCK + 2 * WINDOW
    kj = jnp.arange(n_band, dtype=jnp.int32)[:, None]
    qr = jnp.arange(GROUP * Q_BLOCK, dtype=jnp.int32)[None, :] % Q_BLOCK
    variants = []
    for blk in (0, 1, n_lat_blk - 1):
        kpos = kj + (_band_start(blk, n_ctx, n_tok) - n_ctx)
        qpos = qr + blk * Q_BLOCK
        valid = (jnp.abs(kpos - qpos) <= WINDOW) & (kpos >= 0)
        variants.append(jnp.where(valid, 0.0, NEG_BIG).astype(F32))
    return jnp.stack(variants)


def _attn_kernel(*refs, n_ctx, n_full, band, use_sink, n_tok, n_lat_blk, q_row0):
    refs = list(refs)
    q_ref, kd_ref, vt_ref, eh_ref = refs[:4]
    o_ref, bound_ref = refs[-2:]
    extra = refs[4:-2]
    sink_ref = extra.pop(0) if use_sink else None
    bias_ref = extra.pop(0) if band else None
    nq = GROUP * Q_BLOCK
    n_band = Q_BLOCK + 2 * WINDOW if band else 0
    chunks = [(c0, min(c0 + KEY_CHUNK, n_full), False) for c0 in range(0, n_full, KEY_CHUNK)]
    if band:
        chunks.append((n_full, n_full + n_band, True))
    lo = lax.broadcasted_iota(jnp.int32, (Q_BLOCK, LANES), 1) < HEAD_DIM
    zero = jnp.zeros((Q_BLOCK, LANES), BF16)
    nt = (((1,), (1,)), ((), ()))
    step = pl.program_id(2)

    @pl.when(step == 0)
    def _():
        q = q_ref[...].astype(F32)
        qq = jnp.dot((q * q).astype(BF16), eh_ref[...], preferred_element_type=F32)
        k = kd_ref[...].astype(F32)
        kk = jnp.dot((k * k).astype(BF16), eh_ref[0:LANES, 0:LANES], preferred_element_type=F32)
        bound_ref[0] = jnp.sqrt(jnp.max(qq) * jnp.max(kk)) * BOUND_SLACK

    def band_start(blk):
        return pl.multiple_of(_band_start(blk, n_ctx, n_tok), LANES)

    def extend_q(blk):
        q = q_ref[pl.ds(pl.multiple_of(q_row0 + blk * Q_BLOCK, Q_BLOCK), Q_BLOCK), :]
        parts = []
        for c in range(GROUP * HEAD_DIM // LANES):
            qc = q[:, c * LANES:(c + 1) * LANES]
            parts.append(jnp.where(lo, qc, zero))
            parts.append(jnp.where(lo, zero, qc))
        return jnp.concatenate(parts, axis=0)

    def logits(qm, blk, chunk):
        c0, c1, in_band = chunk
        if in_band:
            variant = jnp.where(blk == 0, 0, jnp.where(blk >= n_lat_blk - 1, 2, 1))
            return lax.dot_general(kd_ref[pl.ds(band_start(blk), n_band), :], qm, nt,
                                   preferred_element_type=F32) + bias_ref[variant]
        return lax.dot_general(kd_ref[c0:c1, :], qm, nt, preferred_element_type=F32)

    def values_t(blk, chunk):
        c0, c1, in_band = chunk
        if in_band:
            return vt_ref[:, pl.ds(band_start(blk), n_band)]
        return vt_ref[:, c0:c1]

    def column_max(qm, blk):
        mx = None
        for chunk in chunks:
            s = logits(qm, blk, chunk)
            cm = jnp.max(s.reshape(s.shape[0] // 8, 8, nq), axis=0)
            mx = cm if mx is None else jnp.maximum(mx, cm)
        return jnp.max(mx, axis=0, keepdims=True)

    def attend(qm, blk, m):
        acc = jnp.zeros((VT_ROWS, nq), F32)
        s_next = logits(qm, blk, chunks[0])
        pv_prev = None
        for c, chunk in enumerate(chunks):
            s = s_next
            if c + 1 < len(chunks):
                s_next = logits(qm, blk, chunks[c + 1])
            p = jnp.exp2(s - m).astype(BF16)
            pv = jnp.dot(values_t(blk, chunk), p, preferred_element_type=F32)
            if pv_prev is not None:
                acc = acc + pv_prev
            pv_prev = pv
        acc = acc + pv_prev
        l = acc[HEAD_DIM:HEAD_DIM + 1, :]
        if use_sink:
            l = l + jnp.exp2(sink_ref[...] - m)
        o_t = acc[0:HEAD_DIM, :] / l
        o_t = jnp.concatenate([o_t[:, g * Q_BLOCK:(g + 1) * Q_BLOCK] for g in range(GROUP)], axis=0)
        return o_t.T.astype(BF16)

    def with_sink(m):
        return jnp.maximum(m, sink_ref[...]) if use_sink else m

    bound = bound_ref[0]
    one_pass = bound <= LOGIT_BOUND_LIMIT

    @pl.when(one_pass)
    def _():
        m = with_sink(jnp.full((1, nq), bound, F32))
        for t in range(Q_PER_STEP):
            blk = step * Q_PER_STEP + t
            o_ref[t * Q_BLOCK:(t + 1) * Q_BLOCK, :] = attend(extend_q(blk), blk, m)

    @pl.when(jnp.logical_not(one_pass))
    def _():
        for t in range(Q_PER_STEP):
            blk = step * Q_PER_STEP + t
            qm = extend_q(blk)
            o_ref[t * Q_BLOCK:(t + 1) * Q_BLOCK, :] = attend(qm, blk, with_sink(column_max(qm, blk)))


def _attention(q, kd, vt, sink_rows, head_ones, *, q_blk_off, n_qblk, n_ctx, n_full, band, name):
    bsz, n_tok, _ = q.shape
    assert n_qblk % Q_PER_STEP == 0
    use_sink = sink_rows is not None
    nq = GROUP * Q_BLOCK
    in_specs = [
        pl.BlockSpec((None, n_tok, GROUP * HEAD_DIM), lambda b, h, j: (b, 0, h)),
        pl.BlockSpec((None, n_tok, 2 * HEAD_DIM), lambda b, h, j: (b, 0, h)),
        pl.BlockSpec((None, None, VT_ROWS, n_tok), lambda b, h, j: (b, h, 0, 0)),
        pl.BlockSpec(head_ones.shape, lambda b, h, j: (0, 0)),
    ]
    args = [q, kd, vt, head_ones]
    if use_sink:
        in_specs.append(pl.BlockSpec((None, 1, nq), lambda b, h, j: (h, 0, 0)))
        args.append(sink_rows)
    if band:
        bias = _window_bias_variants(n_ctx, n_tok, n_qblk)
        in_specs.append(pl.BlockSpec(bias.shape, lambda b, h, j: (0, 0, 0)))
        args.append(bias)
    kern = functools.partial(_attn_kernel, n_ctx=n_ctx, n_full=n_full, band=band, use_sink=use_sink,
                             n_tok=n_tok, n_lat_blk=n_qblk, q_row0=q_blk_off * Q_BLOCK)
    rows = Q_PER_STEP * Q_BLOCK
    return pl.pallas_call(
        kern,
        grid=(bsz, N_KV, n_qblk // Q_PER_STEP),
        in_specs=in_specs,
        out_specs=pl.BlockSpec((None, rows, GROUP * HEAD_DIM), lambda b, h, j: (b, j, h)),
        out_shape=jax.ShapeDtypeStruct((bsz, n_qblk * Q_BLOCK, Q_COLS), BF16),
        scratch_shapes=[pltpu.SMEM((1,), F32)],
        compiler_params=pltpu.CompilerParams(
            dimension_semantics=("parallel", "parallel", "arbitrary"), vmem_limit_bytes=VMEM_LIMIT),
        name=name,
    )(*args)


def _post_kernel(h_ref, ma_ref, mb_ref, mod_ref, wout_ref, gpm_ref, gpre_ref, wup_ref, wdn_ref,
                 gpl_ref, o_ref):
    h = h_ref[...]
    gate_a = mod_ref[2:3, :]
    shift_m = mod_ref[3:4, :]
    scale_m = mod_ref[4:5, :]
    gate_m = mod_ref[5:6, :]
    mix = jnp.concatenate([ma_ref[...], mb_ref[...]], axis=1)
    mix = jnp.dot(mix, wout_ref[...], preferred_element_type=F32)
    h = h + gate_a * _rms(mix, gpm_ref[...])
    u = (_rms(h, gpre_ref[...]) * (1 + scale_m) + shift_m).astype(BF16)
    y = jnp.zeros_like(h)
    for c0 in range(0, D_FF, FF_CHUNK):
        a = jnp.dot(u, wup_ref[:, c0:c0 + FF_CHUNK], preferred_element_type=F32)
        a = jnp.square(jnp.maximum(a, 0.0)).astype(BF16)
        y = y + jnp.dot(a, wdn_ref[c0:c0 + FF_CHUNK, :], preferred_element_type=F32)
    o_ref[...] = h + gate_m * _rms(y, gpl_ref[...])


def _post(h_all, mix_a, mix_b, mods, w_out, g_post_mix, g_pre_mlp, w_up, w_down, g_post_mlp,
          *, tile_off, n_tiles):
    bsz = h_all.shape[0]
    const = lambda b, t: (0, 0)
    resident = functools.partial(pl.BlockSpec, index_map=const, pipeline_mode=pl.Buffered(1))
    rel = lambda b, t: (b, t, 0)
    return pl.pallas_call(
        _post_kernel,
        grid=(bsz, n_tiles),
        in_specs=[
            pl.BlockSpec((None, TOK_TILE, D_MODEL), lambda b, t: (b, tile_off + t, 0)),
            pl.BlockSpec((None, TOK_TILE, Q_COLS), rel),
            pl.BlockSpec((None, TOK_TILE, Q_COLS), rel),
            pl.BlockSpec((None, None, N_MOD, D_MODEL),
                         lambda b, t: (b, jnp.minimum(tile_off + t, 1), 0, 0)),
            resident((D_MODEL, D_MODEL)),
            pl.BlockSpec((1, D_MODEL), const),
            pl.BlockSpec((1, D_MODEL), const),
            resident((D_MODEL, D_FF)),
            resident((D_FF, D_MODEL)),
            pl.BlockSpec((1, D_MODEL), const),
        ],
        out_specs=pl.BlockSpec((None, TOK_TILE, D_MODEL), rel),
        out_shape=jax.ShapeDtypeStruct((bsz, n_tiles * TOK_TILE, D_MODEL), F32),
        compiler_params=pltpu.CompilerParams(
            dimension_semantics=("parallel", "parallel"), vmem_limit_bytes=VMEM_LIMIT),
        name="post",
    )(h_all, mix_a, mix_b, mods, w_out, g_post_mix, g_pre_mlp, w_up, w_down, g_post_mlp)


def _rope_tables(n_ctx, n_lat):
    rows = n_lat // GRID_W
    row_ids = jnp.repeat(jnp.arange(rows, dtype=jnp.int32), GRID_W).astype(F32)
    col_ids = jnp.tile(jnp.arange(GRID_W, dtype=jnp.int32), rows).astype(F32)
    axis_dim = HEAD_DIM // 2
    inv = ROPE_THETA ** (-jnp.arange(0, axis_dim, 2, dtype=F32) / axis_dim)
    ang_r = row_ids[:, None] * inv[None, :]
    ang_c = col_ids[:, None] * inv[None, :]
    cos_r, sin_r, cos_c, sin_c = jnp.cos(ang_r), jnp.sin(ang_r), jnp.cos(ang_c), jnp.sin(ang_c)
    zero = jnp.zeros_like(sin_r)
    cos_h = jnp.concatenate([cos_r, cos_r, cos_c, cos_c], axis=-1)
    sa_h = jnp.concatenate([-sin_r, zero, -sin_c, zero], axis=-1)
    sb_h = jnp.concatenate([zero, sin_r, zero, sin_c], axis=-1)

    def full(t, ctx_value):
        t = jnp.tile(t, (1, LANES // HEAD_DIM))
        return jnp.concatenate([jnp.full((n_ctx, LANES), ctx_value, F32), t], axis=0)

    return full(cos_h, 1.0), full(sa_h, 0.0), full(sb_h, 0.0)


def _head_ones(n_cols):
    head = jnp.arange(n_cols, dtype=jnp.int32) // HEAD_DIM
    return (head[:, None] == head[None, :]).astype(BF16)


def kernel(x, c, ctx, c_ctx, w_ada, b_ada, g_pre_mix, g_post_mix, g_pre_mlp, g_post_mlp,
           w_in, q_norm, k_norm, sink, w_out, w_up, w_down):
    bsz, n_lat, d = x.shape
    n_ctx = ctx.shape[1]
    depth = w_in.shape[0]
    n_tok = n_ctx + n_lat
    assert d == D_MODEL and n_ctx == TOK_TILE and n_lat % TOK_TILE == 0 and bsz + 1 <= ADA_ROWS
    assert w_in.shape[2] == IN_COLS and w_up.shape[2] == D_FF

    cc = jnp.concatenate([c, c_ctx[None, :], jnp.zeros((ADA_ROWS - bsz - 1, d), F32)], axis=0)
    mod_all = _ada(cc, w_ada, b_ada)
    mod_all = mod_all.reshape(depth, ADA_ROWS, N_MOD, d)
    mod_ctx = jnp.broadcast_to(mod_all[:, bsz:bsz + 1], (depth, bsz, N_MOD, d))
    mods = jnp.stack([mod_ctx, mod_all[:, :bsz]], axis=2)

    cos_t, sa_t, sb_t = _rope_tables(n_ctx, n_lat)
    eq = _head_ones(Q_COLS)
    ek = _head_ones(KV_COLS)
    eh = _head_ones(GROUP * HEAD_DIM)
    row = lambda v: v.reshape(1, -1)

    w_in_b = w_in.astype(BF16)
    w_out_b = w_out.astype(BF16)
    w_up_b = w_up.astype(BF16)
    w_down_b = w_down.astype(BF16)

    ctx_tiles = n_ctx // TOK_TILE
    ctx_qblk = n_ctx // Q_BLOCK
    lat_qblk = n_lat // Q_BLOCK

    h_all = jnp.concatenate([ctx, x], axis=1)
    for l in range(depth):
        last = l == depth - 1
        gq = row(jnp.tile(q_norm[l], N_HEADS))
        gk = row(jnp.tile(k_norm[l], N_KV))
        sink_rows = jnp.repeat(sink[l].reshape(N_KV, 1, GROUP) * LOG2_E, Q_BLOCK, axis=2)

        qa, kda, vta, qb, kdb, vtb = _proj(h_all, mods[l], row(g_pre_mix[l]), w_in_b[l], gq, gk,
                                           eq, ek, cos_t, sa_t, sb_t)
        oa = _attention(qa, kda, vta, None, eh, q_blk_off=ctx_qblk, n_qblk=lat_qblk, n_ctx=n_ctx,
                        n_full=n_tok, band=False, name="attn_global")
        ob = _attention(qb, kdb, vtb, sink_rows, eh, q_blk_off=ctx_qblk, n_qblk=lat_qblk, n_ctx=n_ctx,
                        n_full=n_ctx, band=True, name="attn_window")
        if not last:
            oac = _attention(qa, kda, vta, None, eh, q_blk_off=0, n_qblk=ctx_qblk, n_ctx=n_ctx,
                             n_full=n_ctx, band=False, name="attn_ctx_global")
            obc = _attention(qb, kdb, vtb, sink_rows, eh, q_blk_off=0, n_qblk=ctx_qblk, n_ctx=n_ctx,
                             n_full=n_ctx, band=False, name="attn_ctx_sink")
            oa = jnp.concatenate([oac, oa], axis=1)
            ob = jnp.concatenate([obc, ob], axis=1)
            tile_off, n_tiles = 0, n_tok // TOK_TILE
        else:
            tile_off, n_tiles = ctx_tiles, n_lat // TOK_TILE
        h_all = _post(h_all, oa, ob, mods[l], w_out_b[l], row(g_post_mix[l]), row(g_pre_mlp[l]),
                      w_up_b[l], w_down_b[l], row(g_post_mlp[l]), tile_off=tile_off, n_tiles=n_tiles)
    return h_all
```

```python
import functools

import jax
import jax.numpy as jnp
from jax import lax
from jax.experimental import pallas as pl
from jax.experimental.pallas import tpu as pltpu

F32 = jnp.float32
BF16 = jnp.bfloat16

D_MODEL = 1024
HEAD_DIM = 64
N_HEADS = 8
N_KV = 2
GROUP = N_HEADS // N_KV
Q_COLS = N_HEADS * HEAD_DIM
KV_COLS = N_KV * HEAD_DIM
IN_COLS = 2 * (Q_COLS + 2 * KV_COLS)
D_FF = 4 * D_MODEL
GRID_W = 64
WINDOW = 128
Q_BLOCK = 128
ROPE_THETA = 10000.0
EPS = 1e-6
NEG_BIG = -1e30
N_MOD = 6
ROPE_SHIFT = HEAD_DIM // 4
LOG2_E = 1.4426950408889634
Q_SCALE = HEAD_DIM ** -0.5 * LOG2_E

LANES = 128
TOK_TILE = 256
PROJ_SUB = 3
POST_SUBS = (1, 2, 3)
VT_ROWS = HEAD_DIM + 16
KEY_CHUNK = 768
Q_PER_STEP_GLOBAL = 4
Q_PER_STEP_WINDOW = 8
LOGIT_BOUND_LIMIT = 48.0
BOUND_SLACK = 1.01
FF_CHUNK = 1024
ADA_ROWS = 32
ADA_COLS = 1536
VMEM_LIMIT = 56 * 1024 * 1024


def _rms(x, g):
    return x * lax.rsqrt(jnp.mean(x * x, axis=-1, keepdims=True) + EPS) * g


def _ada_kernel(c_ref, w_ref, b_ref, o_ref):
    c = c_ref[...]
    a = c * jax.nn.sigmoid(c)
    w = w_ref[...]
    a_hi = a.astype(BF16)
    a_lo = (a - a_hi.astype(F32)).astype(BF16)
    w_hi = w.astype(BF16)
    w_lo = (w - w_hi.astype(F32)).astype(BF16)
    acc = jnp.dot(a_hi, w_hi, preferred_element_type=F32)
    acc += jnp.dot(a_lo, w_hi, preferred_element_type=F32)
    acc += jnp.dot(a_hi, w_lo, preferred_element_type=F32)
    o_ref[...] = acc + b_ref[...]


def _ada(cc, w_ada, b_ada):
    depth = w_ada.shape[0]
    n_out = w_ada.shape[2]
    return pl.pallas_call(
        _ada_kernel,
        grid=(depth, n_out // ADA_COLS),
        in_specs=[
            pl.BlockSpec((ADA_ROWS, D_MODEL), lambda l, j: (0, 0)),
            pl.BlockSpec((None, D_MODEL, ADA_COLS), lambda l, j: (l, 0, j)),
            pl.BlockSpec((None, 1, ADA_COLS), lambda l, j: (l, 0, j)),
        ],
        out_specs=pl.BlockSpec((None, ADA_ROWS, ADA_COLS), lambda l, j: (l, 0, j)),
        out_shape=jax.ShapeDtypeStruct((depth, ADA_ROWS, n_out), F32),
        compiler_params=pltpu.CompilerParams(
            dimension_semantics=("arbitrary", "arbitrary"), vmem_limit_bytes=VMEM_LIMIT),
        name="ada",
    )(cc, w_ada, b_ada.reshape(depth, 1, n_out))


def _stream_specs(n_sub, block_cols, lat_tile_off, has_ctx):
    specs = []
    for i in range(n_sub):
        if has_ctx:
            idx = lambda b, t, i=i: (b, jnp.maximum(n_sub * t + i - 1, 0), 0)
        else:
            idx = lambda b, t, i=i: (b, lat_tile_off + n_sub * t + i, 0)
        specs.append(pl.BlockSpec((None, TOK_TILE, block_cols), idx))
    if has_ctx:
        specs.append(pl.BlockSpec((None, TOK_TILE, block_cols), lambda b, t: (b, 0, 0)))
    return specs


def _sub_tile(refs, ctx_ref, i):
    v = refs[i][...]
    if ctx_ref is not None and i == 0:
        v = jnp.where(pl.program_id(1) == 0, ctx_ref[...], v)
    return v


def _sub_mod(mod_ref, ctx_first, i):
    if ctx_first and i == 0:
        return jnp.where(pl.program_id(1) == 0, mod_ref[0], mod_ref[1])
    return mod_ref[1]


def _proj_kernel(*refs, n_sub, has_ctx):
    h_refs = refs[:n_sub]
    refs = refs[n_sub:]
    ctx_ref = None
    if has_ctx:
        ctx_ref, refs = refs[0], refs[1:]
    (mod_ref, gpre_ref, w_ref, gq_ref, gk_ref, eq_ref, ek_ref, cos_ref, sa_ref, sb_ref,
     qa_ref, kda_ref, vta_ref, qb_ref, kdb_ref, vtb_ref) = refs
    lo = lax.broadcasted_iota(jnp.int32, (TOK_TILE, LANES), 1) < HEAD_DIM
    kv = 2 * KV_COLS

    def prenorm(i):
        mod = _sub_mod(mod_ref, True, i)
        x = _sub_tile(h_refs, ctx_ref, i)
        return (_rms(x, gpre_ref[...]) * (1 + mod[1:2, :]) + mod[0:1, :]).astype(BF16)

    def stages(i, u):
        rows = slice(i * TOK_TILE, (i + 1) * TOK_TILE)
        cos = cos_ref[rows, :]
        sa = sa_ref[rows, :]
        sb = sb_ref[rows, :]

        def rope(t):
            up = pltpu.roll(t, LANES - ROPE_SHIFT, 1)
            dn = pltpu.roll(t, ROPE_SHIFT, 1)
            return t * cos + up * sa + dn * sb

        def head_norm(t, e_ref, g_ref):
            ss = jnp.dot((t * t).astype(BF16), e_ref[...], preferred_element_type=F32)
            return t * lax.rsqrt(ss * (1.0 / HEAD_DIM) + EPS) * g_ref[...]

        def store_q(q, q_ref):
            for j in range(Q_COLS // LANES):
                sl = slice(j * LANES, (j + 1) * LANES)
                q_ref[rows, sl] = (rope(q[:, sl]) * Q_SCALE).astype(BF16)

        def store_k(k, kd_ref):
            k = rope(k)
            r = pltpu.roll(k, HEAD_DIM, 1)
            kd_ref[rows, 0:LANES] = jnp.where(lo, k, r).astype(BF16)
            kd_ref[rows, LANES:2 * LANES] = jnp.where(lo, r, k).astype(BF16)

        def store_vt(v, vt_ref):
            v_t = v.T.astype(BF16)
            ones = jnp.ones((VT_ROWS - HEAD_DIM, TOK_TILE), BF16)
            for h in range(N_KV):
                vt_ref[h, 0:HEAD_DIM, rows] = v_t[h * HEAD_DIM:(h + 1) * HEAD_DIM, :]
                vt_ref[h, HEAD_DIM:VT_ROWS, rows] = ones

        def project(c0, n):
            return lambda: jnp.dot(u, w_ref[:, c0:c0 + n], preferred_element_type=F32)

        def epi_qa(z):
            store_q(head_norm(z, eq_ref, gq_ref), qa_ref)

        def epi_kva(z):
            store_k(head_norm(z[:, 0:KV_COLS], ek_ref, gk_ref), kda_ref)
            store_vt(z[:, KV_COLS:kv], vta_ref)

        def epi_qb(z):
            store_q(z, qb_ref)

        def epi_kvb(z):
            store_k(z[:, 0:KV_COLS], kdb_ref)
            store_vt(z[:, KV_COLS:kv], vtb_ref)

        return [(project(0, Q_COLS), epi_qa), (project(Q_COLS, kv), epi_kva),
                (project(Q_COLS + kv, Q_COLS), epi_qb), (project(2 * Q_COLS + kv, kv), epi_kvb)]

    u_next = prenorm(0)
    pending = []
    for i in range(n_sub):
        u = u_next
        results = []
        for g, (matmul, epilogue) in enumerate(stages(i, u)):
            results.append((epilogue, matmul()))
            if g == 0 and i + 1 < n_sub:
                u_next = prenorm(i + 1)
            if pending:
                epi, z = pending.pop(0)
                epi(z)
        for epi, z in pending:
            epi(z)
        pending = results
    for epi, z in pending:
        epi(z)


def _proj(h_lat, h_ctx, lat_tile_off, n_tiles, mods, g_pre, w_in, gq, gk, eq, ek, cos_t, sa_t, sb_t):
    bsz = h_lat.shape[0]
    has_ctx = h_ctx is not None
    n_sub = PROJ_SUB
    assert n_tiles % n_sub == 0
    n_tok = n_tiles * TOK_TILE
    rows = n_sub * TOK_TILE
    const = lambda b, t: (0, 0)
    tok = lambda b, t: (b, t, 0)
    q_shape = jax.ShapeDtypeStruct((bsz, n_tok, Q_COLS), BF16)
    kd_shape = jax.ShapeDtypeStruct((bsz, n_tok, 2 * KV_COLS), BF16)
    vt_shape = jax.ShapeDtypeStruct((bsz, N_KV, VT_ROWS, n_tok), BF16)
    q_spec = pl.BlockSpec((None, rows, Q_COLS), tok)
    kd_spec = pl.BlockSpec((None, rows, 2 * KV_COLS), tok)
    vt_spec = pl.BlockSpec((None, N_KV, VT_ROWS, rows), lambda b, t: (b, 0, 0, t))
    rope_spec = pl.BlockSpec((rows, LANES), lambda b, t: (t, 0))
    h_args = [h_lat] * n_sub + ([h_ctx] if has_ctx else [])
    return pl.pallas_call(
        functools.partial(_proj_kernel, n_sub=n_sub, has_ctx=has_ctx),
        grid=(bsz, n_tiles // n_sub),
        in_specs=_stream_specs(n_sub, D_MODEL, lat_tile_off, has_ctx) + [
            pl.BlockSpec((None, 2, N_MOD, D_MODEL), lambda b, t: (b, 0, 0, 0)),
            pl.BlockSpec((1, D_MODEL), const),
            pl.BlockSpec((D_MODEL, IN_COLS), const),
            pl.BlockSpec((1, Q_COLS), const),
            pl.BlockSpec((1, KV_COLS), const),
            pl.BlockSpec((Q_COLS, Q_COLS), const),
            pl.BlockSpec((KV_COLS, KV_COLS), const),
            rope_spec, rope_spec, rope_spec,
        ],
        out_specs=[q_spec, kd_spec, vt_spec, q_spec, kd_spec, vt_spec],
        out_shape=[q_shape, kd_shape, vt_shape, q_shape, kd_shape, vt_shape],
        compiler_params=pltpu.CompilerParams(
            dimension_semantics=("parallel", "parallel"), vmem_limit_bytes=VMEM_LIMIT),
        name="proj",
    )(*h_args, mods, g_pre, w_in, gq, gk, eq, ek, cos_t, sa_t, sb_t)


def _band_start(blk, n_ctx, n_tok):
    return jnp.minimum(n_ctx + (blk - 1) * Q_BLOCK, n_tok - (Q_BLOCK + 2 * WINDOW))


def _window_bias_variants(n_ctx, n_tok, n_lat_blk):
    n_band = Q_BLOCK + 2 * WINDOW
    kj = jnp.arange(n_band, dtype=jnp.int32)[:, None]
    qr = jnp.arange(GROUP * Q_BLOCK, dtype=jnp.int32)[None, :] % Q_BLOCK
    variants = []
    for blk in (0, 1, n_lat_blk - 1):
        kpos = kj + (_band_start(blk, n_ctx, n_tok) - n_ctx)
        qpos = qr + blk * Q_BLOCK
        valid = (jnp.abs(kpos - qpos) <= WINDOW) & (kpos >= 0)
        variants.append(jnp.where(valid, 0.0, NEG_BIG).astype(F32))
    return jnp.stack(variants)


def _attn_kernel(*refs, n_ctx, n_full, band, use_sink, n_tok, n_lat_blk, q_row0, q_per_step,
                 given_bound):
    refs = list(refs)
    q_ref, kd_ref, vt_ref, eh_ref = refs[:4]
    rest = refs[4:]
    sink_ref = rest.pop(0) if use_sink else None
    bias_ref = rest.pop(0) if band else None
    if given_bound:
        bound_ref, o_ref = rest
    else:
        o_ref, bound_ref = rest
    nq = GROUP * Q_BLOCK
    n_band = Q_BLOCK + 2 * WINDOW if band else 0
    chunks = [(c0, min(c0 + KEY_CHUNK, n_full), False) for c0 in range(0, n_full, KEY_CHUNK)]
    if band:
        chunks.append((n_full, n_full + n_band, True))
    lo = lax.broadcasted_iota(jnp.int32, (Q_BLOCK, LANES), 1) < HEAD_DIM
    zero = jnp.zeros((Q_BLOCK, LANES), BF16)
    nt = (((1,), (1,)), ((), ()))
    step = pl.program_id(2)

    if not given_bound:
        @pl.when(step == 0)
        def _():
            q = q_ref[...].astype(F32)
            qq = jnp.dot((q * q).astype(BF16), eh_ref[...], preferred_element_type=F32)
            k = kd_ref[...].astype(F32)
            kk = jnp.dot((k * k).astype(BF16), eh_ref[0:LANES, 0:LANES], preferred_element_type=F32)
            bound_ref[0] = jnp.sqrt(jnp.max(qq) * jnp.max(kk)) * BOUND_SLACK

    def band_start(blk):
        return pl.multiple_of(_band_start(blk, n_ctx, n_tok), LANES)

    def extend_q(blk):
        q = q_ref[pl.ds(pl.multiple_of(q_row0 + blk * Q_BLOCK, Q_BLOCK), Q_BLOCK), :]
        parts = []
        for c in range(GROUP * HEAD_DIM // LANES):
            qc = q[:, c * LANES:(c + 1) * LANES]
            parts.append(jnp.where(lo, qc, zero))
            parts.append(jnp.where(lo, zero, qc))
        return jnp.concatenate(parts, axis=0)

    def logits(qm, blk, chunk):
        c0, c1, in_band = chunk
        if in_band:
            variant = jnp.where(blk == 0, 0, jnp.where(blk >= n_lat_blk - 1, 2, 1))
            return lax.dot_general(kd_ref[pl.ds(band_start(blk), n_band), :], qm, nt,
                                   preferred_element_type=F32) + bias_ref[variant]
        return lax.dot_general(kd_ref[c0:c1, :], qm, nt, preferred_element_type=F32)

    def values_t(blk, chunk):
        c0, c1, in_band = chunk
        if in_band:
            return vt_ref[:, pl.ds(band_start(blk), n_band)]
        return vt_ref[:, c0:c1]

    def column_max(qm, blk):
        mx = None
        for chunk in chunks:
            s = logits(qm, blk, chunk)
            cm = jnp.max(s.reshape(s.shape[0] // 8, 8, nq), axis=0)
            mx = cm if mx is None else jnp.maximum(mx, cm)
        return jnp.max(mx, axis=0, keepdims=True)

    def finalize(t, acc, m):
        l = acc[HEAD_DIM:HEAD_DIM + 1, :]
        if use_sink:
            l = l + jnp.exp2(sink_ref[...] - m)
        o_t = acc[0:HEAD_DIM, :] / l
        o_t = jnp.concatenate([o_t[:, g * Q_BLOCK:(g + 1) * Q_BLOCK] for g in range(GROUP)], axis=0)
        o_ref[t * Q_BLOCK:(t + 1) * Q_BLOCK, :] = o_t.T.astype(BF16)

    def attend_all(qms, shifts):
        items = [(t, c) for t in range(q_per_step) for c in range(len(chunks))]
        blk_of = lambda t: step * q_per_step + t
        s_next = logits(qms[0], blk_of(0), chunks[0])
        acc = [None] * q_per_step
        pv_prev = None

        def retire(prev):
            t, c, pv = prev
            acc[t] = pv if acc[t] is None else acc[t] + pv
            if c == len(chunks) - 1:
                finalize(t, acc[t], shifts[t])

        for n, (t, c) in enumerate(items):
            s = s_next
            if n + 1 < len(items):
                t2, c2 = items[n + 1]
                s_next = logits(qms[t2], blk_of(t2), chunks[c2])
            p = jnp.exp2(s - shifts[t]).astype(BF16)
            pv = jnp.dot(values_t(blk_of(t), chunks[c]), p, preferred_element_type=F32)
            if pv_prev is not None:
                retire(pv_prev)
            pv_prev = (t, c, pv)
        retire(pv_prev)

    def with_sink(m):
        return jnp.maximum(m, sink_ref[...]) if use_sink else m

    bound = bound_ref[0]
    one_pass = bound <= LOGIT_BOUND_LIMIT

    @pl.when(one_pass)
    def _():
        m = with_sink(jnp.full((1, nq), bound, F32))
        qms = [extend_q(step * q_per_step + t) for t in range(q_per_step)]
        attend_all(qms, [m] * q_per_step)

    @pl.when(jnp.logical_not(one_pass))
    def _():
        qms = [extend_q(step * q_per_step + t) for t in range(q_per_step)]
        shifts = [with_sink(column_max(qms[t], step * q_per_step + t)) for t in range(q_per_step)]
        attend_all(qms, shifts)


def _attention(q, kd, vt, sink_rows, head_ones, logit_bound, *, q_blk_off, n_qblk, n_ctx, n_full, band,
               q_per_step, name):
    bsz, n_tok, _ = q.shape
    q_per_step = min(q_per_step, n_qblk)
    assert n_qblk % q_per_step == 0
    use_sink = sink_rows is not None
    nq = GROUP * Q_BLOCK
    in_specs = [
        pl.BlockSpec((None, n_tok, GROUP * HEAD_DIM), lambda b, h, j: (b, 0, h)),
        pl.BlockSpec((None, n_tok, 2 * HEAD_DIM), lambda b, h, j: (b, 0, h)),
        pl.BlockSpec((None, None, VT_ROWS, n_tok), lambda b, h, j: (b, h, 0, 0)),
        pl.BlockSpec(head_ones.shape, lambda b, h, j: (0, 0)),
    ]
    args = [q, kd, vt, head_ones]
    if use_sink:
        in_specs.append(pl.BlockSpec((None, 1, nq), lambda b, h, j: (h, 0, 0)))
        args.append(sink_rows)
    if band:
        bias = _window_bias_variants(n_ctx, n_tok, n_qblk)
        in_specs.append(pl.BlockSpec(bias.shape, lambda b, h, j: (0, 0, 0)))
        args.append(bias)
    given_bound = logit_bound is not None
    if given_bound:
        in_specs.append(pl.BlockSpec(memory_space=pltpu.SMEM))
        args.append(logit_bound)
    kern = functools.partial(_attn_kernel, n_ctx=n_ctx, n_full=n_full, band=band, use_sink=use_sink,
                             n_tok=n_tok, n_lat_blk=n_qblk, q_row0=q_blk_off * Q_BLOCK,
                             q_per_step=q_per_step, given_bound=given_bound)
    rows = q_per_step * Q_BLOCK
    return pl.pallas_call(
        kern,
        grid=(bsz, N_KV, n_qblk // q_per_step),
        in_specs=in_specs,
        out_specs=pl.BlockSpec((None, rows, GROUP * HEAD_DIM), lambda b, h, j: (b, j, h)),
        out_shape=jax.ShapeDtypeStruct((bsz, n_qblk * Q_BLOCK, Q_COLS), BF16),
        scratch_shapes=[] if given_bound else [pltpu.SMEM((1,), F32)],
        compiler_params=pltpu.CompilerParams(
            dimension_semantics=("parallel", "parallel", "arbitrary"), vmem_limit_bytes=VMEM_LIMIT),
        name=name,
    )(*args)


def _post_kernel(*refs, n_sub, has_ctx):
    n_in = n_sub + (1 if has_ctx else 0)
    h_refs, ma_refs, mb_refs = refs[:n_sub], refs[n_in:n_in + n_sub], refs[2 * n_in:2 * n_in + n_sub]
    ctx_h = refs[n_sub] if has_ctx else None
    ctx_ma = refs[n_in + n_sub] if has_ctx else None
    ctx_mb = refs[2 * n_in + n_sub] if has_ctx else None
    mod_ref, wout_ref, gpm_ref, gpre_ref, wup_ref, wdn_ref, gpl_ref, o_ref = refs[3 * n_in:]
    mods = [_sub_mod(mod_ref, has_ctx, i) for i in range(n_sub)]

    def out_proj(i):
        mix = jnp.concatenate([_sub_tile(ma_refs, ctx_ma, i), _sub_tile(mb_refs, ctx_mb, i)], axis=1)
        return jnp.dot(mix, wout_ref[...], preferred_element_type=F32)

    def mixer_residual(i, mix):
        mod = mods[i]
        h = _sub_tile(h_refs, ctx_h, i) + mod[2:3, :] * _rms(mix, gpm_ref[...])
        u = (_rms(h, gpre_ref[...]) * (1 + mod[4:5, :]) + mod[3:4, :]).astype(BF16)
        return h, u

    def mlp_chunk(u, y, c0):
        a = jnp.dot(u, wup_ref[:, c0:c0 + FF_CHUNK], preferred_element_type=F32)
        a = jnp.square(jnp.maximum(a, 0.0)).astype(BF16)
        part = jnp.dot(a, wdn_ref[c0:c0 + FF_CHUNK, :], preferred_element_type=F32)
        return part if y is None else y + part

    def mlp_residual(i, h, y):
        o_ref[i * TOK_TILE:(i + 1) * TOK_TILE, :] = h + mods[i][5:6, :] * _rms(y, gpl_ref[...])

    hu = [None] * n_sub
    mix_prev = None
    for i in range(n_sub):
        mix = out_proj(i)
        if i > 0:
            hu[i - 1] = mixer_residual(i - 1, mix_prev)
        mix_prev = mix
    for i in range(n_sub):
        y = None
        for c, c0 in enumerate(range(0, D_FF, FF_CHUNK)):
            if i == 0 and c == 0:
                hu[n_sub - 1] = mixer_residual(n_sub - 1, mix_prev)
            y = mlp_chunk(hu[i][1], y, c0)
            if i > 0 and c == 0:
                mlp_residual(i - 1, hu[i - 1][0], y_prev)
        y_prev = y
    mlp_residual(n_sub - 1, hu[n_sub - 1][0], y_prev)


def _post(h_lat, h_ctx, lat_tile_off, n_tiles, mix_a, mix_b, ctx_mix_a, ctx_mix_b, mods, w_out,
          g_post_mix, g_pre_mlp, w_up, w_down, g_post_mlp):
    bsz = h_lat.shape[0]
    has_ctx = h_ctx is not None
    n_sub = max(s for s in POST_SUBS if n_tiles % s == 0)
    const = lambda b, t: (0, 0)
    resident = functools.partial(pl.BlockSpec, index_map=const, pipeline_mode=pl.Buffered(1))
    ctx_args = lambda v: [v] if has_ctx else []
    args = ([h_lat] * n_sub + ctx_args(h_ctx) + [mix_a] * n_sub + ctx_args(ctx_mix_a)
            + [mix_b] * n_sub + ctx_args(ctx_mix_b))
    return pl.pallas_call(
        functools.partial(_post_kernel, n_sub=n_sub, has_ctx=has_ctx),
        grid=(bsz, n_tiles // n_sub),
        in_specs=(_stream_specs(n_sub, D_MODEL, lat_tile_off, has_ctx)
                  + _stream_specs(n_sub, Q_COLS, 0, has_ctx)
                  + _stream_specs(n_sub, Q_COLS, 0, has_ctx) + [
            pl.BlockSpec((None, 2, N_MOD, D_MODEL), lambda b, t: (b, 0, 0, 0)),
            resident((D_MODEL, D_MODEL)),
            pl.BlockSpec((1, D_MODEL), const),
            pl.BlockSpec((1, D_MODEL), const),
            resident((D_MODEL, D_FF)),
            resident((D_FF, D_MODEL)),
            pl.BlockSpec((1, D_MODEL), const),
        ]),
        out_specs=pl.BlockSpec((None, n_sub * TOK_TILE, D_MODEL), lambda b, t: (b, t, 0)),
        out_shape=jax.ShapeDtypeStruct((bsz, n_tiles * TOK_TILE, D_MODEL), F32),
        compiler_params=pltpu.CompilerParams(
            dimension_semantics=("parallel", "parallel"), vmem_limit_bytes=VMEM_LIMIT),
        name="post",
    )(*args, mods, w_out, g_post_mix, g_pre_mlp, w_up, w_down, g_post_mlp)


def _rope_tables(n_ctx, n_lat):
    rows = n_lat // GRID_W
    row_ids = jnp.repeat(jnp.arange(rows, dtype=jnp.int32), GRID_W).astype(F32)
    col_ids = jnp.tile(jnp.arange(GRID_W, dtype=jnp.int32), rows).astype(F32)
    axis_dim = HEAD_DIM // 2
    inv = ROPE_THETA ** (-jnp.arange(0, axis_dim, 2, dtype=F32) / axis_dim)
    ang_r = row_ids[:, None] * inv[None, :]
    ang_c = col_ids[:, None] * inv[None, :]
    cos_r, sin_r, cos_c, sin_c = jnp.cos(ang_r), jnp.sin(ang_r), jnp.cos(ang_c), jnp.sin(ang_c)
    zero = jnp.zeros_like(sin_r)
    cos_h = jnp.concatenate([cos_r, cos_r, cos_c, cos_c], axis=-1)
    sa_h = jnp.concatenate([-sin_r, zero, -sin_c, zero], axis=-1)
    sb_h = jnp.concatenate([zero, sin_r, zero, sin_c], axis=-1)

    def full(t, ctx_value):
        t = jnp.tile(t, (1, LANES // HEAD_DIM))
        return jnp.concatenate([jnp.full((n_ctx, LANES), ctx_value, F32), t], axis=0)

    return full(cos_h, 1.0), full(sa_h, 0.0), full(sb_h, 0.0)


def _head_ones(n_cols):
    head = jnp.arange(n_cols, dtype=jnp.int32) // HEAD_DIM
    return (head[:, None] == head[None, :]).astype(BF16)


def kernel(x, c, ctx, c_ctx, w_ada, b_ada, g_pre_mix, g_post_mix, g_pre_mlp, g_post_mlp,
           w_in, q_norm, k_norm, sink, w_out, w_up, w_down):
    bsz, n_lat, d = x.shape
    n_ctx = ctx.shape[1]
    depth = w_in.shape[0]
    n_tok = n_ctx + n_lat
    assert d == D_MODEL and n_ctx == TOK_TILE and n_lat % TOK_TILE == 0 and bsz + 1 <= ADA_ROWS
    assert w_in.shape[2] == IN_COLS and w_up.shape[2] == D_FF

    cc = jnp.concatenate([c, c_ctx[None, :], jnp.zeros((ADA_ROWS - bsz - 1, d), F32)], axis=0)
    mod_all = _ada(cc, w_ada, b_ada)
    mod_all = mod_all.reshape(depth, ADA_ROWS, N_MOD, d)
    mod_ctx = jnp.broadcast_to(mod_all[:, bsz:bsz + 1], (depth, bsz, N_MOD, d))
    mods = jnp.stack([mod_ctx, mod_all[:, :bsz]], axis=2)

    cos_t, sa_t, sb_t = _rope_tables(n_ctx, n_lat)
    eq = _head_ones(Q_COLS)
    ek = _head_ones(KV_COLS)
    eh = _head_ones(GROUP * HEAD_DIM)
    row = lambda v: v.reshape(1, -1)

    w_in_b = w_in.astype(BF16)
    w_out_b = w_out.astype(BF16)
    w_up_b = w_up.astype(BF16)
    w_down_b = w_down.astype(BF16)

    ctx_tiles = n_ctx // TOK_TILE
    assert depth >= 1 and ctx_tiles == 1
    ctx_qblk = n_ctx // Q_BLOCK
    lat_qblk = n_lat // Q_BLOCK

    h_lat, h_ctx = x, ctx
    lat_off = 0
    for l in range(depth):
        last = l == depth - 1
        gq = row(jnp.tile(q_norm[l], N_HEADS))
        gk = row(jnp.tile(k_norm[l], N_KV))
        sink_rows = jnp.repeat(sink[l].reshape(N_KV, 1, GROUP) * LOG2_E, Q_BLOCK, axis=2)

        qa, kda, vta, qb, kdb, vtb = _proj(h_lat, h_ctx, 0, n_tok // TOK_TILE, mods[l],
                                           row(g_pre_mix[l]), w_in_b[l], gq, gk, eq, ek,
                                           cos_t, sa_t, sb_t)
        bound_a = (HEAD_DIM * Q_SCALE * BOUND_SLACK * jnp.max(jnp.abs(q_norm[l]))
                   * jnp.max(jnp.abs(k_norm[l]))).reshape(1)
        oa = _attention(qa, kda, vta, None, eh, bound_a, q_blk_off=ctx_qblk, n_qblk=lat_qblk,
                        n_ctx=n_ctx, n_full=n_tok, band=False, q_per_step=Q_PER_STEP_GLOBAL,
                        name="attn_global")
        ob = _attention(qb, kdb, vtb, sink_rows, eh, None, q_blk_off=ctx_qblk, n_qblk=lat_qblk,
                        n_ctx=n_ctx, n_full=n_ctx, band=True, q_per_step=Q_PER_STEP_WINDOW,
                        name="attn_window")
        post = functools.partial(_post, mods=mods[l], w_out=w_out_b[l], g_post_mix=row(g_post_mix[l]),
                                 g_pre_mlp=row(g_pre_mlp[l]), w_up=w_up_b[l], w_down=w_down_b[l],
                                 g_post_mlp=row(g_post_mlp[l]))
        if not last:
            oac = _attention(qa, kda, vta, None, eh, bound_a, q_blk_off=0, n_qblk=ctx_qblk, n_ctx=n_ctx,
                             n_full=n_ctx, band=False, q_per_step=ctx_qblk, name="attn_ctx_global")
            obc = _attention(qb, kdb, vtb, sink_rows, eh, None, q_blk_off=0, n_qblk=ctx_qblk, n_ctx=n_ctx,
                             n_full=n_ctx, band=False, q_per_step=ctx_qblk, name="attn_ctx_sink")
            assert h_ctx is not None
            h_lat = post(h_lat, h_ctx, lat_off, n_tok // TOK_TILE, oa, ob, oac, obc)
            h_ctx, lat_off = None, ctx_tiles
        else:
            if h_ctx is not None:
                h_lat = post(h_lat, None, 0, n_lat // TOK_TILE, oa, ob, None, None)
            else:
                h_lat = post(h_lat, None, lat_off, n_lat // TOK_TILE, oa, ob, None, None)
    return h_lat
```

```python
import functools

import jax
import jax.numpy as jnp
from jax import lax
from jax.experimental import pallas as pl
from jax.experimental.pallas import tpu as pltpu

F32 = jnp.float32
BF16 = jnp.bfloat16

D_MODEL = 1024
HEAD_DIM = 64
N_HEADS = 8
N_KV = 2
GROUP = N_HEADS // N_KV
Q_COLS = N_HEADS * HEAD_DIM
KV_COLS = N_KV * HEAD_DIM
IN_COLS = 2 * (Q_COLS + 2 * KV_COLS)
D_FF = 4 * D_MODEL
GRID_W = 64
WINDOW = 128
Q_BLOCK = 128
ROPE_THETA = 10000.0
EPS = 1e-6
NEG_BIG = -1e30
N_MOD = 6
ROPE_SHIFT = HEAD_DIM // 4
LOG2_E = 1.4426950408889634
Q_SCALE = HEAD_DIM ** -0.5 * LOG2_E

LANES = 128
TOK_TILE = 256
PROJ_SUB = 3
POST_SUBS = (1, 2, 3)
VT_ROWS = HEAD_DIM + 16
KEY_CHUNK = 768
Q_PER_STEP_GLOBAL = 8
Q_PER_STEP_WINDOW = 16
LOGIT_BOUND_LIMIT = 48.0
BOUND_SLACK = 1.01
FF_CHUNK = 1024
ADA_ROWS = 32
ADA_COLS = 1536
VMEM_LIMIT = 56 * 1024 * 1024


def _rms(x, g):
    return x * lax.rsqrt(jnp.mean(x * x, axis=-1, keepdims=True) + EPS) * g


def _ada_kernel(c_ref, w_ref, b_ref, o_ref):
    c = c_ref[...]
    a = c * jax.nn.sigmoid(c)
    w = w_ref[...]
    a_hi = a.astype(BF16)
    a_lo = (a - a_hi.astype(F32)).astype(BF16)
    w_hi = w.astype(BF16)
    w_lo = (w - w_hi.astype(F32)).astype(BF16)
    acc = jnp.dot(a_hi, w_hi, preferred_element_type=F32)
    acc += jnp.dot(a_lo, w_hi, preferred_element_type=F32)
    acc += jnp.dot(a_hi, w_lo, preferred_element_type=F32)
    o_ref[...] = acc + b_ref[...]


def _ada(cc, w_ada, b_ada):
    depth = w_ada.shape[0]
    n_out = w_ada.shape[2]
    return pl.pallas_call(
        _ada_kernel,
        grid=(depth, n_out // ADA_COLS),
        in_specs=[
            pl.BlockSpec((ADA_ROWS, D_MODEL), lambda l, j: (0, 0)),
            pl.BlockSpec((None, D_MODEL, ADA_COLS), lambda l, j: (l, 0, j)),
            pl.BlockSpec((None, 1, ADA_COLS), lambda l, j: (l, 0, j)),
        ],
        out_specs=pl.BlockSpec((None, ADA_ROWS, ADA_COLS), lambda l, j: (l, 0, j)),
        out_shape=jax.ShapeDtypeStruct((depth, ADA_ROWS, n_out), F32),
        compiler_params=pltpu.CompilerParams(
            dimension_semantics=("arbitrary", "arbitrary"), vmem_limit_bytes=VMEM_LIMIT),
        name="ada",
    )(cc, w_ada, b_ada.reshape(depth, 1, n_out))


def _stream_specs(n_sub, block_cols, lat_tile_off, has_ctx):
    specs = []
    for i in range(n_sub):
        if has_ctx:
            idx = lambda b, t, i=i: (b, jnp.maximum(n_sub * t + i - 1, 0), 0)
        else:
            idx = lambda b, t, i=i: (b, lat_tile_off + n_sub * t + i, 0)
        specs.append(pl.BlockSpec((None, TOK_TILE, block_cols), idx))
    if has_ctx:
        specs.append(pl.BlockSpec((None, TOK_TILE, block_cols), lambda b, t: (b, 0, 0)))
    return specs


def _sub_tile(refs, ctx_ref, i):
    v = refs[i][...]
    if ctx_ref is not None and i == 0:
        v = jnp.where(pl.program_id(1) == 0, ctx_ref[...], v)
    return v


def _sub_mod(mod_ref, ctx_first, i):
    if ctx_first and i == 0:
        return jnp.where(pl.program_id(1) == 0, mod_ref[0], mod_ref[1])
    return mod_ref[1]


def _proj_kernel(*refs, n_sub, has_ctx):
    h_refs = refs[:n_sub]
    refs = refs[n_sub:]
    ctx_ref = None
    if has_ctx:
        ctx_ref, refs = refs[0], refs[1:]
    (mod_ref, gpre_ref, w_ref, gq_ref, gk_ref, eq_ref, ek_ref, cos_ref, sa_ref, sb_ref,
     qa_ref, kda_ref, vta_ref, qb_ref, kdb_ref, vtb_ref) = refs
    lo = lax.broadcasted_iota(jnp.int32, (TOK_TILE, LANES), 1) < HEAD_DIM
    kv = 2 * KV_COLS

    def prenorm(i):
        mod = _sub_mod(mod_ref, True, i)
        x = _sub_tile(h_refs, ctx_ref, i)
        return (_rms(x, gpre_ref[...]) * (1 + mod[1:2, :]) + mod[0:1, :]).astype(BF16)

    def stages(i, u):
        rows = slice(i * TOK_TILE, (i + 1) * TOK_TILE)
        cos = cos_ref[rows, :]
        sa = sa_ref[rows, :]
        sb = sb_ref[rows, :]

        def rope(t):
            up = pltpu.roll(t, LANES - ROPE_SHIFT, 1)
            dn = pltpu.roll(t, ROPE_SHIFT, 1)
            return t * cos + up * sa + dn * sb

        def head_norm(t, e_ref, g_ref):
            ss = jnp.dot((t * t).astype(BF16), e_ref[...], preferred_element_type=F32)
            return t * lax.rsqrt(ss * (1.0 / HEAD_DIM) + EPS) * g_ref[...]

        def store_q(q, q_ref):
            for j in range(Q_COLS // LANES):
                sl = slice(j * LANES, (j + 1) * LANES)
                q_ref[rows, sl] = (rope(q[:, sl]) * Q_SCALE).astype(BF16)

        def store_k(k, kd_ref):
            k = rope(k)
            r = pltpu.roll(k, HEAD_DIM, 1)
            kd_ref[rows, 0:LANES] = jnp.where(lo, k, r).astype(BF16)
            kd_ref[rows, LANES:2 * LANES] = jnp.where(lo, r, k).astype(BF16)

        def store_vt(v, vt_ref):
            v_t = v.T.astype(BF16)
            ones = jnp.ones((VT_ROWS - HEAD_DIM, TOK_TILE), BF16)
            for h in range(N_KV):
                vt_ref[h, 0:HEAD_DIM, rows] = v_t[h * HEAD_DIM:(h + 1) * HEAD_DIM, :]
                vt_ref[h, HEAD_DIM:VT_ROWS, rows] = ones

        def project(c0, n):
            return lambda: jnp.dot(u, w_ref[:, c0:c0 + n], preferred_element_type=F32)

        def epi_qa(z):
            store_q(head_norm(z, eq_ref, gq_ref), qa_ref)

        def epi_kva(z):
            store_k(head_norm(z[:, 0:KV_COLS], ek_ref, gk_ref), kda_ref)
            store_vt(z[:, KV_COLS:kv], vta_ref)

        def epi_qb(z):
            store_q(z, qb_ref)

        def epi_kvb(z):
            store_k(z[:, 0:KV_COLS], kdb_ref)
            store_vt(z[:, KV_COLS:kv], vtb_ref)

        return [(project(0, Q_COLS), epi_qa), (project(Q_COLS, kv), epi_kva),
                (project(Q_COLS + kv, Q_COLS), epi_qb), (project(2 * Q_COLS + kv, kv), epi_kvb)]

    u_next = prenorm(0)
    pending = []
    for i in range(n_sub):
        u = u_next
        results = []
        for g, (matmul, epilogue) in enumerate(stages(i, u)):
            results.append((epilogue, matmul()))
            if g == 0 and i + 1 < n_sub:
                u_next = prenorm(i + 1)
            if pending:
                epi, z = pending.pop(0)
                epi(z)
        for epi, z in pending:
            epi(z)
        pending = results
    for epi, z in pending:
        epi(z)


def _proj(h_lat, h_ctx, lat_tile_off, n_tiles, mods, g_pre, w_in, gq, gk, eq, ek, cos_t, sa_t, sb_t):
    bsz = h_lat.shape[0]
    has_ctx = h_ctx is not None
    n_sub = PROJ_SUB
    assert n_tiles % n_sub == 0
    n_tok = n_tiles * TOK_TILE
    rows = n_sub * TOK_TILE
    const = lambda b, t: (0, 0)
    tok = lambda b, t: (b, t, 0)
    q_shape = jax.ShapeDtypeStruct((bsz, n_tok, Q_COLS), BF16)
    kd_shape = jax.ShapeDtypeStruct((bsz, n_tok, 2 * KV_COLS), BF16)
    vt_shape = jax.ShapeDtypeStruct((bsz, N_KV, VT_ROWS, n_tok), BF16)
    q_spec = pl.BlockSpec((None, rows, Q_COLS), tok)
    kd_spec = pl.BlockSpec((None, rows, 2 * KV_COLS), tok)
    vt_spec = pl.BlockSpec((None, N_KV, VT_ROWS, rows), lambda b, t: (b, 0, 0, t))
    rope_spec = pl.BlockSpec((rows, LANES), lambda b, t: (t, 0))
    h_args = [h_lat] * n_sub + ([h_ctx] if has_ctx else [])
    return pl.pallas_call(
        functools.partial(_proj_kernel, n_sub=n_sub, has_ctx=has_ctx),
        grid=(bsz, n_tiles // n_sub),
        in_specs=_stream_specs(n_sub, D_MODEL, lat_tile_off, has_ctx) + [
            pl.BlockSpec((None, 2, N_MOD, D_MODEL), lambda b, t: (b, 0, 0, 0)),
            pl.BlockSpec((1, D_MODEL), const),
            pl.BlockSpec((D_MODEL, IN_COLS), const),
            pl.BlockSpec((1, Q_COLS), const),
            pl.BlockSpec((1, KV_COLS), const),
            pl.BlockSpec((Q_COLS, Q_COLS), const),
            pl.BlockSpec((KV_COLS, KV_COLS), const),
            rope_spec, rope_spec, rope_spec,
        ],
        out_specs=[q_spec, kd_spec, vt_spec, q_spec, kd_spec, vt_spec],
        out_shape=[q_shape, kd_shape, vt_shape, q_shape, kd_shape, vt_shape],
        compiler_params=pltpu.CompilerParams(
            dimension_semantics=("parallel", "parallel"), vmem_limit_bytes=VMEM_LIMIT),
        name="proj",
    )(*h_args, mods, g_pre, w_in, gq, gk, eq, ek, cos_t, sa_t, sb_t)


def _band_start(blk, n_ctx, n_tok):
    return jnp.minimum(n_ctx + (blk - 1) * Q_BLOCK, n_tok - (Q_BLOCK + 2 * WINDOW))


def _window_bias_variants(n_ctx, n_tok, n_lat_blk):
    n_band = Q_BLOCK + 2 * WINDOW
    kj = jnp.arange(n_band, dtype=jnp.int32)[:, None]
    qr = jnp.arange(GROUP * Q_BLOCK, dtype=jnp.int32)[None, :] % Q_BLOCK
    variants = []
    for blk in (0, 1, n_lat_blk - 1):
        kpos = kj + (_band_start(blk, n_ctx, n_tok) - n_ctx)
        qpos = qr + blk * Q_BLOCK
        valid = (jnp.abs(kpos - qpos) <= WINDOW) & (kpos >= 0)
        variants.append(jnp.where(valid, 0.0, NEG_BIG).astype(F32))
    return jnp.stack(variants)


def _attn_kernel(*refs, n_ctx, n_full, band, use_sink, n_tok, n_lat_blk, q_row0, q_per_step,
                 given_bound):
    refs = list(refs)
    q_ref, kd_ref, vt_ref, eh_ref = refs[:4]
    rest = refs[4:]
    sink_ref = rest.pop(0) if use_sink else None
    bias_ref = rest.pop(0) if band else None
    if given_bound:
        bound_ref, o_ref = rest
    else:
        o_ref, bound_ref = rest
    nq = GROUP * Q_BLOCK
    n_band = Q_BLOCK + 2 * WINDOW if band else 0
    chunks = [(c0, min(c0 + KEY_CHUNK, n_full), False) for c0 in range(0, n_full, KEY_CHUNK)]
    if band:
        chunks.append((n_full, n_full + n_band, True))
    lo = lax.broadcasted_iota(jnp.int32, (Q_BLOCK, LANES), 1) < HEAD_DIM
    zero = jnp.zeros((Q_BLOCK, LANES), BF16)
    nt = (((1,), (1,)), ((), ()))
    step = pl.program_id(2)

    if not given_bound:
        @pl.when(step == 0)
        def _():
            q = q_ref[...].astype(F32)
            qq = jnp.dot((q * q).astype(BF16), eh_ref[...], preferred_element_type=F32)
            k = kd_ref[...].astype(F32)
            kk = jnp.dot((k * k).astype(BF16), eh_ref[0:LANES, 0:LANES], preferred_element_type=F32)
            bound_ref[0] = jnp.sqrt(jnp.max(qq) * jnp.max(kk)) * BOUND_SLACK

    def band_start(blk):
        return pl.multiple_of(_band_start(blk, n_ctx, n_tok), LANES)

    def extend_q(blk):
        q = q_ref[pl.ds(pl.multiple_of(q_row0 + blk * Q_BLOCK, Q_BLOCK), Q_BLOCK), :]
        parts = []
        for c in range(GROUP * HEAD_DIM // LANES):
            qc = q[:, c * LANES:(c + 1) * LANES]
            parts.append(jnp.where(lo, qc, zero))
            parts.append(jnp.where(lo, zero, qc))
        return jnp.concatenate(parts, axis=0)

    def logits(qm, blk, chunk):
        c0, c1, in_band = chunk
        if in_band:
            variant = jnp.where(blk == 0, 0, jnp.where(blk >= n_lat_blk - 1, 2, 1))
            return lax.dot_general(kd_ref[pl.ds(band_start(blk), n_band), :], qm, nt,
                                   preferred_element_type=F32) + bias_ref[variant]
        return lax.dot_general(kd_ref[c0:c1, :], qm, nt, preferred_element_type=F32)

    def values_t(blk, chunk):
        c0, c1, in_band = chunk
        if in_band:
            return vt_ref[:, pl.ds(band_start(blk), n_band)]
        return vt_ref[:, c0:c1]

    def column_max(qm, blk):
        mx = None
        for chunk in chunks:
            s = logits(qm, blk, chunk)
            cm = jnp.max(s.reshape(s.shape[0] // 8, 8, nq), axis=0)
            mx = cm if mx is None else jnp.maximum(mx, cm)
        return jnp.max(mx, axis=0, keepdims=True)

    def finalize(t, acc, m):
        l = acc[HEAD_DIM:HEAD_DIM + 1, :]
        if use_sink:
            l = l + jnp.exp2(sink_ref[...] - m)
        o_t = acc[0:HEAD_DIM, :] / l
        o_t = jnp.concatenate([o_t[:, g * Q_BLOCK:(g + 1) * Q_BLOCK] for g in range(GROUP)], axis=0)
        o_ref[t * Q_BLOCK:(t + 1) * Q_BLOCK, :] = o_t.T.astype(BF16)

    def attend_all(qms, shifts):
        items = [(t, c) for t in range(q_per_step) for c in range(len(chunks))]
        blk_of = lambda t: step * q_per_step + t
        s_next = logits(qms[0], blk_of(0), chunks[0])
        acc = [None] * q_per_step
        pv_prev = None

        def retire(prev):
            t, c, pv = prev
            acc[t] = pv if acc[t] is None else acc[t] + pv
            if c == len(chunks) - 1:
                finalize(t, acc[t], shifts[t])

        for n, (t, c) in enumerate(items):
            s = s_next
            if n + 1 < len(items):
                t2, c2 = items[n + 1]
                s_next = logits(qms[t2], blk_of(t2), chunks[c2])
            p = jnp.exp2(s - shifts[t]).astype(BF16)
            pv = jnp.dot(values_t(blk_of(t), chunks[c]), p, preferred_element_type=F32)
            if pv_prev is not None:
                retire(pv_prev)
            pv_prev = (t, c, pv)
        retire(pv_prev)

    def with_sink(m):
        return jnp.maximum(m, sink_ref[...]) if use_sink else m

    bound = bound_ref[0]
    one_pass = bound <= LOGIT_BOUND_LIMIT

    @pl.when(one_pass)
    def _():
        m = with_sink(jnp.full((1, nq), bound, F32))
        qms = [extend_q(step * q_per_step + t) for t in range(q_per_step)]
        attend_all(qms, [m] * q_per_step)

    @pl.when(jnp.logical_not(one_pass))
    def _():
        qms = [extend_q(step * q_per_step + t) for t in range(q_per_step)]
        shifts = [with_sink(column_max(qms[t], step * q_per_step + t)) for t in range(q_per_step)]
        attend_all(qms, shifts)


def _attention(q, kd, vt, sink_rows, head_ones, logit_bound, *, q_blk_off, n_qblk, n_ctx, n_full, band,
               q_per_step, name):
    bsz, n_tok, _ = q.shape
    q_per_step = min(q_per_step, n_qblk)
    n_rows = n_tok if (band or n_full == n_tok) else max(n_full, (q_blk_off + n_qblk) * Q_BLOCK)
    assert n_qblk % q_per_step == 0
    use_sink = sink_rows is not None
    nq = GROUP * Q_BLOCK
    in_specs = [
        pl.BlockSpec((None, n_rows, GROUP * HEAD_DIM), lambda b, h, j: (b, 0, h)),
        pl.BlockSpec((None, n_rows, 2 * HEAD_DIM), lambda b, h, j: (b, 0, h)),
        pl.BlockSpec((None, None, VT_ROWS, n_rows), lambda b, h, j: (b, h, 0, 0)),
        pl.BlockSpec(head_ones.shape, lambda b, h, j: (0, 0)),
    ]
    args = [q, kd, vt, head_ones]
    if use_sink:
        in_specs.append(pl.BlockSpec((None, 1, nq), lambda b, h, j: (h, 0, 0)))
        args.append(sink_rows)
    if band:
        bias = _window_bias_variants(n_ctx, n_tok, n_qblk)
        in_specs.append(pl.BlockSpec(bias.shape, lambda b, h, j: (0, 0, 0)))
        args.append(bias)
    given_bound = logit_bound is not None
    if given_bound:
        in_specs.append(pl.BlockSpec(memory_space=pltpu.SMEM))
        args.append(logit_bound)
    kern = functools.partial(_attn_kernel, n_ctx=n_ctx, n_full=n_full, band=band, use_sink=use_sink,
                             n_tok=n_tok, n_lat_blk=n_qblk, q_row0=q_blk_off * Q_BLOCK,
                             q_per_step=q_per_step, given_bound=given_bound)
    rows = q_per_step * Q_BLOCK
    return pl.pallas_call(
        kern,
        grid=(bsz, N_KV, n_qblk // q_per_step),
        in_specs=in_specs,
        out_specs=pl.BlockSpec((None, rows, GROUP * HEAD_DIM), lambda b, h, j: (b, j, h)),
        out_shape=jax.ShapeDtypeStruct((bsz, n_qblk * Q_BLOCK, Q_COLS), BF16),
        scratch_shapes=[] if given_bound else [pltpu.SMEM((1,), F32)],
        compiler_params=pltpu.CompilerParams(
            dimension_semantics=("parallel", "parallel", "arbitrary"), vmem_limit_bytes=VMEM_LIMIT),
        name=name,
    )(*args)


def _post_kernel(*refs, n_sub, has_ctx):
    n_in = n_sub + (1 if has_ctx else 0)
    h_refs, ma_refs, mb_refs = refs[:n_sub], refs[n_in:n_in + n_sub], refs[2 * n_in:2 * n_in + n_sub]
    ctx_h = refs[n_sub] if has_ctx else None
    ctx_ma = refs[n_in + n_sub] if has_ctx else None
    ctx_mb = refs[2 * n_in + n_sub] if has_ctx else None
    mod_ref, wout_ref, gpm_ref, gpre_ref, wup_ref, wdn_ref, gpl_ref, o_ref = refs[3 * n_in:]
    mods = [_sub_mod(mod_ref, has_ctx, i) for i in range(n_sub)]

    def out_proj(i):
        mix = jnp.concatenate([_sub_tile(ma_refs, ctx_ma, i), _sub_tile(mb_refs, ctx_mb, i)], axis=1)
        return jnp.dot(mix, wout_ref[...], preferred_element_type=F32)

    def mixer_residual(i, mix):
        mod = mods[i]
        h = _sub_tile(h_refs, ctx_h, i) + mod[2:3, :] * _rms(mix, gpm_ref[...])
        u = (_rms(h, gpre_ref[...]) * (1 + mod[4:5, :]) + mod[3:4, :]).astype(BF16)
        return h, u

    def mlp_chunk(u, y, c0):
        a = jnp.dot(u, wup_ref[:, c0:c0 + FF_CHUNK], preferred_element_type=F32)
        a = jnp.square(jnp.maximum(a, 0.0)).astype(BF16)
        part = jnp.dot(a, wdn_ref[c0:c0 + FF_CHUNK, :], preferred_element_type=F32)
        return part if y is None else y + part

    def mlp_residual(i, h, y):
        o_ref[i * TOK_TILE:(i + 1) * TOK_TILE, :] = h + mods[i][5:6, :] * _rms(y, gpl_ref[...])

    hu = [None] * n_sub
    mix_prev = None
    for i in range(n_sub):
        mix = out_proj(i)
        if i > 0:
            hu[i - 1] = mixer_residual(i - 1, mix_prev)
        mix_prev = mix
    for i in range(n_sub):
        y = None
        for c, c0 in enumerate(range(0, D_FF, FF_CHUNK)):
            if i == 0 and c == 0:
                hu[n_sub - 1] = mixer_residual(n_sub - 1, mix_prev)
            y = mlp_chunk(hu[i][1], y, c0)
            if i > 0 and c == 0:
                mlp_residual(i - 1, hu[i - 1][0], y_prev)
        y_prev = y
    mlp_residual(n_sub - 1, hu[n_sub - 1][0], y_prev)


def _post(h_lat, h_ctx, lat_tile_off, n_tiles, mix_a, mix_b, ctx_mix_a, ctx_mix_b, mods, w_out,
          g_post_mix, g_pre_mlp, w_up, w_down, g_post_mlp):
    bsz = h_lat.shape[0]
    has_ctx = h_ctx is not None
    n_sub = max(s for s in POST_SUBS if n_tiles % s == 0)
    const = lambda b, t: (0, 0)
    resident = functools.partial(pl.BlockSpec, index_map=const, pipeline_mode=pl.Buffered(1))
    ctx_args = lambda v: [v] if has_ctx else []
    args = ([h_lat] * n_sub + ctx_args(h_ctx) + [mix_a] * n_sub + ctx_args(ctx_mix_a)
            + [mix_b] * n_sub + ctx_args(ctx_mix_b))
    return pl.pallas_call(
        functools.partial(_post_kernel, n_sub=n_sub, has_ctx=has_ctx),
        grid=(bsz, n_tiles // n_sub),
        in_specs=(_stream_specs(n_sub, D_MODEL, lat_tile_off, has_ctx)
                  + _stream_specs(n_sub, Q_COLS, 0, has_ctx)
                  + _stream_specs(n_sub, Q_COLS, 0, has_ctx) + [
            pl.BlockSpec((None, 2, N_MOD, D_MODEL), lambda b, t: (b, 0, 0, 0)),
            resident((D_MODEL, D_MODEL)),
            pl.BlockSpec((1, D_MODEL), const),
            pl.BlockSpec((1, D_MODEL), const),
            resident((D_MODEL, D_FF)),
            resident((D_FF, D_MODEL)),
            pl.BlockSpec((1, D_MODEL), const),
        ]),
        out_specs=pl.BlockSpec((None, n_sub * TOK_TILE, D_MODEL), lambda b, t: (b, t, 0)),
        out_shape=jax.ShapeDtypeStruct((bsz, n_tiles * TOK_TILE, D_MODEL), F32),
        compiler_params=pltpu.CompilerParams(
            dimension_semantics=("parallel", "parallel"), vmem_limit_bytes=VMEM_LIMIT),
        name="post",
    )(*args, mods, w_out, g_post_mix, g_pre_mlp, w_up, w_down, g_post_mlp)


def _rope_tables(n_ctx, n_lat):
    rows = n_lat // GRID_W
    row_ids = jnp.repeat(jnp.arange(rows, dtype=jnp.int32), GRID_W).astype(F32)
    col_ids = jnp.tile(jnp.arange(GRID_W, dtype=jnp.int32), rows).astype(F32)
    axis_dim = HEAD_DIM // 2
    inv = ROPE_THETA ** (-jnp.arange(0, axis_dim, 2, dtype=F32) / axis_dim)
    ang_r = row_ids[:, None] * inv[None, :]
    ang_c = col_ids[:, None] * inv[None, :]
    cos_r, sin_r, cos_c, sin_c = jnp.cos(ang_r), jnp.sin(ang_r), jnp.cos(ang_c), jnp.sin(ang_c)
    zero = jnp.zeros_like(sin_r)
    cos_h = jnp.concatenate([cos_r, cos_r, cos_c, cos_c], axis=-1)
    sa_h = jnp.concatenate([-sin_r, zero, -sin_c, zero], axis=-1)
    sb_h = jnp.concatenate([zero, sin_r, zero, sin_c], axis=-1)

    def full(t, ctx_value):
        t = jnp.tile(t, (1, LANES // HEAD_DIM))
        return jnp.concatenate([jnp.full((n_ctx, LANES), ctx_value, F32), t], axis=0)

    return full(cos_h, 1.0), full(sa_h, 0.0), full(sb_h, 0.0)


def _head_ones(n_cols):
    head = jnp.arange(n_cols, dtype=jnp.int32) // HEAD_DIM
    return (head[:, None] == head[None, :]).astype(BF16)


def kernel(x, c, ctx, c_ctx, w_ada, b_ada, g_pre_mix, g_post_mix, g_pre_mlp, g_post_mlp,
           w_in, q_norm, k_norm, sink, w_out, w_up, w_down):
    bsz, n_lat, d = x.shape
    n_ctx = ctx.shape[1]
    depth = w_in.shape[0]
    n_tok = n_ctx + n_lat
    assert d == D_MODEL and n_ctx == TOK_TILE and n_lat % TOK_TILE == 0 and bsz + 1 <= ADA_ROWS
    assert w_in.shape[2] == IN_COLS and w_up.shape[2] == D_FF

    cc = jnp.concatenate([c, c_ctx[None, :], jnp.zeros((ADA_ROWS - bsz - 1, d), F32)], axis=0)
    mod_all = _ada(cc, w_ada, b_ada)
    mod_all = mod_all.reshape(depth, ADA_ROWS, N_MOD, d)
    mod_ctx = jnp.broadcast_to(mod_all[:, bsz:bsz + 1], (depth, bsz, N_MOD, d))
    mods = jnp.stack([mod_ctx, mod_all[:, :bsz]], axis=2)

    cos_t, sa_t, sb_t = _rope_tables(n_ctx, n_lat)
    eq = _head_ones(Q_COLS)
    ek = _head_ones(KV_COLS)
    eh = _head_ones(GROUP * HEAD_DIM)
    row = lambda v: v.reshape(1, -1)

    w_in_b = w_in.astype(BF16)
    w_out_b = w_out.astype(BF16)
    w_up_b = w_up.astype(BF16)
    w_down_b = w_down.astype(BF16)

    ctx_tiles = n_ctx // TOK_TILE
    assert depth >= 1 and ctx_tiles == 1
    ctx_qblk = n_ctx // Q_BLOCK
    lat_qblk = n_lat // Q_BLOCK

    h_lat, h_ctx = x, ctx
    lat_off = 0
    for l in range(depth):
        last = l == depth - 1
        gq = row(jnp.tile(q_norm[l], N_HEADS))
        gk = row(jnp.tile(k_norm[l], N_KV))
        sink_rows = jnp.repeat(sink[l].reshape(N_KV, 1, GROUP) * LOG2_E, Q_BLOCK, axis=2)

        qa, kda, vta, qb, kdb, vtb = _proj(h_lat, h_ctx, 0, n_tok // TOK_TILE, mods[l],
                                           row(g_pre_mix[l]), w_in_b[l], gq, gk, eq, ek,
                                           cos_t, sa_t, sb_t)
        bound_a = (HEAD_DIM * Q_SCALE * BOUND_SLACK * jnp.max(jnp.abs(q_norm[l]))
                   * jnp.max(jnp.abs(k_norm[l]))).reshape(1)
        oa = _attention(qa, kda, vta, None, eh, bound_a, q_blk_off=ctx_qblk, n_qblk=lat_qblk,
                        n_ctx=n_ctx, n_full=n_tok, band=False, q_per_step=Q_PER_STEP_GLOBAL,
                        name="attn_global")
        ob = _attention(qb, kdb, vtb, sink_rows, eh, None, q_blk_off=ctx_qblk, n_qblk=lat_qblk,
                        n_ctx=n_ctx, n_full=n_ctx, band=True, q_per_step=Q_PER_STEP_WINDOW,
                        name="attn_window")
        post = functools.partial(_post, mods=mods[l], w_out=w_out_b[l], g_post_mix=row(g_post_mix[l]),
                                 g_pre_mlp=row(g_pre_mlp[l]), w_up=w_up_b[l], w_down=w_down_b[l],
                                 g_post_mlp=row(g_post_mlp[l]))
        if not last:
            oac = _attention(qa, kda, vta, None, eh, bound_a, q_blk_off=0, n_qblk=ctx_qblk, n_ctx=n_ctx,
                             n_full=n_ctx, band=False, q_per_step=ctx_qblk, name="attn_ctx_global")
            obc = _attention(qb, kdb, vtb, sink_rows, eh, None, q_blk_off=0, n_qblk=ctx_qblk, n_ctx=n_ctx,
                             n_full=n_ctx, band=False, q_per_step=ctx_qblk, name="attn_ctx_sink")
            assert h_ctx is not None
            h_lat = post(h_lat, h_ctx, lat_off, n_tok // TOK_TILE, oa, ob, oac, obc)
            h_ctx, lat_off = None, ctx_tiles
        else:
            if h_ctx is not None:
                h_lat = post(h_lat, None, 0, n_lat // TOK_TILE, oa, ob, None, None)
            else:
                h_lat = post(h_lat, None, lat_off, n_lat // TOK_TILE, oa, ob, None, None)
    return h_lat
```

```python
import functools

import jax
import jax.numpy as jnp
from jax import lax
from jax.experimental import pallas as pl
from jax.experimental.pallas import tpu as pltpu

F32 = jnp.float32
BF16 = jnp.bfloat16

D_MODEL = 1024
HEAD_DIM = 64
N_HEADS = 8
N_KV = 2
GROUP = N_HEADS // N_KV
Q_COLS = N_HEADS * HEAD_DIM
KV_COLS = N_KV * HEAD_DIM
IN_COLS = 2 * (Q_COLS + 2 * KV_COLS)
D_FF = 4 * D_MODEL
GRID_W = 64
WINDOW = 128
Q_BLOCK = 128
ROPE_THETA = 10000.0
EPS = 1e-6
NEG_BIG = -1e30
N_MOD = 6
ROPE_SHIFT = HEAD_DIM // 4
LOG2_E = 1.4426950408889634
Q_SCALE = HEAD_DIM ** -0.5 * LOG2_E

LANES = 128
TOK_TILE = 256
PROJ_SUB = 3
POST_SUBS = (1, 2, 3)
VT_ROWS = HEAD_DIM + 16
KEY_CHUNK = 2304
Q_PER_STEP_GLOBAL = 8
Q_PER_STEP_WINDOW = 16
LOGIT_BOUND_LIMIT = 48.0
BOUND_SLACK = 1.01
FF_CHUNK = 1024
ADA_ROWS = 32
ADA_COLS = 1536
VMEM_LIMIT = 56 * 1024 * 1024


def _rms(x, g):
    return x * lax.rsqrt(jnp.mean(x * x, axis=-1, keepdims=True) + EPS) * g


def _ada_kernel(c_ref, w_ref, b_ref, o_ref):
    c = c_ref[...]
    a = c * jax.nn.sigmoid(c)
    w = w_ref[...]
    a_hi = a.astype(BF16)
    a_lo = (a - a_hi.astype(F32)).astype(BF16)
    w_hi = w.astype(BF16)
    w_lo = (w - w_hi.astype(F32)).astype(BF16)
    acc = jnp.dot(a_hi, w_hi, preferred_element_type=F32)
    acc += jnp.dot(a_lo, w_hi, preferred_element_type=F32)
    acc += jnp.dot(a_hi, w_lo, preferred_element_type=F32)
    o_ref[...] = acc + b_ref[...]


def _ada(cc, w_ada, b_ada):
    depth = w_ada.shape[0]
    n_out = w_ada.shape[2]
    return pl.pallas_call(
        _ada_kernel,
        grid=(depth, n_out // ADA_COLS),
        in_specs=[
            pl.BlockSpec((ADA_ROWS, D_MODEL), lambda l, j: (0, 0)),
            pl.BlockSpec((None, D_MODEL, ADA_COLS), lambda l, j: (l, 0, j)),
            pl.BlockSpec((None, 1, ADA_COLS), lambda l, j: (l, 0, j)),
        ],
        out_specs=pl.BlockSpec((None, ADA_ROWS, ADA_COLS), lambda l, j: (l, 0, j)),
        out_shape=jax.ShapeDtypeStruct((depth, ADA_ROWS, n_out), F32),
        compiler_params=pltpu.CompilerParams(
            dimension_semantics=("arbitrary", "arbitrary"), vmem_limit_bytes=VMEM_LIMIT),
        name="ada",
    )(cc, w_ada, b_ada.reshape(depth, 1, n_out))


def _stream_specs(n_sub, block_cols, lat_tile_off, has_ctx):
    specs = []
    for i in range(n_sub):
        if has_ctx:
            idx = lambda b, t, i=i: (b, jnp.maximum(n_sub * t + i - 1, 0), 0)
        else:
            idx = lambda b, t, i=i: (b, lat_tile_off + n_sub * t + i, 0)
        specs.append(pl.BlockSpec((None, TOK_TILE, block_cols), idx))
    if has_ctx:
        specs.append(pl.BlockSpec((None, TOK_TILE, block_cols), lambda b, t: (b, 0, 0)))
    return specs


def _sub_tile(refs, ctx_ref, i):
    v = refs[i][...]
    if ctx_ref is not None and i == 0:
        v = jnp.where(pl.program_id(1) == 0, ctx_ref[...], v)
    return v


def _sub_mod(mod_ref, ctx_first, i):
    if ctx_first and i == 0:
        return jnp.where(pl.program_id(1) == 0, mod_ref[0], mod_ref[1])
    return mod_ref[1]


def _proj_kernel(*refs, n_sub, has_ctx):
    h_refs = refs[:n_sub]
    refs = refs[n_sub:]
    ctx_ref = None
    if has_ctx:
        ctx_ref, refs = refs[0], refs[1:]
    (mod_ref, gpre_ref, w_ref, gq_ref, gk_ref, eq_ref, ek_ref, cos_ref, sa_ref, sb_ref,
     qa_ref, kda_ref, vta_ref, qb_ref, kdb_ref, vtb_ref) = refs
    lo = lax.broadcasted_iota(jnp.int32, (TOK_TILE, LANES), 1) < HEAD_DIM
    kv = 2 * KV_COLS

    def prenorm(i):
        mod = _sub_mod(mod_ref, True, i)
        x = _sub_tile(h_refs, ctx_ref, i)
        return (_rms(x, gpre_ref[...]) * (1 + mod[1:2, :]) + mod[0:1, :]).astype(BF16)

    def stages(i, u):
        rows = slice(i * TOK_TILE, (i + 1) * TOK_TILE)
        cos = cos_ref[rows, :]
        sa = sa_ref[rows, :]
        sb = sb_ref[rows, :]

        def rope(t):
            up = pltpu.roll(t, LANES - ROPE_SHIFT, 1)
            dn = pltpu.roll(t, ROPE_SHIFT, 1)
            return t * cos + up * sa + dn * sb

        def head_norm(t, e_ref, g_ref):
            ss = jnp.dot((t * t).astype(BF16), e_ref[...], preferred_element_type=F32)
            return t * lax.rsqrt(ss * (1.0 / HEAD_DIM) + EPS) * g_ref[...]

        def store_q(q, q_ref):
            for j in range(Q_COLS // LANES):
                sl = slice(j * LANES, (j + 1) * LANES)
                q_ref[rows, sl] = (rope(q[:, sl]) * Q_SCALE).astype(BF16)

        def store_k(k, kd_ref):
            k = rope(k)
            r = pltpu.roll(k, HEAD_DIM, 1)
            kd_ref[rows, 0:LANES] = jnp.where(lo, k, r).astype(BF16)
            kd_ref[rows, LANES:2 * LANES] = jnp.where(lo, r, k).astype(BF16)

        def store_vt(v, vt_ref):
            v_t = v.T.astype(BF16)
            ones = jnp.ones((VT_ROWS - HEAD_DIM, TOK_TILE), BF16)
            for h in range(N_KV):
                vt_ref[h, 0:HEAD_DIM, rows] = v_t[h * HEAD_DIM:(h + 1) * HEAD_DIM, :]
                vt_ref[h, HEAD_DIM:VT_ROWS, rows] = ones

        def project(c0, n):
            return lambda: jnp.dot(u, w_ref[:, c0:c0 + n], preferred_element_type=F32)

        def epi_qa(z):
            store_q(head_norm(z, eq_ref, gq_ref), qa_ref)

        def epi_kva(z):
            store_k(head_norm(z[:, 0:KV_COLS], ek_ref, gk_ref), kda_ref)
            store_vt(z[:, KV_COLS:kv], vta_ref)

        def epi_qb(z):
            store_q(z, qb_ref)

        def epi_kvb(z):
            store_k(z[:, 0:KV_COLS], kdb_ref)
            store_vt(z[:, KV_COLS:kv], vtb_ref)

        return [(project(0, Q_COLS), epi_qa), (project(Q_COLS, kv), epi_kva),
                (project(Q_COLS + kv, Q_COLS), epi_qb), (project(2 * Q_COLS + kv, kv), epi_kvb)]

    u_next = prenorm(0)
    pending = []
    for i in range(n_sub):
        u = u_next
        results = []
        for g, (matmul, epilogue) in enumerate(stages(i, u)):
            results.append((epilogue, matmul()))
            if g == 0 and i + 1 < n_sub:
                u_next = prenorm(i + 1)
            if pending:
                epi, z = pending.pop(0)
                epi(z)
        for epi, z in pending:
            epi(z)
        pending = results
    for epi, z in pending:
        epi(z)


def _proj(h_lat, h_ctx, lat_tile_off, n_tiles, mods, g_pre, w_in, gq, gk, eq, ek, cos_t, sa_t, sb_t):
    bsz = h_lat.shape[0]
    has_ctx = h_ctx is not None
    n_sub = PROJ_SUB
    assert n_tiles % n_sub == 0
    n_tok = n_tiles * TOK_TILE
    rows = n_sub * TOK_TILE
    const = lambda b, t: (0, 0)
    tok = lambda b, t: (b, t, 0)
    q_shape = jax.ShapeDtypeStruct((bsz, n_tok, Q_COLS), BF16)
    kd_shape = jax.ShapeDtypeStruct((bsz, n_tok, 2 * KV_COLS), BF16)
    vt_shape = jax.ShapeDtypeStruct((bsz, N_KV, VT_ROWS, n_tok), BF16)
    q_spec = pl.BlockSpec((None, rows, Q_COLS), tok)
    kd_spec = pl.BlockSpec((None, rows, 2 * KV_COLS), tok)
    vt_spec = pl.BlockSpec((None, N_KV, VT_ROWS, rows), lambda b, t: (b, 0, 0, t))
    rope_spec = pl.BlockSpec((rows, LANES), lambda b, t: (t, 0))
    h_args = [h_lat] * n_sub + ([h_ctx] if has_ctx else [])
    return pl.pallas_call(
        functools.partial(_proj_kernel, n_sub=n_sub, has_ctx=has_ctx),
        grid=(bsz, n_tiles // n_sub),
        in_specs=_stream_specs(n_sub, D_MODEL, lat_tile_off, has_ctx) + [
            pl.BlockSpec((None, 2, N_MOD, D_MODEL), lambda b, t: (b, 0, 0, 0)),
            pl.BlockSpec((1, D_MODEL), const),
            pl.BlockSpec((D_MODEL, IN_COLS), const),
            pl.BlockSpec((1, Q_COLS), const),
            pl.BlockSpec((1, KV_COLS), const),
            pl.BlockSpec((Q_COLS, Q_COLS), const),
            pl.BlockSpec((KV_COLS, KV_COLS), const),
            rope_spec, rope_spec, rope_spec,
        ],
        out_specs=[q_spec, kd_spec, vt_spec, q_spec, kd_spec, vt_spec],
        out_shape=[q_shape, kd_shape, vt_shape, q_shape, kd_shape, vt_shape],
        compiler_params=pltpu.CompilerParams(
            dimension_semantics=("parallel", "parallel"), vmem_limit_bytes=VMEM_LIMIT),
        name="proj",
    )(*h_args, mods, g_pre, w_in, gq, gk, eq, ek, cos_t, sa_t, sb_t)


def _band_start(blk, n_ctx, n_tok):
    return jnp.minimum(n_ctx + (blk - 1) * Q_BLOCK, n_tok - (Q_BLOCK + 2 * WINDOW))


def _window_bias_variants(n_ctx, n_tok, n_lat_blk):
    n_band = Q_BLOCK + 2 * WINDOW
    kj = jnp.arange(n_band, dtype=jnp.int32)[:, None]
    qr = jnp.arange(GROUP * Q_BLOCK, dtype=jnp.int32)[None, :] % Q_BLOCK
    variants = []
    for blk in (0, 1, n_lat_blk - 1):
        kpos = kj + (_band_start(blk, n_ctx, n_tok) - n_ctx)
        qpos = qr + blk * Q_BLOCK
        valid = (jnp.abs(kpos - qpos) <= WINDOW) & (kpos >= 0)
        variants.append(jnp.where(valid, 0.0, NEG_BIG).astype(F32))
    return jnp.stack(variants)


def _attn_kernel(*refs, n_ctx, n_full, band, use_sink, n_tok, n_lat_blk, q_row0, q_per_step,
                 given_bound):
    refs = list(refs)
    q_ref, kd_ref, vt_ref, eh_ref = refs[:4]
    rest = refs[4:]
    sink_ref = rest.pop(0) if use_sink else None
    bias_ref = rest.pop(0) if band else None
    if given_bound:
        bound_ref, o_ref = rest
    else:
        o_ref, bound_ref = rest
    nq = GROUP * Q_BLOCK
    n_band = Q_BLOCK + 2 * WINDOW if band else 0
    chunks = [(c0, min(c0 + KEY_CHUNK, n_full), False) for c0 in range(0, n_full, KEY_CHUNK)]
    if band:
        chunks.append((n_full, n_full + n_band, True))
    lo = lax.broadcasted_iota(jnp.int32, (Q_BLOCK, LANES), 1) < HEAD_DIM
    zero = jnp.zeros((Q_BLOCK, LANES), BF16)
    nt = (((1,), (1,)), ((), ()))
    step = pl.program_id(2)

    if not given_bound:
        @pl.when(step == 0)
        def _():
            q = q_ref[...].astype(F32)
            qq = jnp.dot((q * q).astype(BF16), eh_ref[...], preferred_element_type=F32)
            k = kd_ref[...].astype(F32)
            kk = jnp.dot((k * k).astype(BF16), eh_ref[0:LANES, 0:LANES], preferred_element_type=F32)
            bound_ref[0] = jnp.sqrt(jnp.max(qq) * jnp.max(kk)) * BOUND_SLACK

    def band_start(blk):
        return pl.multiple_of(_band_start(blk, n_ctx, n_tok), LANES)

    def extend_q(blk):
        q = q_ref[pl.ds(pl.multiple_of(q_row0 + blk * Q_BLOCK, Q_BLOCK), Q_BLOCK), :]
        parts = []
        for c in range(GROUP * HEAD_DIM // LANES):
            qc = q[:, c * LANES:(c + 1) * LANES]
            parts.append(jnp.where(lo, qc, zero))
            parts.append(jnp.where(lo, zero, qc))
        return jnp.concatenate(parts, axis=0)

    def logits(qm, blk, chunk):
        c0, c1, in_band = chunk
        if in_band:
            variant = jnp.where(blk == 0, 0, jnp.where(blk >= n_lat_blk - 1, 2, 1))
            return lax.dot_general(kd_ref[pl.ds(band_start(blk), n_band), :], qm, nt,
                                   preferred_element_type=F32) + bias_ref[variant]
        return lax.dot_general(kd_ref[c0:c1, :], qm, nt, preferred_element_type=F32)

    def values_t(blk, chunk):
        c0, c1, in_band = chunk
        if in_band:
            return vt_ref[:, pl.ds(band_start(blk), n_band)]
        return vt_ref[:, c0:c1]

    def column_max(qm, blk):
        mx = None
        for chunk in chunks:
            s = logits(qm, blk, chunk)
            cm = jnp.max(s.reshape(s.shape[0] // 8, 8, nq), axis=0)
            mx = cm if mx is None else jnp.maximum(mx, cm)
        return jnp.max(mx, axis=0, keepdims=True)

    def finalize(t, acc, m):
        l = acc[HEAD_DIM:HEAD_DIM + 1, :]
        if use_sink:
            l = l + jnp.exp2(sink_ref[...] - m)
        o_t = acc[0:HEAD_DIM, :] / l
        o_t = jnp.concatenate([o_t[:, g * Q_BLOCK:(g + 1) * Q_BLOCK] for g in range(GROUP)], axis=0)
        o_ref[t * Q_BLOCK:(t + 1) * Q_BLOCK, :] = o_t.T.astype(BF16)

    def attend_all(qms, shifts):
        items = [(t, c) for t in range(q_per_step) for c in range(len(chunks))]
        blk_of = lambda t: step * q_per_step + t
        s_next = logits(qms[0], blk_of(0), chunks[0])
        acc = [None] * q_per_step
        pv_prev = None

        def retire(prev):
            t, c, pv = prev
            acc[t] = pv if acc[t] is None else acc[t] + pv
            if c == len(chunks) - 1:
                finalize(t, acc[t], shifts[t])

        for n, (t, c) in enumerate(items):
            s = s_next
            if n + 1 < len(items):
                t2, c2 = items[n + 1]
                s_next = logits(qms[t2], blk_of(t2), chunks[c2])
            p = jnp.exp2(s - shifts[t]).astype(BF16)
            pv = jnp.dot(values_t(blk_of(t), chunks[c]), p, preferred_element_type=F32)
            if pv_prev is not None:
                retire(pv_prev)
            pv_prev = (t, c, pv)
        retire(pv_prev)

    def with_sink(m):
        return jnp.maximum(m, sink_ref[...]) if use_sink else m

    bound = bound_ref[0]
    one_pass = bound <= LOGIT_BOUND_LIMIT

    @pl.when(one_pass)
    def _():
        m = with_sink(jnp.full((1, nq), bound, F32))
        qms = [extend_q(step * q_per_step + t) for t in range(q_per_step)]
        attend_all(qms, [m] * q_per_step)

    @pl.when(jnp.logical_not(one_pass))
    def _():
        qms = [extend_q(step * q_per_step + t) for t in range(q_per_step)]
        shifts = [with_sink(column_max(qms[t], step * q_per_step + t)) for t in range(q_per_step)]
        attend_all(qms, shifts)


def _attention(q, kd, vt, sink_rows, head_ones, logit_bound, *, q_blk_off, n_qblk, n_ctx, n_full, band,
               q_per_step, name):
    bsz, n_tok, _ = q.shape
    q_per_step = min(q_per_step, n_qblk)
    n_rows = n_tok if (band or n_full == n_tok) else max(n_full, (q_blk_off + n_qblk) * Q_BLOCK)
    assert n_qblk % q_per_step == 0
    use_sink = sink_rows is not None
    nq = GROUP * Q_BLOCK
    in_specs = [
        pl.BlockSpec((None, n_rows, GROUP * HEAD_DIM), lambda b, h, j: (b, 0, h)),
        pl.BlockSpec((None, n_rows, 2 * HEAD_DIM), lambda b, h, j: (b, 0, h)),
        pl.BlockSpec((None, None, VT_ROWS, n_rows), lambda b, h, j: (b, h, 0, 0)),
        pl.BlockSpec(head_ones.shape, lambda b, h, j: (0, 0)),
    ]
    args = [q, kd, vt, head_ones]
    if use_sink:
        in_specs.append(pl.BlockSpec((None, 1, nq), lambda b, h, j: (h, 0, 0)))
        args.append(sink_rows)
    if band:
        bias = _window_bias_variants(n_ctx, n_tok, n_qblk)
        in_specs.append(pl.BlockSpec(bias.shape, lambda b, h, j: (0, 0, 0)))
        args.append(bias)
    given_bound = logit_bound is not None
    if given_bound:
        in_specs.append(pl.BlockSpec(memory_space=pltpu.SMEM))
        args.append(logit_bound)
    kern = functools.partial(_attn_kernel, n_ctx=n_ctx, n_full=n_full, band=band, use_sink=use_sink,
                             n_tok=n_tok, n_lat_blk=n_qblk, q_row0=q_blk_off * Q_BLOCK,
                             q_per_step=q_per_step, given_bound=given_bound)
    rows = q_per_step * Q_BLOCK
    return pl.pallas_call(
        kern,
        grid=(bsz, N_KV, n_qblk // q_per_step),
        in_specs=in_specs,
        out_specs=pl.BlockSpec((None, rows, GROUP * HEAD_DIM), lambda b, h, j: (b, j, h)),
        out_shape=jax.ShapeDtypeStruct((bsz, n_qblk * Q_BLOCK, Q_COLS), BF16),
        scratch_shapes=[] if given_bound else [pltpu.SMEM((1,), F32)],
        compiler_params=pltpu.CompilerParams(
            dimension_semantics=("parallel", "parallel", "arbitrary"), vmem_limit_bytes=VMEM_LIMIT),
        name=name,
    )(*args)


def _post_kernel(*refs, n_sub, has_ctx):
    n_in = n_sub + (1 if has_ctx else 0)
    h_refs, ma_refs, mb_refs = refs[:n_sub], refs[n_in:n_in + n_sub], refs[2 * n_in:2 * n_in + n_sub]
    ctx_h = refs[n_sub] if has_ctx else None
    ctx_ma = refs[n_in + n_sub] if has_ctx else None
    ctx_mb = refs[2 * n_in + n_sub] if has_ctx else None
    mod_ref, wout_ref, gpm_ref, gpre_ref, wup_ref, wdn_ref, gpl_ref, o_ref = refs[3 * n_in:]
    mods = [_sub_mod(mod_ref, has_ctx, i) for i in range(n_sub)]

    def out_proj(i):
        mix = jnp.concatenate([_sub_tile(ma_refs, ctx_ma, i), _sub_tile(mb_refs, ctx_mb, i)], axis=1)
        return jnp.dot(mix, wout_ref[...], preferred_element_type=F32)

    def mixer_residual(i, mix):
        mod = mods[i]
        h = _sub_tile(h_refs, ctx_h, i) + mod[2:3, :] * _rms(mix, gpm_ref[...])
        u = (_rms(h, gpre_ref[...]) * (1 + mod[4:5, :]) + mod[3:4, :]).astype(BF16)
        return h, u

    def mlp_chunk(u, y, c0):
        a = jnp.dot(u, wup_ref[:, c0:c0 + FF_CHUNK], preferred_element_type=F32)
        a = jnp.square(jnp.maximum(a, 0.0)).astype(BF16)
        part = jnp.dot(a, wdn_ref[c0:c0 + FF_CHUNK, :], preferred_element_type=F32)
        return part if y is None else y + part

    def mlp_residual(i, h, y):
        o_ref[i * TOK_TILE:(i + 1) * TOK_TILE, :] = h + mods[i][5:6, :] * _rms(y, gpl_ref[...])

    hu = [None] * n_sub
    mix_prev = None
    for i in range(n_sub):
        mix = out_proj(i)
        if i > 0:
            hu[i - 1] = mixer_residual(i - 1, mix_prev)
        mix_prev = mix
    for i in range(n_sub):
        y = None
        for c, c0 in enumerate(range(0, D_FF, FF_CHUNK)):
            if i == 0 and c == 0:
                hu[n_sub - 1] = mixer_residual(n_sub - 1, mix_prev)
            y = mlp_chunk(hu[i][1], y, c0)
            if i > 0 and c == 0:
                mlp_residual(i - 1, hu[i - 1][0], y_prev)
        y_prev = y
    mlp_residual(n_sub - 1, hu[n_sub - 1][0], y_prev)


def _post(h_lat, h_ctx, lat_tile_off, n_tiles, mix_a, mix_b, ctx_mix_a, ctx_mix_b, mods, w_out,
          g_post_mix, g_pre_mlp, w_up, w_down, g_post_mlp):
    bsz = h_lat.shape[0]
    has_ctx = h_ctx is not None
    n_sub = max(s for s in POST_SUBS if n_tiles % s == 0)
    const = lambda b, t: (0, 0)
    resident = functools.partial(pl.BlockSpec, index_map=const, pipeline_mode=pl.Buffered(1))
    ctx_args = lambda v: [v] if has_ctx else []
    args = ([h_lat] * n_sub + ctx_args(h_ctx) + [mix_a] * n_sub + ctx_args(ctx_mix_a)
            + [mix_b] * n_sub + ctx_args(ctx_mix_b))
    return pl.pallas_call(
        functools.partial(_post_kernel, n_sub=n_sub, has_ctx=has_ctx),
        grid=(bsz, n_tiles // n_sub),
        in_specs=(_stream_specs(n_sub, D_MODEL, lat_tile_off, has_ctx)
                  + _stream_specs(n_sub, Q_COLS, 0, has_ctx)
                  + _stream_specs(n_sub, Q_COLS, 0, has_ctx) + [
            pl.BlockSpec((None, 2, N_MOD, D_MODEL), lambda b, t: (b, 0, 0, 0)),
            resident((D_MODEL, D_MODEL)),
            pl.BlockSpec((1, D_MODEL), const),
            pl.BlockSpec((1, D_MODEL), const),
            resident((D_MODEL, D_FF)),
            resident((D_FF, D_MODEL)),
            pl.BlockSpec((1, D_MODEL), const),
        ]),
        out_specs=pl.BlockSpec((None, n_sub * TOK_TILE, D_MODEL), lambda b, t: (b, t, 0)),
        out_shape=jax.ShapeDtypeStruct((bsz, n_tiles * TOK_TILE, D_MODEL), F32),
        compiler_params=pltpu.CompilerParams(
            dimension_semantics=("parallel", "parallel"), vmem_limit_bytes=VMEM_LIMIT),
        name="post",
    )(*args, mods, w_out, g_post_mix, g_pre_mlp, w_up, w_down, g_post_mlp)


def _rope_tables(n_ctx, n_lat):
    rows = n_lat // GRID_W
    row_ids = jnp.repeat(jnp.arange(rows, dtype=jnp.int32), GRID_W).astype(F32)
    col_ids = jnp.tile(jnp.arange(GRID_W, dtype=jnp.int32), rows).astype(F32)
    axis_dim = HEAD_DIM // 2
    inv = ROPE_THETA ** (-jnp.arange(0, axis_dim, 2, dtype=F32) / axis_dim)
    ang_r = row_ids[:, None] * inv[None, :]
    ang_c = col_ids[:, None] * inv[None, :]
    cos_r, sin_r, cos_c, sin_c = jnp.cos(ang_r), jnp.sin(ang_r), jnp.cos(ang_c), jnp.sin(ang_c)
    zero = jnp.zeros_like(sin_r)
    cos_h = jnp.concatenate([cos_r, cos_r, cos_c, cos_c], axis=-1)
    sa_h = jnp.concatenate([-sin_r, zero, -sin_c, zero], axis=-1)
    sb_h = jnp.concatenate([zero, sin_r, zero, sin_c], axis=-1)

    def full(t, ctx_value):
        t = jnp.tile(t, (1, LANES // HEAD_DIM))
        return jnp.concatenate([jnp.full((n_ctx, LANES), ctx_value, F32), t], axis=0)

    return full(cos_h, 1.0), full(sa_h, 0.0), full(sb_h, 0.0)


def _head_ones(n_cols):
    head = jnp.arange(n_cols, dtype=jnp.int32) // HEAD_DIM
    return (head[:, None] == head[None, :]).astype(BF16)


def kernel(x, c, ctx, c_ctx, w_ada, b_ada, g_pre_mix, g_post_mix, g_pre_mlp, g_post_mlp,
           w_in, q_norm, k_norm, sink, w_out, w_up, w_down):
    bsz, n_lat, d = x.shape
    n_ctx = ctx.shape[1]
    depth = w_in.shape[0]
    n_tok = n_ctx + n_lat
    assert d == D_MODEL and n_ctx == TOK_TILE and n_lat % TOK_TILE == 0 and bsz + 1 <= ADA_ROWS
    assert w_in.shape[2] == IN_COLS and w_up.shape[2] == D_FF

    cc = jnp.concatenate([c, c_ctx[None, :], jnp.zeros((ADA_ROWS - bsz - 1, d), F32)], axis=0)
    mod_all = _ada(cc, w_ada, b_ada)
    mod_all = mod_all.reshape(depth, ADA_ROWS, N_MOD, d)
    mod_ctx = jnp.broadcast_to(mod_all[:, bsz:bsz + 1], (depth, bsz, N_MOD, d))
    mods = jnp.stack([mod_ctx, mod_all[:, :bsz]], axis=2)

    cos_t, sa_t, sb_t = _rope_tables(n_ctx, n_lat)
    eq = _head_ones(Q_COLS)
    ek = _head_ones(KV_COLS)
    eh = _head_ones(GROUP * HEAD_DIM)
    row = lambda v: v.reshape(1, -1)

    w_in_b = w_in.astype(BF16)
    w_out_b = w_out.astype(BF16)
    w_up_b = w_up.astype(BF16)
    w_down_b = w_down.astype(BF16)

    ctx_tiles = n_ctx // TOK_TILE
    assert depth >= 1 and ctx_tiles == 1
    ctx_qblk = n_ctx // Q_BLOCK
    lat_qblk = n_lat // Q_BLOCK

    h_lat, h_ctx = x, ctx
    lat_off = 0
    for l in range(depth):
        last = l == depth - 1
        gq = row(jnp.tile(q_norm[l], N_HEADS))
        gk = row(jnp.tile(k_norm[l], N_KV))
        sink_rows = jnp.repeat(sink[l].reshape(N_KV, 1, GROUP) * LOG2_E, Q_BLOCK, axis=2)

        qa, kda, vta, qb, kdb, vtb = _proj(h_lat, h_ctx, 0, n_tok // TOK_TILE, mods[l],
                                           row(g_pre_mix[l]), w_in_b[l], gq, gk, eq, ek,
                                           cos_t, sa_t, sb_t)
        bound_a = (HEAD_DIM * Q_SCALE * BOUND_SLACK * jnp.max(jnp.abs(q_norm[l]))
                   * jnp.max(jnp.abs(k_norm[l]))).reshape(1)
        oa = _attention(qa, kda, vta, None, eh, bound_a, q_blk_off=ctx_qblk, n_qblk=lat_qblk,
                        n_ctx=n_ctx, n_full=n_tok, band=False, q_per_step=Q_PER_STEP_GLOBAL,
                        name="attn_global")
        ob = _attention(qb, kdb, vtb, sink_rows, eh, None, q_blk_off=ctx_qblk, n_qblk=lat_qblk,
                        n_ctx=n_ctx, n_full=n_ctx, band=True, q_per_step=Q_PER_STEP_WINDOW,
                        name="attn_window")
        post = functools.partial(_post, mods=mods[l], w_out=w_out_b[l], g_post_mix=row(g_post_mix[l]),
                                 g_pre_mlp=row(g_pre_mlp[l]), w_up=w_up_b[l], w_down=w_down_b[l],
                                 g_post_mlp=row(g_post_mlp[l]))
        if not last:
            oac = _attention(qa, kda, vta, None, eh, bound_a, q_blk_off=0, n_qblk=ctx_qblk, n_ctx=n_ctx,
                             n_full=n_ctx, band=False, q_per_step=ctx_qblk, name="attn_ctx_global")
            obc = _attention(qb, kdb, vtb, sink_rows, eh, None, q_blk_off=0, n_qblk=ctx_qblk, n_ctx=n_ctx,
                             n_full=n_ctx, band=False, q_per_step=ctx_qblk, name="attn_ctx_sink")
            assert h_ctx is not None
            h_lat = post(h_lat, h_ctx, lat_off, n_tok // TOK_TILE, oa, ob, oac, obc)
            h_ctx, lat_off = None, ctx_tiles
        else:
            if h_ctx is not None:
                h_lat = post(h_lat, None, 0, n_lat // TOK_TILE, oa, ob, None, None)
            else:
                h_lat = post(h_lat, None, lat_off, n_lat // TOK_TILE, oa, ob, None, None)
    return h_lat
```

```python
import functools

import jax
import jax.numpy as jnp
from jax import lax
from jax.experimental import pallas as pl
from jax.experimental.pallas import tpu as pltpu

F32 = jnp.float32
BF16 = jnp.bfloat16

D_MODEL = 1024
HEAD_DIM = 64
N_HEADS = 8
N_KV = 2
GROUP = N_HEADS // N_KV
Q_COLS = N_HEADS * HEAD_DIM
KV_COLS = N_KV * HEAD_DIM
IN_COLS = 2 * (Q_COLS + 2 * KV_COLS)
D_FF = 4 * D_MODEL
GRID_W = 64
WINDOW = 128
Q_BLOCK = 128
ROPE_THETA = 10000.0
EPS = 1e-6
NEG_BIG = -1e30
N_MOD = 6
ROPE_SHIFT = HEAD_DIM // 4
LOG2_E = 1.4426950408889634
Q_SCALE = HEAD_DIM ** -0.5 * LOG2_E

LANES = 128
TOK_TILE = 256
PROJ_SUB = 9
POST_SUBS = (1, 2, 3)
VT_ROWS = HEAD_DIM + 16
KEY_CHUNK = 2304
Q_PER_STEP_GLOBAL = 8
Q_PER_STEP_WINDOW = 16
LOGIT_BOUND_LIMIT = 48.0
BOUND_SLACK = 1.01
FF_CHUNK = 1024
ADA_ROWS = 32
ADA_COLS = 1536
VMEM_LIMIT = 56 * 1024 * 1024


def _rms(x, g):
    return x * lax.rsqrt(jnp.mean(x * x, axis=-1, keepdims=True) + EPS) * g


def _ada_kernel(c_ref, w_ref, b_ref, o_ref):
    c = c_ref[...]
    a = c * jax.nn.sigmoid(c)
    w = w_ref[...]
    a_hi = a.astype(BF16)
    a_lo = (a - a_hi.astype(F32)).astype(BF16)
    w_hi = w.astype(BF16)
    w_lo = (w - w_hi.astype(F32)).astype(BF16)
    acc = jnp.dot(a_hi, w_hi, preferred_element_type=F32)
    acc += jnp.dot(a_lo, w_hi, preferred_element_type=F32)
    acc += jnp.dot(a_hi, w_lo, preferred_element_type=F32)
    o_ref[...] = acc + b_ref[...]


def _ada(cc, w_ada, b_ada):
    depth = w_ada.shape[0]
    n_out = w_ada.shape[2]
    return pl.pallas_call(
        _ada_kernel,
        grid=(depth, n_out // ADA_COLS),
        in_specs=[
            pl.BlockSpec((ADA_ROWS, D_MODEL), lambda l, j: (0, 0)),
            pl.BlockSpec((None, D_MODEL, ADA_COLS), lambda l, j: (l, 0, j)),
            pl.BlockSpec((None, 1, ADA_COLS), lambda l, j: (l, 0, j)),
        ],
        out_specs=pl.BlockSpec((None, ADA_ROWS, ADA_COLS), lambda l, j: (l, 0, j)),
        out_shape=jax.ShapeDtypeStruct((depth, ADA_ROWS, n_out), F32),
        compiler_params=pltpu.CompilerParams(
            dimension_semantics=("arbitrary", "arbitrary"), vmem_limit_bytes=VMEM_LIMIT),
        name="ada",
    )(cc, w_ada, b_ada.reshape(depth, 1, n_out))


def _stream_specs(n_sub, block_cols, lat_tile_off, has_ctx):
    specs = []
    for i in range(n_sub):
        if has_ctx:
            idx = lambda b, t, i=i: (b, jnp.maximum(n_sub * t + i - 1, 0), 0)
        else:
            idx = lambda b, t, i=i: (b, lat_tile_off + n_sub * t + i, 0)
        specs.append(pl.BlockSpec((None, TOK_TILE, block_cols), idx))
    if has_ctx:
        specs.append(pl.BlockSpec((None, TOK_TILE, block_cols), lambda b, t: (b, 0, 0)))
    return specs


def _sub_tile(refs, ctx_ref, i):
    v = refs[i][...]
    if ctx_ref is not None and i == 0:
        v = jnp.where(pl.program_id(1) == 0, ctx_ref[...], v)
    return v


def _sub_mod(mod_ref, ctx_first, i):
    if ctx_first and i == 0:
        return jnp.where(pl.program_id(1) == 0, mod_ref[0], mod_ref[1])
    return mod_ref[1]


def _proj_kernel(*refs, n_sub, has_ctx):
    h_refs = refs[:n_sub]
    refs = refs[n_sub:]
    ctx_ref = None
    if has_ctx:
        ctx_ref, refs = refs[0], refs[1:]
    (mod_ref, gpre_ref, w_ref, gq_ref, gk_ref, eq_ref, ek_ref, cos_ref, sa_ref, sb_ref,
     qa_ref, kda_ref, vta_ref, qb_ref, kdb_ref, vtb_ref) = refs
    kv = 2 * KV_COLS

    def prenorm(i):
        mod = _sub_mod(mod_ref, True, i)
        x = _sub_tile(h_refs, ctx_ref, i)
        return (_rms(x, gpre_ref[...]) * (1 + mod[1:2, :]) + mod[0:1, :]).astype(BF16)

    def stages(i, u):
        rows = slice(i * TOK_TILE, (i + 1) * TOK_TILE)
        cos = cos_ref[rows, :]
        sa = sa_ref[rows, :]
        sb = sb_ref[rows, :]

        def rope(t):
            up = pltpu.roll(t, LANES - ROPE_SHIFT, 1)
            dn = pltpu.roll(t, ROPE_SHIFT, 1)
            return t * cos + up * sa + dn * sb

        def head_norm(t, e_ref, g_ref):
            ss = jnp.dot((t * t).astype(BF16), e_ref[...], preferred_element_type=F32)
            return t * lax.rsqrt(ss * (1.0 / HEAD_DIM) + EPS) * g_ref[...]

        def store_q(q, qt_ref):
            q = jnp.concatenate([rope(q[:, j * LANES:(j + 1) * LANES]) for j in range(Q_COLS // LANES)],
                                axis=1) * Q_SCALE
            q_t = q.T.astype(BF16)
            for h in range(N_HEADS):
                qt_ref[h, :, rows] = q_t[h * HEAD_DIM:(h + 1) * HEAD_DIM, :]

        def store_k(k, k_ref):
            k_ref[rows, :] = rope(k).astype(BF16)

        def store_vt(v, vt_ref):
            v_t = v.T.astype(BF16)
            ones = jnp.ones((VT_ROWS - HEAD_DIM, TOK_TILE), BF16)
            for h in range(N_KV):
                vt_ref[h, 0:HEAD_DIM, rows] = v_t[h * HEAD_DIM:(h + 1) * HEAD_DIM, :]
                vt_ref[h, HEAD_DIM:VT_ROWS, rows] = ones

        def project(c0, n):
            return lambda: jnp.dot(u, w_ref[:, c0:c0 + n], preferred_element_type=F32)

        def epi_qa(z):
            store_q(head_norm(z, eq_ref, gq_ref), qa_ref)

        def epi_kva(z):
            store_k(head_norm(z[:, 0:KV_COLS], ek_ref, gk_ref), kda_ref)
            store_vt(z[:, KV_COLS:kv], vta_ref)

        def epi_qb(z):
            store_q(z, qb_ref)

        def epi_kvb(z):
            store_k(z[:, 0:KV_COLS], kdb_ref)
            store_vt(z[:, KV_COLS:kv], vtb_ref)

        return [(project(0, Q_COLS), epi_qa), (project(Q_COLS, kv), epi_kva),
                (project(Q_COLS + kv, Q_COLS), epi_qb), (project(2 * Q_COLS + kv, kv), epi_kvb)]

    u_next = prenorm(0)
    pending = []
    for i in range(n_sub):
        u = u_next
        results = []
        for g, (matmul, epilogue) in enumerate(stages(i, u)):
            results.append((epilogue, matmul()))
            if g == 0 and i + 1 < n_sub:
                u_next = prenorm(i + 1)
            if pending:
                epi, z = pending.pop(0)
                epi(z)
        for epi, z in pending:
            epi(z)
        pending = results
    for epi, z in pending:
        epi(z)


def _proj(h_lat, h_ctx, lat_tile_off, n_tiles, mods, g_pre, w_in, gq, gk, eq, ek, cos_t, sa_t, sb_t):
    bsz = h_lat.shape[0]
    has_ctx = h_ctx is not None
    n_sub = PROJ_SUB
    assert n_tiles % n_sub == 0
    n_tok = n_tiles * TOK_TILE
    rows = n_sub * TOK_TILE
    const = lambda b, t: (0, 0)
    tok = lambda b, t: (b, t, 0)
    q_shape = jax.ShapeDtypeStruct((bsz, N_HEADS, HEAD_DIM, n_tok), BF16)
    kd_shape = jax.ShapeDtypeStruct((bsz, n_tok, KV_COLS), BF16)
    vt_shape = jax.ShapeDtypeStruct((bsz, N_KV, VT_ROWS, n_tok), BF16)
    q_spec = pl.BlockSpec((None, N_HEADS, HEAD_DIM, rows), lambda b, t: (b, 0, 0, t))
    kd_spec = pl.BlockSpec((None, rows, KV_COLS), tok)
    vt_spec = pl.BlockSpec((None, N_KV, VT_ROWS, rows), lambda b, t: (b, 0, 0, t))
    rope_spec = pl.BlockSpec((rows, LANES), lambda b, t: (t, 0))
    h_args = [h_lat] * n_sub + ([h_ctx] if has_ctx else [])
    return pl.pallas_call(
        functools.partial(_proj_kernel, n_sub=n_sub, has_ctx=has_ctx),
        grid=(bsz, n_tiles // n_sub),
        in_specs=_stream_specs(n_sub, D_MODEL, lat_tile_off, has_ctx) + [
            pl.BlockSpec((None, 2, N_MOD, D_MODEL), lambda b, t: (b, 0, 0, 0)),
            pl.BlockSpec((1, D_MODEL), const),
            pl.BlockSpec((D_MODEL, IN_COLS), const),
            pl.BlockSpec((1, Q_COLS), const),
            pl.BlockSpec((1, KV_COLS), const),
            pl.BlockSpec((Q_COLS, Q_COLS), const),
            pl.BlockSpec((KV_COLS, KV_COLS), const),
            rope_spec, rope_spec, rope_spec,
        ],
        out_specs=[q_spec, kd_spec, vt_spec, q_spec, kd_spec, vt_spec],
        out_shape=[q_shape, kd_shape, vt_shape, q_shape, kd_shape, vt_shape],
        compiler_params=pltpu.CompilerParams(
            dimension_semantics=("parallel", "parallel"), vmem_limit_bytes=VMEM_LIMIT),
        name="proj",
    )(*h_args, mods, g_pre, w_in, gq, gk, eq, ek, cos_t, sa_t, sb_t)


def _band_start(blk, n_ctx, n_tok):
    return jnp.minimum(n_ctx + (blk - 1) * Q_BLOCK, n_tok - (Q_BLOCK + 2 * WINDOW))


def _window_bias_variants(n_ctx, n_tok, n_lat_blk):
    n_band = Q_BLOCK + 2 * WINDOW
    kj = jnp.arange(n_band, dtype=jnp.int32)[:, None]
    qr = jnp.arange(GROUP * Q_BLOCK, dtype=jnp.int32)[None, :] % Q_BLOCK
    variants = []
    for blk in (0, 1, n_lat_blk - 1):
        kpos = kj + (_band_start(blk, n_ctx, n_tok) - n_ctx)
        qpos = qr + blk * Q_BLOCK
        valid = (jnp.abs(kpos - qpos) <= WINDOW) & (kpos >= 0)
        variants.append(jnp.where(valid, 0.0, NEG_BIG).astype(F32))
    return jnp.stack(variants)


def _attn_kernel(*refs, n_ctx, n_full, band, use_sink, n_tok, n_lat_blk, q_row0, q_per_step,
                 given_bound):
    refs = list(refs)
    qt_ref, k_ref, vt_ref, eh_ref = refs[:4]
    rest = refs[4:]
    sink_ref = rest.pop(0) if use_sink else None
    bias_ref = rest.pop(0) if band else None
    if given_bound:
        bound_ref, o_ref = rest
    else:
        o_ref, bound_ref = rest
    nq = GROUP * Q_BLOCK
    n_band = Q_BLOCK + 2 * WINDOW if band else 0
    chunks = [(c0, min(c0 + KEY_CHUNK, n_full), False) for c0 in range(0, n_full, KEY_CHUNK)]
    if band:
        chunks.append((n_full, n_full + n_band, True))
    kv_head = pl.program_id(1)
    step = pl.program_id(2)

    if not given_bound:
        @pl.when(step == 0)
        def _():
            q = qt_ref[...].astype(F32)
            qq = jnp.sum(q * q, axis=1)
            k = k_ref[...].astype(F32)
            kk = jnp.dot((k * k).astype(BF16), eh_ref[...], preferred_element_type=F32)
            bound_ref[0] = jnp.sqrt(jnp.max(qq) * jnp.max(kk)) * BOUND_SLACK

    def band_start(blk):
        return pl.multiple_of(_band_start(blk, n_ctx, n_tok), LANES)

    def extend_q(blk):
        cols = pl.ds(pl.multiple_of(q_row0 + blk * Q_BLOCK, Q_BLOCK), Q_BLOCK)
        top = jnp.concatenate([qt_ref[g, :, cols] for g in range(GROUP)], axis=1)
        zero = jnp.zeros_like(top)
        return jnp.where(kv_head == 0, jnp.concatenate([top, zero], axis=0),
                         jnp.concatenate([zero, top], axis=0))

    def logits(qm, blk, chunk):
        c0, c1, in_band = chunk
        if in_band:
            variant = jnp.where(blk == 0, 0, jnp.where(blk >= n_lat_blk - 1, 2, 1))
            return jnp.dot(k_ref[pl.ds(band_start(blk), n_band), :], qm,
                           preferred_element_type=F32) + bias_ref[variant]
        return jnp.dot(k_ref[c0:c1, :], qm, preferred_element_type=F32)

    def values_t(blk, chunk):
        c0, c1, in_band = chunk
        if in_band:
            return vt_ref[:, pl.ds(band_start(blk), n_band)]
        return vt_ref[:, c0:c1]

    def column_max(qm, blk):
        mx = None
        for chunk in chunks:
            s = logits(qm, blk, chunk)
            cm = jnp.max(s.reshape(s.shape[0] // 8, 8, nq), axis=0)
            mx = cm if mx is None else jnp.maximum(mx, cm)
        return jnp.max(mx, axis=0, keepdims=True)

    def finalize(t, acc, m):
        l = acc[HEAD_DIM:HEAD_DIM + 1, :]
        if use_sink:
            l = l + jnp.exp2(sink_ref[...] - m)
        o_t = acc[0:HEAD_DIM, :] / l
        o_t = jnp.concatenate([o_t[:, g * Q_BLOCK:(g + 1) * Q_BLOCK] for g in range(GROUP)], axis=0)
        o_ref[t * Q_BLOCK:(t + 1) * Q_BLOCK, :] = o_t.T.astype(BF16)

    def attend_all(qms, shifts):
        items = [(t, c) for t in range(q_per_step) for c in range(len(chunks))]
        blk_of = lambda t: step * q_per_step + t
        s_next = logits(qms[0], blk_of(0), chunks[0])
        acc = [None] * q_per_step
        pv_prev = None

        def retire(prev):
            t, c, pv = prev
            acc[t] = pv if acc[t] is None else acc[t] + pv
            if c == len(chunks) - 1:
                finalize(t, acc[t], shifts[t])

        for n, (t, c) in enumerate(items):
            s = s_next
            if n + 1 < len(items):
                t2, c2 = items[n + 1]
                s_next = logits(qms[t2], blk_of(t2), chunks[c2])
            p = jnp.exp2(s - shifts[t]).astype(BF16)
            pv = jnp.dot(values_t(blk_of(t), chunks[c]), p, preferred_element_type=F32)
            if pv_prev is not None:
                retire(pv_prev)
            pv_prev = (t, c, pv)
        retire(pv_prev)

    def with_sink(m):
        return jnp.maximum(m, sink_ref[...]) if use_sink else m

    bound = bound_ref[0]
    one_pass = bound <= LOGIT_BOUND_LIMIT

    @pl.when(one_pass)
    def _():
        m = with_sink(jnp.full((1, nq), bound, F32))
        qms = [extend_q(step * q_per_step + t) for t in range(q_per_step)]
        attend_all(qms, [m] * q_per_step)

    @pl.when(jnp.logical_not(one_pass))
    def _():
        qms = [extend_q(step * q_per_step + t) for t in range(q_per_step)]
        shifts = [with_sink(column_max(qms[t], step * q_per_step + t)) for t in range(q_per_step)]
        attend_all(qms, shifts)


def _attention(q, kd, vt, sink_rows, head_ones, logit_bound, *, q_blk_off, n_qblk, n_ctx, n_full, band,
               q_per_step, name):
    bsz, n_tok, _ = kd.shape
    q_per_step = min(q_per_step, n_qblk)
    n_rows = n_tok if (band or n_full == n_tok) else max(n_full, (q_blk_off + n_qblk) * Q_BLOCK)
    assert n_qblk % q_per_step == 0
    use_sink = sink_rows is not None
    nq = GROUP * Q_BLOCK
    in_specs = [
        pl.BlockSpec((None, GROUP, HEAD_DIM, n_rows), lambda b, h, j: (b, h, 0, 0)),
        pl.BlockSpec((None, n_rows, KV_COLS), lambda b, h, j: (b, 0, 0)),
        pl.BlockSpec((None, None, VT_ROWS, n_rows), lambda b, h, j: (b, h, 0, 0)),
        pl.BlockSpec(head_ones.shape, lambda b, h, j: (0, 0)),
    ]
    args = [q, kd, vt, head_ones]
    if use_sink:
        in_specs.append(pl.BlockSpec((None, 1, nq), lambda b, h, j: (h, 0, 0)))
        args.append(sink_rows)
    if band:
        bias = _window_bias_variants(n_ctx, n_tok, n_qblk)
        in_specs.append(pl.BlockSpec(bias.shape, lambda b, h, j: (0, 0, 0)))
        args.append(bias)
    given_bound = logit_bound is not None
    if given_bound:
        in_specs.append(pl.BlockSpec(memory_space=pltpu.SMEM))
        args.append(logit_bound)
    kern = functools.partial(_attn_kernel, n_ctx=n_ctx, n_full=n_full, band=band, use_sink=use_sink,
                             n_tok=n_tok, n_lat_blk=n_qblk, q_row0=q_blk_off * Q_BLOCK,
                             q_per_step=q_per_step, given_bound=given_bound)
    rows = q_per_step * Q_BLOCK
    return pl.pallas_call(
        kern,
        grid=(bsz, N_KV, n_qblk // q_per_step),
        in_specs=in_specs,
        out_specs=pl.BlockSpec((None, rows, GROUP * HEAD_DIM), lambda b, h, j: (b, j, h)),
        out_shape=jax.ShapeDtypeStruct((bsz, n_qblk * Q_BLOCK, Q_COLS), BF16),
        scratch_shapes=[] if given_bound else [pltpu.SMEM((1,), F32)],
        compiler_params=pltpu.CompilerParams(
            dimension_semantics=("parallel", "parallel", "arbitrary"), vmem_limit_bytes=VMEM_LIMIT),
        name=name,
    )(*args)


def _post_kernel(*refs, n_sub, has_ctx):
    n_in = n_sub + (1 if has_ctx else 0)
    h_refs, ma_refs, mb_refs = refs[:n_sub], refs[n_in:n_in + n_sub], refs[2 * n_in:2 * n_in + n_sub]
    ctx_h = refs[n_sub] if has_ctx else None
    ctx_ma = refs[n_in + n_sub] if has_ctx else None
    ctx_mb = refs[2 * n_in + n_sub] if has_ctx else None
    mod_ref, wout_ref, gpm_ref, gpre_ref, wup_ref, wdn_ref, gpl_ref, o_ref = refs[3 * n_in:]
    mods = [_sub_mod(mod_ref, has_ctx, i) for i in range(n_sub)]

    def out_proj(i):
        mix = jnp.concatenate([_sub_tile(ma_refs, ctx_ma, i), _sub_tile(mb_refs, ctx_mb, i)], axis=1)
        return jnp.dot(mix, wout_ref[...], preferred_element_type=F32)

    def mixer_residual(i, mix):
        mod = mods[i]
        h = _sub_tile(h_refs, ctx_h, i) + mod[2:3, :] * _rms(mix, gpm_ref[...])
        u = (_rms(h, gpre_ref[...]) * (1 + mod[4:5, :]) + mod[3:4, :]).astype(BF16)
        return h, u

    def mlp_chunk(u, y, c0):
        a = jnp.dot(u, wup_ref[:, c0:c0 + FF_CHUNK], preferred_element_type=F32)
        a = jnp.square(jnp.maximum(a, 0.0)).astype(BF16)
        part = jnp.dot(a, wdn_ref[c0:c0 + FF_CHUNK, :], preferred_element_type=F32)
        return part if y is None else y + part

    def mlp_residual(i, h, y):
        o_ref[i * TOK_TILE:(i + 1) * TOK_TILE, :] = h + mods[i][5:6, :] * _rms(y, gpl_ref[...])

    hu = [None] * n_sub
    mix_prev = None
    for i in range(n_sub):
        mix = out_proj(i)
        if i > 0:
            hu[i - 1] = mixer_residual(i - 1, mix_prev)
        mix_prev = mix
    for i in range(n_sub):
        y = None
        for c, c0 in enumerate(range(0, D_FF, FF_CHUNK)):
            if i == 0 and c == 0:
                hu[n_sub - 1] = mixer_residual(n_sub - 1, mix_prev)
            y = mlp_chunk(hu[i][1], y, c0)
            if i > 0 and c == 0:
                mlp_residual(i - 1, hu[i - 1][0], y_prev)
        y_prev = y
    mlp_residual(n_sub - 1, hu[n_sub - 1][0], y_prev)


def _post(h_lat, h_ctx, lat_tile_off, n_tiles, mix_a, mix_b, ctx_mix_a, ctx_mix_b, mods, w_out,
          g_post_mix, g_pre_mlp, w_up, w_down, g_post_mlp):
    bsz = h_lat.shape[0]
    has_ctx = h_ctx is not None
    n_sub = max(s for s in POST_SUBS if n_tiles % s == 0)
    const = lambda b, t: (0, 0)
    resident = functools.partial(pl.BlockSpec, index_map=const, pipeline_mode=pl.Buffered(1))
    ctx_args = lambda v: [v] if has_ctx else []
    args = ([h_lat] * n_sub + ctx_args(h_ctx) + [mix_a] * n_sub + ctx_args(ctx_mix_a)
            + [mix_b] * n_sub + ctx_args(ctx_mix_b))
    return pl.pallas_call(
        functools.partial(_post_kernel, n_sub=n_sub, has_ctx=has_ctx),
        grid=(bsz, n_tiles // n_sub),
        in_specs=(_stream_specs(n_sub, D_MODEL, lat_tile_off, has_ctx)
                  + _stream_specs(n_sub, Q_COLS, 0, has_ctx)
                  + _stream_specs(n_sub, Q_COLS, 0, has_ctx) + [
            pl.BlockSpec((None, 2, N_MOD, D_MODEL), lambda b, t: (b, 0, 0, 0)),
            resident((D_MODEL, D_MODEL)),
            pl.BlockSpec((1, D_MODEL), const),
            pl.BlockSpec((1, D_MODEL), const),
            resident((D_MODEL, D_FF)),
            resident((D_FF, D_MODEL)),
            pl.BlockSpec((1, D_MODEL), const),
        ]),
        out_specs=pl.BlockSpec((None, n_sub * TOK_TILE, D_MODEL), lambda b, t: (b, t, 0)),
        out_shape=jax.ShapeDtypeStruct((bsz, n_tiles * TOK_TILE, D_MODEL), F32),
        compiler_params=pltpu.CompilerParams(
            dimension_semantics=("parallel", "parallel"), vmem_limit_bytes=VMEM_LIMIT),
        name="post",
    )(*args, mods, w_out, g_post_mix, g_pre_mlp, w_up, w_down, g_post_mlp)


def _rope_tables(n_ctx, n_lat):
    rows = n_lat // GRID_W
    row_ids = jnp.repeat(jnp.arange(rows, dtype=jnp.int32), GRID_W).astype(F32)
    col_ids = jnp.tile(jnp.arange(GRID_W, dtype=jnp.int32), rows).astype(F32)
    axis_dim = HEAD_DIM // 2
    inv = ROPE_THETA ** (-jnp.arange(0, axis_dim, 2, dtype=F32) / axis_dim)
    ang_r = row_ids[:, None] * inv[None, :]
    ang_c = col_ids[:, None] * inv[None, :]
    cos_r, sin_r, cos_c, sin_c = jnp.cos(ang_r), jnp.sin(ang_r), jnp.cos(ang_c), jnp.sin(ang_c)
    zero = jnp.zeros_like(sin_r)
    cos_h = jnp.concatenate([cos_r, cos_r, cos_c, cos_c], axis=-1)
    sa_h = jnp.concatenate([-sin_r, zero, -sin_c, zero], axis=-1)
    sb_h = jnp.concatenate([zero, sin_r, zero, sin_c], axis=-1)

    def full(t, ctx_value):
        t = jnp.tile(t, (1, LANES // HEAD_DIM))
        return jnp.concatenate([jnp.full((n_ctx, LANES), ctx_value, F32), t], axis=0)

    return full(cos_h, 1.0), full(sa_h, 0.0), full(sb_h, 0.0)


def _head_ones(n_cols):
    head = jnp.arange(n_cols, dtype=jnp.int32) // HEAD_DIM
    return (head[:, None] == head[None, :]).astype(BF16)


def kernel(x, c, ctx, c_ctx, w_ada, b_ada, g_pre_mix, g_post_mix, g_pre_mlp, g_post_mlp,
           w_in, q_norm, k_norm, sink, w_out, w_up, w_down):
    bsz, n_lat, d = x.shape
    n_ctx = ctx.shape[1]
    depth = w_in.shape[0]
    n_tok = n_ctx + n_lat
    assert d == D_MODEL and n_ctx == TOK_TILE and n_lat % TOK_TILE == 0 and bsz + 1 <= ADA_ROWS
    assert w_in.shape[2] == IN_COLS and w_up.shape[2] == D_FF

    cc = jnp.concatenate([c, c_ctx[None, :], jnp.zeros((ADA_ROWS - bsz - 1, d), F32)], axis=0)
    mod_all = _ada(cc, w_ada, b_ada)
    mod_all = mod_all.reshape(depth, ADA_ROWS, N_MOD, d)
    mod_ctx = jnp.broadcast_to(mod_all[:, bsz:bsz + 1], (depth, bsz, N_MOD, d))
    mods = jnp.stack([mod_ctx, mod_all[:, :bsz]], axis=2)

    cos_t, sa_t, sb_t = _rope_tables(n_ctx, n_lat)
    eq = _head_ones(Q_COLS)
    ek = _head_ones(KV_COLS)
    eh = _head_ones(KV_COLS)
    row = lambda v: v.reshape(1, -1)

    w_in_b = w_in.astype(BF16)
    w_out_b = w_out.astype(BF16)
    w_up_b = w_up.astype(BF16)
    w_down_b = w_down.astype(BF16)

    ctx_tiles = n_ctx // TOK_TILE
    assert depth >= 1 and ctx_tiles == 1
    ctx_qblk = n_ctx // Q_BLOCK
    lat_qblk = n_lat // Q_BLOCK

    h_lat, h_ctx = x, ctx
    lat_off = 0
    for l in range(depth):
        last = l == depth - 1
        gq = row(jnp.tile(q_norm[l], N_HEADS))
        gk = row(jnp.tile(k_norm[l], N_KV))
        sink_rows = jnp.repeat(sink[l].reshape(N_KV, 1, GROUP) * LOG2_E, Q_BLOCK, axis=2)

        qa, kda, vta, qb, kdb, vtb = _proj(h_lat, h_ctx, 0, n_tok // TOK_TILE, mods[l],
                                           row(g_pre_mix[l]), w_in_b[l], gq, gk, eq, ek,
                                           cos_t, sa_t, sb_t)
        bound_a = (HEAD_DIM * Q_SCALE * BOUND_SLACK * jnp.max(jnp.abs(q_norm[l]))
                   * jnp.max(jnp.abs(k_norm[l]))).reshape(1)
        oa = _attention(qa, kda, vta, None, eh, bound_a, q_blk_off=ctx_qblk, n_qblk=lat_qblk,
                        n_ctx=n_ctx, n_full=n_tok, band=False, q_per_step=Q_PER_STEP_GLOBAL,
                        name="attn_global")
        ob = _attention(qb, kdb, vtb, sink_rows, eh, None, q_blk_off=ctx_qblk, n_qblk=lat_qblk,
                        n_ctx=n_ctx, n_full=n_ctx, band=True, q_per_step=Q_PER_STEP_WINDOW,
                        name="attn_window")
        post = functools.partial(_post, mods=mods[l], w_out=w_out_b[l], g_post_mix=row(g_post_mix[l]),
                                 g_pre_mlp=row(g_pre_mlp[l]), w_up=w_up_b[l], w_down=w_down_b[l],
                                 g_post_mlp=row(g_post_mlp[l]))
        if not last:
            oac = _attention(qa, kda, vta, None, eh, bound_a, q_blk_off=0, n_qblk=ctx_qblk, n_ctx=n_ctx,
                             n_full=n_ctx, band=False, q_per_step=ctx_qblk, name="attn_ctx_global")
            obc = _attention(qb, kdb, vtb, sink_rows, eh, None, q_blk_off=0, n_qblk=ctx_qblk, n_ctx=n_ctx,
                             n_full=n_ctx, band=False, q_per_step=ctx_qblk, name="attn_ctx_sink")
            assert h_ctx is not None
            h_lat = post(h_lat, h_ctx, lat_off, n_tok // TOK_TILE, oa, ob, oac, obc)
            h_ctx, lat_off = None, ctx_tiles
        else:
            if h_ctx is not None:
                h_lat = post(h_lat, None, 0, n_lat // TOK_TILE, oa, ob, None, None)
            else:
                h_lat = post(h_lat, None, lat_off, n_lat // TOK_TILE, oa, ob, None, None)
    return h_lat
```

```python
import functools

import jax
import jax.numpy as jnp
from jax import lax
from jax.experimental import pallas as pl
from jax.experimental.pallas import tpu as pltpu

F32 = jnp.float32
BF16 = jnp.bfloat16

D_MODEL = 1024
HEAD_DIM = 64
N_HEADS = 8
N_KV = 2
GROUP = N_HEADS // N_KV
Q_COLS = N_HEADS * HEAD_DIM
KV_COLS = N_KV * HEAD_DIM
IN_COLS = 2 * (Q_COLS + 2 * KV_COLS)
D_FF = 4 * D_MODEL
GRID_W = 64
WINDOW = 128
Q_BLOCK = 128
ROPE_THETA = 10000.0
EPS = 1e-6
NEG_BIG = -1e30
N_MOD = 6
ROPE_SHIFT = HEAD_DIM // 4
LOG2_E = 1.4426950408889634
Q_SCALE = HEAD_DIM ** -0.5 * LOG2_E

LANES = 128
TOK_TILE = 256
PROJ_SUB = 9
POST_SUBS = (1, 2, 3, 4)
VT_ROWS = HEAD_DIM + 16
KEY_CHUNK = 2304
Q_PER_STEP_GLOBAL = 8
Q_PER_STEP_WINDOW = 16
LOGIT_BOUND_LIMIT = 48.0
BOUND_SLACK = 1.01
FF_CHUNK = 1024
ADA_ROWS = 32
ADA_COLS = 1536
VMEM_LIMIT = 56 * 1024 * 1024


def _rms(x, g):
    return x * lax.rsqrt(jnp.mean(x * x, axis=-1, keepdims=True) + EPS) * g


def _ada_kernel(c_ref, w_ref, b_ref, o_ref):
    c = c_ref[...]
    a = c * jax.nn.sigmoid(c)
    w = w_ref[...]
    a_hi = a.astype(BF16)
    a_lo = (a - a_hi.astype(F32)).astype(BF16)
    w_hi = w.astype(BF16)
    w_lo = (w - w_hi.astype(F32)).astype(BF16)
    acc = jnp.dot(a_hi, w_hi, preferred_element_type=F32)
    acc += jnp.dot(a_lo, w_hi, preferred_element_type=F32)
    acc += jnp.dot(a_hi, w_lo, preferred_element_type=F32)
    o_ref[...] = acc + b_ref[...]


def _ada(cc, w_ada, b_ada):
    depth = w_ada.shape[0]
    n_out = w_ada.shape[2]
    return pl.pallas_call(
        _ada_kernel,
        grid=(depth, n_out // ADA_COLS),
        in_specs=[
            pl.BlockSpec((ADA_ROWS, D_MODEL), lambda l, j: (0, 0)),
            pl.BlockSpec((None, D_MODEL, ADA_COLS), lambda l, j: (l, 0, j)),
            pl.BlockSpec((None, 1, ADA_COLS), lambda l, j: (l, 0, j)),
        ],
        out_specs=pl.BlockSpec((None, ADA_ROWS, ADA_COLS), lambda l, j: (l, 0, j)),
        out_shape=jax.ShapeDtypeStruct((depth, ADA_ROWS, n_out), F32),
        compiler_params=pltpu.CompilerParams(
            dimension_semantics=("arbitrary", "arbitrary"), vmem_limit_bytes=VMEM_LIMIT),
        name="ada",
    )(cc, w_ada, b_ada.reshape(depth, 1, n_out))


def _stream_specs(n_sub, block_cols, lat_tile_off, has_ctx):
    specs = []
    for i in range(n_sub):
        if has_ctx:
            idx = lambda b, t, i=i: (b, jnp.maximum(n_sub * t + i - 1, 0), 0)
        else:
            idx = lambda b, t, i=i: (b, lat_tile_off + n_sub * t + i, 0)
        specs.append(pl.BlockSpec((None, TOK_TILE, block_cols), idx))
    if has_ctx:
        specs.append(pl.BlockSpec((None, TOK_TILE, block_cols), lambda b, t: (b, 0, 0)))
    return specs


def _sub_tile(refs, ctx_ref, i):
    v = refs[i][...]
    if ctx_ref is not None and i == 0:
        v = jnp.where(pl.program_id(1) == 0, ctx_ref[...], v)
    return v


def _sub_mod(mod_ref, ctx_first, i):
    if ctx_first and i == 0:
        return jnp.where(pl.program_id(1) == 0, mod_ref[0], mod_ref[1])
    return mod_ref[1]


def _proj_kernel(*refs, n_sub, has_ctx):
    h_refs = refs[:n_sub]
    refs = refs[n_sub:]
    ctx_ref = None
    if has_ctx:
        ctx_ref, refs = refs[0], refs[1:]
    (mod_ref, gpre_ref, w_ref, gq_ref, gk_ref, eq_ref, ek_ref, cos_ref, sa_ref, sb_ref,
     qa_ref, kda_ref, vta_ref, qb_ref, kdb_ref, vtb_ref) = refs
    kv = 2 * KV_COLS

    def prenorm(i):
        mod = _sub_mod(mod_ref, True, i)
        x = _sub_tile(h_refs, ctx_ref, i)
        return (_rms(x, gpre_ref[...]) * (1 + mod[1:2, :]) + mod[0:1, :]).astype(BF16)

    def stages(i, u):
        rows = slice(i * TOK_TILE, (i + 1) * TOK_TILE)
        cos = cos_ref[rows, :]
        sa = sa_ref[rows, :]
        sb = sb_ref[rows, :]

        def rope(t):
            up = pltpu.roll(t, LANES - ROPE_SHIFT, 1)
            dn = pltpu.roll(t, ROPE_SHIFT, 1)
            return t * cos + up * sa + dn * sb

        def head_norm(t, e_ref, g_ref):
            ss = jnp.dot((t * t).astype(BF16), e_ref[...], preferred_element_type=F32)
            return t * lax.rsqrt(ss * (1.0 / HEAD_DIM) + EPS) * g_ref[...]

        def store_q(q, qt_ref):
            q = jnp.concatenate([rope(q[:, j * LANES:(j + 1) * LANES]) for j in range(Q_COLS // LANES)],
                                axis=1) * Q_SCALE
            q_t = q.T.astype(BF16)
            for h in range(N_HEADS):
                qt_ref[h, :, rows] = q_t[h * HEAD_DIM:(h + 1) * HEAD_DIM, :]

        def store_k(k, k_ref):
            k_ref[rows, :] = rope(k).astype(BF16)

        def store_vt(v, vt_ref):
            v_t = v.T.astype(BF16)
            ones = jnp.ones((VT_ROWS - HEAD_DIM, TOK_TILE), BF16)
            for h in range(N_KV):
                vt_ref[h, 0:HEAD_DIM, rows] = v_t[h * HEAD_DIM:(h + 1) * HEAD_DIM, :]
                vt_ref[h, HEAD_DIM:VT_ROWS, rows] = ones

        def project(c0, n):
            return lambda: jnp.dot(u, w_ref[:, c0:c0 + n], preferred_element_type=F32)

        def epi_qa(z):
            store_q(head_norm(z, eq_ref, gq_ref), qa_ref)

        def epi_kva(z):
            store_k(head_norm(z[:, 0:KV_COLS], ek_ref, gk_ref), kda_ref)
            store_vt(z[:, KV_COLS:kv], vta_ref)

        def epi_qb(z):
            store_q(z, qb_ref)

        def epi_kvb(z):
            store_k(z[:, 0:KV_COLS], kdb_ref)
            store_vt(z[:, KV_COLS:kv], vtb_ref)

        return [(project(0, Q_COLS), epi_qa), (project(Q_COLS, kv), epi_kva),
                (project(Q_COLS + kv, Q_COLS), epi_qb), (project(2 * Q_COLS + kv, kv), epi_kvb)]

    u_next = prenorm(0)
    pending = []
    for i in range(n_sub):
        u = u_next
        results = []
        for g, (matmul, epilogue) in enumerate(stages(i, u)):
            results.append((epilogue, matmul()))
            if g == 0 and i + 1 < n_sub:
                u_next = prenorm(i + 1)
            if pending:
                epi, z = pending.pop(0)
                epi(z)
        for epi, z in pending:
            epi(z)
        pending = results
    for epi, z in pending:
        epi(z)


def _proj(h_lat, h_ctx, lat_tile_off, n_tiles, mods, g_pre, w_in, gq, gk, eq, ek, cos_t, sa_t, sb_t):
    bsz = h_lat.shape[0]
    has_ctx = h_ctx is not None
    n_sub = PROJ_SUB
    assert n_tiles % n_sub == 0
    n_tok = n_tiles * TOK_TILE
    rows = n_sub * TOK_TILE
    const = lambda b, t: (0, 0)
    tok = lambda b, t: (b, t, 0)
    q_shape = jax.ShapeDtypeStruct((bsz, N_HEADS, HEAD_DIM, n_tok), BF16)
    kd_shape = jax.ShapeDtypeStruct((bsz, n_tok, KV_COLS), BF16)
    vt_shape = jax.ShapeDtypeStruct((bsz, N_KV, VT_ROWS, n_tok), BF16)
    q_spec = pl.BlockSpec((None, N_HEADS, HEAD_DIM, rows), lambda b, t: (b, 0, 0, t))
    kd_spec = pl.BlockSpec((None, rows, KV_COLS), tok)
    vt_spec = pl.BlockSpec((None, N_KV, VT_ROWS, rows), lambda b, t: (b, 0, 0, t))
    rope_spec = pl.BlockSpec((rows, LANES), lambda b, t: (t, 0))
    h_args = [h_lat] * n_sub + ([h_ctx] if has_ctx else [])
    return pl.pallas_call(
        functools.partial(_proj_kernel, n_sub=n_sub, has_ctx=has_ctx),
        grid=(bsz, n_tiles // n_sub),
        in_specs=_stream_specs(n_sub, D_MODEL, lat_tile_off, has_ctx) + [
            pl.BlockSpec((None, 2, N_MOD, D_MODEL), lambda b, t: (b, 0, 0, 0)),
            pl.BlockSpec((1, D_MODEL), const),
            pl.BlockSpec((D_MODEL, IN_COLS), const),
            pl.BlockSpec((1, Q_COLS), const),
            pl.BlockSpec((1, KV_COLS), const),
            pl.BlockSpec((Q_COLS, Q_COLS), const),
            pl.BlockSpec((KV_COLS, KV_COLS), const),
            rope_spec, rope_spec, rope_spec,
        ],
        out_specs=[q_spec, kd_spec, vt_spec, q_spec, kd_spec, vt_spec],
        out_shape=[q_shape, kd_shape, vt_shape, q_shape, kd_shape, vt_shape],
        compiler_params=pltpu.CompilerParams(
            dimension_semantics=("parallel", "parallel"), vmem_limit_bytes=VMEM_LIMIT),
        name="proj",
    )(*h_args, mods, g_pre, w_in, gq, gk, eq, ek, cos_t, sa_t, sb_t)


def _band_start(blk, n_ctx, n_tok):
    return jnp.minimum(n_ctx + (blk - 1) * Q_BLOCK, n_tok - (Q_BLOCK + 2 * WINDOW))


def _window_bias_variants(n_ctx, n_tok, n_lat_blk):
    n_band = Q_BLOCK + 2 * WINDOW
    kj = jnp.arange(n_band, dtype=jnp.int32)[:, None]
    qr = jnp.arange(GROUP * Q_BLOCK, dtype=jnp.int32)[None, :] % Q_BLOCK
    variants = []
    for blk in (0, 1, n_lat_blk - 1):
        kpos = kj + (_band_start(blk, n_ctx, n_tok) - n_ctx)
        qpos = qr + blk * Q_BLOCK
        valid = (jnp.abs(kpos - qpos) <= WINDOW) & (kpos >= 0)
        variants.append(jnp.where(valid, 0.0, NEG_BIG).astype(F32))
    return jnp.stack(variants)


def _attn_kernel(*refs, n_ctx, n_full, band, use_sink, n_tok, n_lat_blk, q_row0, q_per_step,
                 given_bound):
    refs = list(refs)
    qt_ref, k_ref, vt_ref, eh_ref = refs[:4]
    rest = refs[4:]
    sink_ref = rest.pop(0) if use_sink else None
    bias_ref = rest.pop(0) if band else None
    if given_bound:
        bound_ref, o_ref = rest
    else:
        o_ref, bound_ref = rest
    nq = GROUP * Q_BLOCK
    n_band = Q_BLOCK + 2 * WINDOW if band else 0
    chunks = [(c0, min(c0 + KEY_CHUNK, n_full), False) for c0 in range(0, n_full, KEY_CHUNK)]
    if band:
        chunks.append((n_full, n_full + n_band, True))
    kv_head = pl.program_id(1)
    step = pl.program_id(2)

    if not given_bound:
        @pl.when(step == 0)
        def _():
            q = qt_ref[...].astype(F32)
            qq = jnp.sum(q * q, axis=1)
            k = k_ref[...]
            kk = jnp.dot(k * k, eh_ref[...], preferred_element_type=F32)
            bound_ref[0] = jnp.sqrt(jnp.max(qq) * jnp.max(kk)) * BOUND_SLACK

    def band_start(blk):
        return pl.multiple_of(_band_start(blk, n_ctx, n_tok), LANES)

    def extend_q(blk):
        cols = pl.ds(pl.multiple_of(q_row0 + blk * Q_BLOCK, Q_BLOCK), Q_BLOCK)
        top = jnp.concatenate([qt_ref[g, :, cols] for g in range(GROUP)], axis=1)
        zero = jnp.zeros_like(top)
        return jnp.where(kv_head == 0, jnp.concatenate([top, zero], axis=0),
                         jnp.concatenate([zero, top], axis=0))

    def logits(qm, blk, chunk):
        c0, c1, in_band = chunk
        if in_band:
            variant = jnp.where(blk == 0, 0, jnp.where(blk >= n_lat_blk - 1, 2, 1))
            return jnp.dot(k_ref[pl.ds(band_start(blk), n_band), :], qm,
                           preferred_element_type=F32) + bias_ref[variant]
        return jnp.dot(k_ref[c0:c1, :], qm, preferred_element_type=F32)

    def values_t(blk, chunk):
        c0, c1, in_band = chunk
        if in_band:
            return vt_ref[:, pl.ds(band_start(blk), n_band)]
        return vt_ref[:, c0:c1]

    def column_max(qm, blk):
        mx = None
        for chunk in chunks:
            s = logits(qm, blk, chunk)
            cm = jnp.max(s.reshape(s.shape[0] // 8, 8, nq), axis=0)
            mx = cm if mx is None else jnp.maximum(mx, cm)
        return jnp.max(mx, axis=0, keepdims=True)

    def finalize(t, acc, m):
        l = acc[HEAD_DIM:HEAD_DIM + 1, :]
        if use_sink:
            l = l + jnp.exp2(sink_ref[...] - m)
        o_t = acc[0:HEAD_DIM, :] / l
        o_t = jnp.concatenate([o_t[:, g * Q_BLOCK:(g + 1) * Q_BLOCK] for g in range(GROUP)], axis=0)
        o_ref[t * Q_BLOCK:(t + 1) * Q_BLOCK, :] = o_t.T.astype(BF16)

    def attend_all(qms, shifts):
        items = [(t, c) for t in range(q_per_step) for c in range(len(chunks))]
        blk_of = lambda t: step * q_per_step + t
        s_next = logits(qms[0], blk_of(0), chunks[0])
        acc = [None] * q_per_step
        pv_prev = None

        def retire(prev):
            t, c, pv = prev
            acc[t] = pv if acc[t] is None else acc[t] + pv
            if c == len(chunks) - 1:
                finalize(t, acc[t], shifts[t])

        for n, (t, c) in enumerate(items):
            s = s_next
            if n + 1 < len(items):
                t2, c2 = items[n + 1]
                s_next = logits(qms[t2], blk_of(t2), chunks[c2])
            p = jnp.exp2(s - shifts[t]).astype(BF16)
            pv = jnp.dot(values_t(blk_of(t), chunks[c]), p, preferred_element_type=F32)
            if pv_prev is not None:
                retire(pv_prev)
            pv_prev = (t, c, pv)
        retire(pv_prev)

    def with_sink(m):
        return jnp.maximum(m, sink_ref[...]) if use_sink else m

    bound = bound_ref[0]
    one_pass = bound <= LOGIT_BOUND_LIMIT

    @pl.when(one_pass)
    def _():
        m = with_sink(jnp.full((1, nq), bound, F32))
        qms = [extend_q(step * q_per_step + t) for t in range(q_per_step)]
        attend_all(qms, [m] * q_per_step)

    @pl.when(jnp.logical_not(one_pass))
    def _():
        qms = [extend_q(step * q_per_step + t) for t in range(q_per_step)]
        shifts = [with_sink(column_max(qms[t], step * q_per_step + t)) for t in range(q_per_step)]
        attend_all(qms, shifts)


def _attention(q, kd, vt, sink_rows, head_ones, logit_bound, *, q_blk_off, n_qblk, n_ctx, n_full, band,
               q_per_step, name):
    bsz, n_tok, _ = kd.shape
    q_per_step = min(q_per_step, n_qblk)
    n_rows = n_tok if (band or n_full == n_tok) else max(n_full, (q_blk_off + n_qblk) * Q_BLOCK)
    assert n_qblk % q_per_step == 0
    use_sink = sink_rows is not None
    nq = GROUP * Q_BLOCK
    in_specs = [
        pl.BlockSpec((None, GROUP, HEAD_DIM, n_rows), lambda b, h, j: (b, h, 0, 0)),
        pl.BlockSpec((None, n_rows, KV_COLS), lambda b, h, j: (b, 0, 0)),
        pl.BlockSpec((None, None, VT_ROWS, n_rows), lambda b, h, j: (b, h, 0, 0)),
        pl.BlockSpec(head_ones.shape, lambda b, h, j: (0, 0)),
    ]
    args = [q, kd, vt, head_ones]
    if use_sink:
        in_specs.append(pl.BlockSpec((None, 1, nq), lambda b, h, j: (h, 0, 0)))
        args.append(sink_rows)
    if band:
        bias = _window_bias_variants(n_ctx, n_tok, n_qblk)
        in_specs.append(pl.BlockSpec(bias.shape, lambda b, h, j: (0, 0, 0)))
        args.append(bias)
    given_bound = logit_bound is not None
    if given_bound:
        in_specs.append(pl.BlockSpec(memory_space=pltpu.SMEM))
        args.append(logit_bound)
    kern = functools.partial(_attn_kernel, n_ctx=n_ctx, n_full=n_full, band=band, use_sink=use_sink,
                             n_tok=n_tok, n_lat_blk=n_qblk, q_row0=q_blk_off * Q_BLOCK,
                             q_per_step=q_per_step, given_bound=given_bound)
    rows = q_per_step * Q_BLOCK
    return pl.pallas_call(
        kern,
        grid=(bsz, N_KV, n_qblk // q_per_step),
        in_specs=in_specs,
        out_specs=pl.BlockSpec((None, rows, GROUP * HEAD_DIM), lambda b, h, j: (b, j, h)),
        out_shape=jax.ShapeDtypeStruct((bsz, n_qblk * Q_BLOCK, Q_COLS), BF16),
        scratch_shapes=[] if given_bound else [pltpu.SMEM((1,), F32)],
        compiler_params=pltpu.CompilerParams(
            dimension_semantics=("parallel", "parallel", "arbitrary"), vmem_limit_bytes=VMEM_LIMIT),
        name=name,
    )(*args)


def _post_kernel(*refs, n_sub, has_ctx):
    n_in = n_sub + (1 if has_ctx else 0)
    h_refs, ma_refs, mb_refs = refs[:n_sub], refs[n_in:n_in + n_sub], refs[2 * n_in:2 * n_in + n_sub]
    ctx_h = refs[n_sub] if has_ctx else None
    ctx_ma = refs[n_in + n_sub] if has_ctx else None
    ctx_mb = refs[2 * n_in + n_sub] if has_ctx else None
    mod_ref, wout_ref, gpm_ref, gpre_ref, wup_ref, wdn_ref, gpl_ref, o_ref = refs[3 * n_in:]
    mods = [_sub_mod(mod_ref, has_ctx, i) for i in range(n_sub)]

    def out_proj(i):
        mix = jnp.concatenate([_sub_tile(ma_refs, ctx_ma, i), _sub_tile(mb_refs, ctx_mb, i)], axis=1)
        return jnp.dot(mix, wout_ref[...], preferred_element_type=F32)

    def mixer_residual(i, mix):
        mod = mods[i]
        h = _sub_tile(h_refs, ctx_h, i) + mod[2:3, :] * _rms(mix, gpm_ref[...])
        u = (_rms(h, gpre_ref[...]) * (1 + mod[4:5, :]) + mod[3:4, :]).astype(BF16)
        return h, u

    def mlp_chunk(u, y, c0):
        a = jnp.dot(u, wup_ref[:, c0:c0 + FF_CHUNK], preferred_element_type=F32)
        a = jnp.square(jnp.maximum(a, 0.0)).astype(BF16)
        part = jnp.dot(a, wdn_ref[c0:c0 + FF_CHUNK, :], preferred_element_type=F32)
        return part if y is None else y + part

    def mlp_residual(i, h, y):
        o_ref[i * TOK_TILE:(i + 1) * TOK_TILE, :] = h + mods[i][5:6, :] * _rms(y, gpl_ref[...])

    hu = [None] * n_sub
    mix_prev = None
    for i in range(n_sub):
        mix = out_proj(i)
        if i > 0:
            hu[i - 1] = mixer_residual(i - 1, mix_prev)
        mix_prev = mix
    for i in range(n_sub):
        y = None
        for c, c0 in enumerate(range(0, D_FF, FF_CHUNK)):
            if i == 0 and c == 0:
                hu[n_sub - 1] = mixer_residual(n_sub - 1, mix_prev)
            y = mlp_chunk(hu[i][1], y, c0)
            if i > 0 and c == 0:
                mlp_residual(i - 1, hu[i - 1][0], y_prev)
        y_prev = y
    mlp_residual(n_sub - 1, hu[n_sub - 1][0], y_prev)


def _post(h_lat, h_ctx, lat_tile_off, n_tiles, mix_a, mix_b, ctx_mix_a, ctx_mix_b, mods, w_out,
          g_post_mix, g_pre_mlp, w_up, w_down, g_post_mlp):
    bsz = h_lat.shape[0]
    has_ctx = h_ctx is not None
    n_sub = max(s for s in POST_SUBS if n_tiles % s == 0)
    const = lambda b, t: (0, 0)
    resident = functools.partial(pl.BlockSpec, index_map=const, pipeline_mode=pl.Buffered(1))
    ctx_args = lambda v: [v] if has_ctx else []
    args = ([h_lat] * n_sub + ctx_args(h_ctx) + [mix_a] * n_sub + ctx_args(ctx_mix_a)
            + [mix_b] * n_sub + ctx_args(ctx_mix_b))
    return pl.pallas_call(
        functools.partial(_post_kernel, n_sub=n_sub, has_ctx=has_ctx),
        grid=(bsz, n_tiles // n_sub),
        in_specs=(_stream_specs(n_sub, D_MODEL, lat_tile_off, has_ctx)
                  + _stream_specs(n_sub, Q_COLS, 0, has_ctx)
                  + _stream_specs(n_sub, Q_COLS, 0, has_ctx) + [
            pl.BlockSpec((None, 2, N_MOD, D_MODEL), lambda b, t: (b, 0, 0, 0)),
            resident((D_MODEL, D_MODEL)),
            pl.BlockSpec((1, D_MODEL), const),
            pl.BlockSpec((1, D_MODEL), const),
            resident((D_MODEL, D_FF)),
            resident((D_FF, D_MODEL)),
            pl.BlockSpec((1, D_MODEL), const),
        ]),
        out_specs=pl.BlockSpec((None, n_sub * TOK_TILE, D_MODEL), lambda b, t: (b, t, 0)),
        out_shape=jax.ShapeDtypeStruct((bsz, n_tiles * TOK_TILE, D_MODEL), F32),
        compiler_params=pltpu.CompilerParams(
            dimension_semantics=("parallel", "parallel"), vmem_limit_bytes=VMEM_LIMIT),
        name="post",
    )(*args, mods, w_out, g_post_mix, g_pre_mlp, w_up, w_down, g_post_mlp)


def _rope_tables(n_ctx, n_lat):
    rows = n_lat // GRID_W
    row_ids = jnp.repeat(jnp.arange(rows, dtype=jnp.int32), GRID_W).astype(F32)
    col_ids = jnp.tile(jnp.arange(GRID_W, dtype=jnp.int32), rows).astype(F32)
    axis_dim = HEAD_DIM // 2
    inv = ROPE_THETA ** (-jnp.arange(0, axis_dim, 2, dtype=F32) / axis_dim)
    ang_r = row_ids[:, None] * inv[None, :]
    ang_c = col_ids[:, None] * inv[None, :]
    cos_r, sin_r, cos_c, sin_c = jnp.cos(ang_r), jnp.sin(ang_r), jnp.cos(ang_c), jnp.sin(ang_c)
    zero = jnp.zeros_like(sin_r)
    cos_h = jnp.concatenate([cos_r, cos_r, cos_c, cos_c], axis=-1)
    sa_h = jnp.concatenate([-sin_r, zero, -sin_c, zero], axis=-1)
    sb_h = jnp.concatenate([zero, sin_r, zero, sin_c], axis=-1)

    def full(t, ctx_value):
        t = jnp.tile(t, (1, LANES // HEAD_DIM))
        return jnp.concatenate([jnp.full((n_ctx, LANES), ctx_value, F32), t], axis=0)

    return full(cos_h, 1.0), full(sa_h, 0.0), full(sb_h, 0.0)


def _head_ones(n_cols):
    head = jnp.arange(n_cols, dtype=jnp.int32) // HEAD_DIM
    return (head[:, None] == head[None, :]).astype(BF16)


def kernel(x, c, ctx, c_ctx, w_ada, b_ada, g_pre_mix, g_post_mix, g_pre_mlp, g_post_mlp,
           w_in, q_norm, k_norm, sink, w_out, w_up, w_down):
    bsz, n_lat, d = x.shape
    n_ctx = ctx.shape[1]
    depth = w_in.shape[0]
    n_tok = n_ctx + n_lat
    assert d == D_MODEL and n_ctx == TOK_TILE and n_lat % TOK_TILE == 0 and bsz + 1 <= ADA_ROWS
    assert w_in.shape[2] == IN_COLS and w_up.shape[2] == D_FF

    cc = jnp.concatenate([c, c_ctx[None, :], jnp.zeros((ADA_ROWS - bsz - 1, d), F32)], axis=0)
    mod_all = _ada(cc, w_ada, b_ada)
    mod_all = mod_all.reshape(depth, ADA_ROWS, N_MOD, d)
    mod_ctx = jnp.broadcast_to(mod_all[:, bsz:bsz + 1], (depth, bsz, N_MOD, d))
    mods = jnp.stack([mod_ctx, mod_all[:, :bsz]], axis=2)

    cos_t, sa_t, sb_t = _rope_tables(n_ctx, n_lat)
    eq = _head_ones(Q_COLS)
    ek = _head_ones(KV_COLS)
    eh = _head_ones(KV_COLS)
    row = lambda v: v.reshape(1, -1)

    w_in_b = w_in.astype(BF16)
    w_out_b = w_out.astype(BF16)
    w_up_b = w_up.astype(BF16)
    w_down_b = w_down.astype(BF16)

    ctx_tiles = n_ctx // TOK_TILE
    assert depth >= 1 and ctx_tiles == 1
    ctx_qblk = n_ctx // Q_BLOCK
    lat_qblk = n_lat // Q_BLOCK

    h_lat, h_ctx = x, ctx
    lat_off = 0
    for l in range(depth):
        last = l == depth - 1
        gq = row(jnp.tile(q_norm[l], N_HEADS))
        gk = row(jnp.tile(k_norm[l], N_KV))
        sink_rows = jnp.repeat(sink[l].reshape(N_KV, 1, GROUP) * LOG2_E, Q_BLOCK, axis=2)

        qa, kda, vta, qb, kdb, vtb = _proj(h_lat, h_ctx, 0, n_tok // TOK_TILE, mods[l],
                                           row(g_pre_mix[l]), w_in_b[l], gq, gk, eq, ek,
                                           cos_t, sa_t, sb_t)
        bound_a = (HEAD_DIM * Q_SCALE * BOUND_SLACK * jnp.max(jnp.abs(q_norm[l]))
                   * jnp.max(jnp.abs(k_norm[l]))).reshape(1)
        oa = _attention(qa, kda, vta, None, eh, bound_a, q_blk_off=ctx_qblk, n_qblk=lat_qblk,
                        n_ctx=n_ctx, n_full=n_tok, band=False, q_per_step=Q_PER_STEP_GLOBAL,
                        name="attn_global")
        ob = _attention(qb, kdb, vtb, sink_rows, eh, None, q_blk_off=ctx_qblk, n_qblk=lat_qblk,
                        n_ctx=n_ctx, n_full=n_ctx, band=True, q_per_step=Q_PER_STEP_WINDOW,
                        name="attn_window")
        post = functools.partial(_post, mods=mods[l], w_out=w_out_b[l], g_post_mix=row(g_post_mix[l]),
                                 g_pre_mlp=row(g_pre_mlp[l]), w_up=w_up_b[l], w_down=w_down_b[l],
                                 g_post_mlp=row(g_post_mlp[l]))
        if not last:
            oac = _attention(qa, kda, vta, None, eh, bound_a, q_blk_off=0, n_qblk=ctx_qblk, n_ctx=n_ctx,
                             n_full=n_ctx, band=False, q_per_step=ctx_qblk, name="attn_ctx_global")
            obc = _attention(qb, kdb, vtb, sink_rows, eh, None, q_blk_off=0, n_qblk=ctx_qblk, n_ctx=n_ctx,
                             n_full=n_ctx, band=False, q_per_step=ctx_qblk, name="attn_ctx_sink")
            assert h_ctx is not None
            h_lat = post(h_lat, h_ctx, lat_off, n_tok // TOK_TILE, oa, ob, oac, obc)
            h_ctx, lat_off = None, ctx_tiles
        else:
            if h_ctx is not None:
                h_lat = post(h_lat, None, 0, n_lat // TOK_TILE, oa, ob, None, None)
            else:
                h_lat = post(h_lat, None, lat_off, n_lat // TOK_TILE, oa, ob, None, None)
    return h_lat
```

```python
import functools

import jax
import jax.numpy as jnp
from jax import lax
from jax.experimental import pallas as pl
from jax.experimental.pallas import tpu as pltpu

F32 = jnp.float32
BF16 = jnp.bfloat16

D_MODEL = 1024
HEAD_DIM = 64
N_HEADS = 8
N_KV = 2
GROUP = N_HEADS // N_KV
Q_COLS = N_HEADS * HEAD_DIM
KV_COLS = N_KV * HEAD_DIM
IN_COLS = 2 * (Q_COLS + 2 * KV_COLS)
D_FF = 4 * D_MODEL
GRID_W = 64
WINDOW = 128
Q_BLOCK = 128
ROPE_THETA = 10000.0
EPS = 1e-6
NEG_BIG = -1e30
N_MOD = 6
ROPE_SHIFT = HEAD_DIM // 4
LOG2_E = 1.4426950408889634
Q_SCALE = HEAD_DIM ** -0.5 * LOG2_E

LANES = 128
TOK_TILE = 256
PROJ_SUB = 9
POST_SUBS = (1, 2, 3, 4)
VT_ROWS = HEAD_DIM + 16
KEY_CHUNK = 2304
Q_PER_STEP_GLOBAL = 8
Q_PER_STEP_WINDOW = 16
LOGIT_BOUND_LIMIT = 48.0
BOUND_SLACK = 1.01
FF_CHUNK = 1024
ADA_ROWS = 32
ADA_COLS = 1536
VMEM_LIMIT = 56 * 1024 * 1024


def _rms(x, g):
    return x * lax.rsqrt(jnp.mean(x * x, axis=-1, keepdims=True) + EPS) * g


def _ada_kernel(c_ref, w_ref, b_ref, o_ref):
    c = c_ref[...]
    a = c * jax.nn.sigmoid(c)
    w = w_ref[...]
    a_hi = a.astype(BF16)
    a_lo = (a - a_hi.astype(F32)).astype(BF16)
    w_hi = w.astype(BF16)
    w_lo = (w - w_hi.astype(F32)).astype(BF16)
    acc = jnp.dot(a_hi, w_hi, preferred_element_type=F32)
    acc += jnp.dot(a_lo, w_hi, preferred_element_type=F32)
    acc += jnp.dot(a_hi, w_lo, preferred_element_type=F32)
    o_ref[...] = acc + b_ref[...]


def _ada(cc, w_ada, b_ada):
    depth = w_ada.shape[0]
    n_out = w_ada.shape[2]
    return pl.pallas_call(
        _ada_kernel,
        grid=(depth, n_out // ADA_COLS),
        in_specs=[
            pl.BlockSpec((ADA_ROWS, D_MODEL), lambda l, j: (0, 0)),
            pl.BlockSpec((None, D_MODEL, ADA_COLS), lambda l, j: (l, 0, j)),
            pl.BlockSpec((None, 1, ADA_COLS), lambda l, j: (l, 0, j)),
        ],
        out_specs=pl.BlockSpec((None, ADA_ROWS, ADA_COLS), lambda l, j: (l, 0, j)),
        out_shape=jax.ShapeDtypeStruct((depth, ADA_ROWS, n_out), F32),
        compiler_params=pltpu.CompilerParams(
            dimension_semantics=("arbitrary", "arbitrary"), vmem_limit_bytes=VMEM_LIMIT),
        name="ada",
    )(cc, w_ada, b_ada.reshape(depth, 1, n_out))


def _stream_specs(n_sub, block_cols, lat_tile_off, has_ctx):
    specs = []
    for i in range(n_sub):
        if has_ctx:
            idx = lambda b, t, i=i: (b, jnp.maximum(n_sub * t + i - 1, 0), 0)
        else:
            idx = lambda b, t, i=i: (b, lat_tile_off + n_sub * t + i, 0)
        specs.append(pl.BlockSpec((None, TOK_TILE, block_cols), idx))
    if has_ctx:
        specs.append(pl.BlockSpec((None, TOK_TILE, block_cols), lambda b, t: (b, 0, 0)))
    return specs


def _sub_tile(refs, ctx_ref, i):
    v = refs[i][...]
    if ctx_ref is not None and i == 0:
        v = jnp.where(pl.program_id(1) == 0, ctx_ref[...], v)
    return v


def _sub_mod(mod_ref, ctx_first, i):
    if ctx_first and i == 0:
        return jnp.where(pl.program_id(1) == 0, mod_ref[0], mod_ref[1])
    return mod_ref[1]


def _proj_kernel(*refs, n_sub, has_ctx):
    h_refs = refs[:n_sub]
    refs = refs[n_sub:]
    ctx_ref = None
    if has_ctx:
        ctx_ref, refs = refs[0], refs[1:]
    (mod_ref, gpre_ref, w_ref, gq_ref, gk_ref, eq_ref, ek_ref, cos_ref, sa_ref, sb_ref,
     qa_ref, kda_ref, vta_ref, qb_ref, kdb_ref, vtb_ref) = refs
    kv = 2 * KV_COLS

    def prenorm(i):
        mod = _sub_mod(mod_ref, True, i)
        x = _sub_tile(h_refs, ctx_ref, i)
        return (_rms(x, gpre_ref[...]) * (1 + mod[1:2, :]) + mod[0:1, :]).astype(BF16)

    def stages(i, u):
        rows = slice(i * TOK_TILE, (i + 1) * TOK_TILE)
        cos = cos_ref[rows, :]
        sa = sa_ref[rows, :]
        sb = sb_ref[rows, :]

        def rope(t):
            up = pltpu.roll(t, LANES - ROPE_SHIFT, 1)
            dn = pltpu.roll(t, ROPE_SHIFT, 1)
            return t * cos + up * sa + dn * sb

        def head_norm(t, e_ref, g_ref):
            ss = jnp.dot((t * t).astype(BF16), e_ref[...], preferred_element_type=F32)
            return t * lax.rsqrt(ss * (1.0 / HEAD_DIM) + EPS) * g_ref[...]

        def store_q(q, qt_ref):
            q = jnp.concatenate([rope(q[:, j * LANES:(j + 1) * LANES]) for j in range(Q_COLS // LANES)],
                                axis=1) * Q_SCALE
            q_t = q.T.astype(BF16)
            for h in range(N_HEADS):
                qt_ref[h, :, rows] = q_t[h * HEAD_DIM:(h + 1) * HEAD_DIM, :]

        def store_k(k, k_ref):
            k_ref[rows, :] = rope(k).astype(BF16)

        def store_vt(v, vt_ref):
            v_t = v.T.astype(BF16)
            ones = jnp.ones((VT_ROWS - HEAD_DIM, TOK_TILE), BF16)
            for h in range(N_KV):
                vt_ref[h, 0:HEAD_DIM, rows] = v_t[h * HEAD_DIM:(h + 1) * HEAD_DIM, :]
                vt_ref[h, HEAD_DIM:VT_ROWS, rows] = ones

        def project(c0, n):
            return lambda: jnp.dot(u, w_ref[:, c0:c0 + n], preferred_element_type=F32)

        def epi_qa(z):
            store_q(head_norm(z, eq_ref, gq_ref), qa_ref)

        def epi_kva(z):
            store_k(head_norm(z[:, 0:KV_COLS], ek_ref, gk_ref), kda_ref)
            store_vt(z[:, KV_COLS:kv], vta_ref)

        def epi_qb(z):
            store_q(z, qb_ref)

        def epi_kvb(z):
            store_k(z[:, 0:KV_COLS], kdb_ref)
            store_vt(z[:, KV_COLS:kv], vtb_ref)

        return [(project(0, Q_COLS), epi_qa), (project(Q_COLS, kv), epi_kva),
                (project(Q_COLS + kv, Q_COLS), epi_qb), (project(2 * Q_COLS + kv, kv), epi_kvb)]

    u_next = prenorm(0)
    pending = []
    for i in range(n_sub):
        u = u_next
        results = []
        for g, (matmul, epilogue) in enumerate(stages(i, u)):
            results.append((epilogue, matmul()))
            if g == 0 and i + 1 < n_sub:
                u_next = prenorm(i + 1)
            if pending:
                epi, z = pending.pop(0)
                epi(z)
        for epi, z in pending:
            epi(z)
        pending = results
    for epi, z in pending:
        epi(z)


def _proj(h_lat, h_ctx, lat_tile_off, n_tiles, mods, g_pre, w_in, layer, gq, gk, eq, ek, cos_t, sa_t,
          sb_t):
    bsz = h_lat.shape[0]
    has_ctx = h_ctx is not None
    n_sub = PROJ_SUB
    assert n_tiles % n_sub == 0
    n_tok = n_tiles * TOK_TILE
    rows = n_sub * TOK_TILE
    const = lambda b, t: (0, 0)
    tok = lambda b, t: (b, t, 0)
    q_shape = jax.ShapeDtypeStruct((bsz, N_HEADS, HEAD_DIM, n_tok), BF16)
    kd_shape = jax.ShapeDtypeStruct((bsz, n_tok, KV_COLS), BF16)
    vt_shape = jax.ShapeDtypeStruct((bsz, N_KV, VT_ROWS, n_tok), BF16)
    q_spec = pl.BlockSpec((None, N_HEADS, HEAD_DIM, rows), lambda b, t: (b, 0, 0, t))
    kd_spec = pl.BlockSpec((None, rows, KV_COLS), tok)
    vt_spec = pl.BlockSpec((None, N_KV, VT_ROWS, rows), lambda b, t: (b, 0, 0, t))
    rope_spec = pl.BlockSpec((rows, LANES), lambda b, t: (t, 0))
    h_args = [h_lat] * n_sub + ([h_ctx] if has_ctx else [])
    return pl.pallas_call(
        functools.partial(_proj_kernel, n_sub=n_sub, has_ctx=has_ctx),
        grid=(bsz, n_tiles // n_sub),
        in_specs=_stream_specs(n_sub, D_MODEL, lat_tile_off, has_ctx) + [
            pl.BlockSpec((None, 2, N_MOD, D_MODEL), lambda b, t: (b, 0, 0, 0)),
            pl.BlockSpec((1, D_MODEL), const),
            pl.BlockSpec((None, D_MODEL, IN_COLS), lambda b, t: (layer, 0, 0)),
            pl.BlockSpec((1, Q_COLS), const),
            pl.BlockSpec((1, KV_COLS), const),
            pl.BlockSpec((Q_COLS, Q_COLS), const),
            pl.BlockSpec((KV_COLS, KV_COLS), const),
            rope_spec, rope_spec, rope_spec,
        ],
        out_specs=[q_spec, kd_spec, vt_spec, q_spec, kd_spec, vt_spec],
        out_shape=[q_shape, kd_shape, vt_shape, q_shape, kd_shape, vt_shape],
        compiler_params=pltpu.CompilerParams(
            dimension_semantics=("parallel", "parallel"), vmem_limit_bytes=VMEM_LIMIT),
        name="proj",
    )(*h_args, mods, g_pre, w_in, gq, gk, eq, ek, cos_t, sa_t, sb_t)


def _band_start(blk, n_ctx, n_tok):
    return jnp.minimum(n_ctx + (blk - 1) * Q_BLOCK, n_tok - (Q_BLOCK + 2 * WINDOW))


def _window_bias_variants(n_ctx, n_tok, n_lat_blk):
    n_band = Q_BLOCK + 2 * WINDOW
    kj = jnp.arange(n_band, dtype=jnp.int32)[:, None]
    qr = jnp.arange(GROUP * Q_BLOCK, dtype=jnp.int32)[None, :] % Q_BLOCK
    variants = []
    for blk in (0, 1, n_lat_blk - 1):
        kpos = kj + (_band_start(blk, n_ctx, n_tok) - n_ctx)
        qpos = qr + blk * Q_BLOCK
        valid = (jnp.abs(kpos - qpos) <= WINDOW) & (kpos >= 0)
        variants.append(jnp.where(valid, 0.0, NEG_BIG).astype(F32))
    return jnp.stack(variants)


def _attn_kernel(*refs, n_ctx, n_full, band, use_sink, n_tok, n_lat_blk, q_row0, q_per_step,
                 given_bound):
    refs = list(refs)
    qt_ref, k_ref, vt_ref, eh_ref = refs[:4]
    rest = refs[4:]
    sink_ref = rest.pop(0) if use_sink else None
    bias_ref = rest.pop(0) if band else None
    if given_bound:
        bound_ref, o_ref = rest
    else:
        o_ref, bound_ref = rest
    nq = GROUP * Q_BLOCK
    n_band = Q_BLOCK + 2 * WINDOW if band else 0
    chunks = [(c0, min(c0 + KEY_CHUNK, n_full), False) for c0 in range(0, n_full, KEY_CHUNK)]
    if band:
        chunks.append((n_full, n_full + n_band, True))
    kv_head = pl.program_id(1)
    step = pl.program_id(2)

    if not given_bound:
        @pl.when(step == 0)
        def _():
            q = qt_ref[...].astype(F32)
            qq = jnp.sum(q * q, axis=1)
            k = k_ref[...]
            kk = jnp.dot(k * k, eh_ref[...], preferred_element_type=F32)
            bound_ref[0] = jnp.sqrt(jnp.max(qq) * jnp.max(kk)) * BOUND_SLACK

    def band_start(blk):
        return pl.multiple_of(_band_start(blk, n_ctx, n_tok), LANES)

    def extend_q(blk):
        cols = pl.ds(pl.multiple_of(q_row0 + blk * Q_BLOCK, Q_BLOCK), Q_BLOCK)
        top = jnp.concatenate([qt_ref[g, :, cols] for g in range(GROUP)], axis=1)
        zero = jnp.zeros_like(top)
        return jnp.where(kv_head == 0, jnp.concatenate([top, zero], axis=0),
                         jnp.concatenate([zero, top], axis=0))

    def logits(qm, blk, chunk):
        c0, c1, in_band = chunk
        if in_band:
            variant = jnp.where(blk == 0, 0, jnp.where(blk >= n_lat_blk - 1, 2, 1))
            return jnp.dot(k_ref[pl.ds(band_start(blk), n_band), :], qm,
                           preferred_element_type=F32) + bias_ref[variant]
        return jnp.dot(k_ref[c0:c1, :], qm, preferred_element_type=F32)

    def values_t(blk, chunk):
        c0, c1, in_band = chunk
        if in_band:
            return vt_ref[:, pl.ds(band_start(blk), n_band)]
        return vt_ref[:, c0:c1]

    def column_max(qm, blk):
        mx = None
        for chunk in chunks:
            s = logits(qm, blk, chunk)
            cm = jnp.max(s.reshape(s.shape[0] // 8, 8, nq), axis=0)
            mx = cm if mx is None else jnp.maximum(mx, cm)
        return jnp.max(mx, axis=0, keepdims=True)

    def finalize(t, acc, m):
        l = acc[HEAD_DIM:HEAD_DIM + 1, :]
        if use_sink:
            l = l + jnp.exp2(sink_ref[...] - m)
        o_t = acc[0:HEAD_DIM, :] / l
        o_t = jnp.concatenate([o_t[:, g * Q_BLOCK:(g + 1) * Q_BLOCK] for g in range(GROUP)], axis=0)
        o_ref[t * Q_BLOCK:(t + 1) * Q_BLOCK, :] = o_t.T.astype(BF16)

    def attend_all(qms, shifts):
        items = [(t, c) for t in range(q_per_step) for c in range(len(chunks))]
        blk_of = lambda t: step * q_per_step + t
        s_next = logits(qms[0], blk_of(0), chunks[0])
        acc = [None] * q_per_step
        pv_prev = None

        def retire(prev):
            t, c, pv = prev
            acc[t] = pv if acc[t] is None else acc[t] + pv
            if c == len(chunks) - 1:
                finalize(t, acc[t], shifts[t])

        for n, (t, c) in enumerate(items):
            s = s_next
            if n + 1 < len(items):
                t2, c2 = items[n + 1]
                s_next = logits(qms[t2], blk_of(t2), chunks[c2])
            p = jnp.exp2(s - shifts[t]).astype(BF16)
            pv = jnp.dot(values_t(blk_of(t), chunks[c]), p, preferred_element_type=F32)
            if pv_prev is not None:
                retire(pv_prev)
            pv_prev = (t, c, pv)
        retire(pv_prev)

    def with_sink(m):
        return jnp.maximum(m, sink_ref[...]) if use_sink else m

    bound = bound_ref[0]
    one_pass = bound <= LOGIT_BOUND_LIMIT

    @pl.when(one_pass)
    def _():
        m = with_sink(jnp.full((1, nq), bound, F32))
        qms = [extend_q(step * q_per_step + t) for t in range(q_per_step)]
        attend_all(qms, [m] * q_per_step)

    @pl.when(jnp.logical_not(one_pass))
    def _():
        qms = [extend_q(step * q_per_step + t) for t in range(q_per_step)]
        shifts = [with_sink(column_max(qms[t], step * q_per_step + t)) for t in range(q_per_step)]
        attend_all(qms, shifts)


def _attention(q, kd, vt, sink_rows, head_ones, logit_bound, *, q_blk_off, n_qblk, n_ctx, n_full, band,
               q_per_step, name):
    bsz, n_tok, _ = kd.shape
    q_per_step = min(q_per_step, n_qblk)
    n_rows = n_tok if (band or n_full == n_tok) else max(n_full, (q_blk_off + n_qblk) * Q_BLOCK)
    assert n_qblk % q_per_step == 0
    use_sink = sink_rows is not None
    nq = GROUP * Q_BLOCK
    in_specs = [
        pl.BlockSpec((None, GROUP, HEAD_DIM, n_rows), lambda b, h, j: (b, h, 0, 0)),
        pl.BlockSpec((None, n_rows, KV_COLS), lambda b, h, j: (b, 0, 0)),
        pl.BlockSpec((None, None, VT_ROWS, n_rows), lambda b, h, j: (b, h, 0, 0)),
        pl.BlockSpec(head_ones.shape, lambda b, h, j: (0, 0)),
    ]
    args = [q, kd, vt, head_ones]
    if use_sink:
        in_specs.append(pl.BlockSpec((None, 1, nq), lambda b, h, j: (h, 0, 0)))
        args.append(sink_rows)
    if band:
        bias = _window_bias_variants(n_ctx, n_tok, n_qblk)
        in_specs.append(pl.BlockSpec(bias.shape, lambda b, h, j: (0, 0, 0)))
        args.append(bias)
    given_bound = logit_bound is not None
    if given_bound:
        in_specs.append(pl.BlockSpec(memory_space=pltpu.SMEM))
        args.append(logit_bound)
    kern = functools.partial(_attn_kernel, n_ctx=n_ctx, n_full=n_full, band=band, use_sink=use_sink,
                             n_tok=n_tok, n_lat_blk=n_qblk, q_row0=q_blk_off * Q_BLOCK,
                             q_per_step=q_per_step, given_bound=given_bound)
    rows = q_per_step * Q_BLOCK
    return pl.pallas_call(
        kern,
        grid=(bsz, N_KV, n_qblk // q_per_step),
        in_specs=in_specs,
        out_specs=pl.BlockSpec((None, rows, GROUP * HEAD_DIM), lambda b, h, j: (b, j, h)),
        out_shape=jax.ShapeDtypeStruct((bsz, n_qblk * Q_BLOCK, Q_COLS), BF16),
        scratch_shapes=[] if given_bound else [pltpu.SMEM((1,), F32)],
        compiler_params=pltpu.CompilerParams(
            dimension_semantics=("parallel", "parallel", "arbitrary"), vmem_limit_bytes=VMEM_LIMIT),
        name=name,
    )(*args)


def _post_kernel(*refs, n_sub, has_ctx):
    n_in = n_sub + (1 if has_ctx else 0)
    h_refs, ma_refs, mb_refs = refs[:n_sub], refs[n_in:n_in + n_sub], refs[2 * n_in:2 * n_in + n_sub]
    ctx_h = refs[n_sub] if has_ctx else None
    ctx_ma = refs[n_in + n_sub] if has_ctx else None
    ctx_mb = refs[2 * n_in + n_sub] if has_ctx else None
    mod_ref, wout_ref, gpm_ref, gpre_ref, wup_ref, wdn_ref, gpl_ref, o_ref = refs[3 * n_in:]
    mods = [_sub_mod(mod_ref, has_ctx, i) for i in range(n_sub)]

    def out_proj(i):
        mix = jnp.concatenate([_sub_tile(ma_refs, ctx_ma, i), _sub_tile(mb_refs, ctx_mb, i)], axis=1)
        return jnp.dot(mix, wout_ref[...], preferred_element_type=F32)

    def mixer_residual(i, mix):
        mod = mods[i]
        h = _sub_tile(h_refs, ctx_h, i) + mod[2:3, :] * _rms(mix, gpm_ref[...])
        u = (_rms(h, gpre_ref[...]) * (1 + mod[4:5, :]) + mod[3:4, :]).astype(BF16)
        return h, u

    def mlp_chunk(u, y, c0):
        a = jnp.dot(u, wup_ref[:, c0:c0 + FF_CHUNK], preferred_element_type=F32)
        a = jnp.square(jnp.maximum(a, 0.0)).astype(BF16)
        part = jnp.dot(a, wdn_ref[c0:c0 + FF_CHUNK, :], preferred_element_type=F32)
        return part if y is None else y + part

    def mlp_residual(i, h, y):
        o_ref[i * TOK_TILE:(i + 1) * TOK_TILE, :] = h + mods[i][5:6, :] * _rms(y, gpl_ref[...])

    hu = [None] * n_sub
    mix_prev = None
    for i in range(n_sub):
        mix = out_proj(i)
        if i > 0:
            hu[i - 1] = mixer_residual(i - 1, mix_prev)
        mix_prev = mix
    for i in range(n_sub):
        y = None
        for c, c0 in enumerate(range(0, D_FF, FF_CHUNK)):
            if i == 0 and c == 0:
                hu[n_sub - 1] = mixer_residual(n_sub - 1, mix_prev)
            y = mlp_chunk(hu[i][1], y, c0)
            if i > 0 and c == 0:
                mlp_residual(i - 1, hu[i - 1][0], y_prev)
        y_prev = y
    mlp_residual(n_sub - 1, hu[n_sub - 1][0], y_prev)


def _post(h_lat, h_ctx, lat_tile_off, n_tiles, mix_a, mix_b, ctx_mix_a, ctx_mix_b, mods, layer, w_out,
          g_post_mix, g_pre_mlp, w_up, w_down, g_post_mlp):
    bsz = h_lat.shape[0]
    has_ctx = h_ctx is not None
    n_sub = max(s for s in POST_SUBS if n_tiles % s == 0)
    const = lambda b, t: (0, 0)
    resident = lambda rows, cols: pl.BlockSpec((None, rows, cols), lambda b, t: (layer, 0, 0),
                                               pipeline_mode=pl.Buffered(1))
    ctx_args = lambda v: [v] if has_ctx else []
    args = ([h_lat] * n_sub + ctx_args(h_ctx) + [mix_a] * n_sub + ctx_args(ctx_mix_a)
            + [mix_b] * n_sub + ctx_args(ctx_mix_b))
    return pl.pallas_call(
        functools.partial(_post_kernel, n_sub=n_sub, has_ctx=has_ctx),
        grid=(bsz, n_tiles // n_sub),
        in_specs=(_stream_specs(n_sub, D_MODEL, lat_tile_off, has_ctx)
                  + _stream_specs(n_sub, Q_COLS, 0, has_ctx)
                  + _stream_specs(n_sub, Q_COLS, 0, has_ctx) + [
            pl.BlockSpec((None, 2, N_MOD, D_MODEL), lambda b, t: (b, 0, 0, 0)),
            resident(D_MODEL, D_MODEL),
            pl.BlockSpec((1, D_MODEL), const),
            pl.BlockSpec((1, D_MODEL), const),
            resident(D_MODEL, D_FF),
            resident(D_FF, D_MODEL),
            pl.BlockSpec((1, D_MODEL), const),
        ]),
        out_specs=pl.BlockSpec((None, n_sub * TOK_TILE, D_MODEL), lambda b, t: (b, t, 0)),
        out_shape=jax.ShapeDtypeStruct((bsz, n_tiles * TOK_TILE, D_MODEL), F32),
        compiler_params=pltpu.CompilerParams(
            dimension_semantics=("parallel", "parallel"), vmem_limit_bytes=VMEM_LIMIT),
        name="post",
    )(*args, mods, w_out, g_post_mix, g_pre_mlp, w_up, w_down, g_post_mlp)


def _rope_tables(n_ctx, n_lat):
    rows = n_lat // GRID_W
    row_ids = jnp.repeat(jnp.arange(rows, dtype=jnp.int32), GRID_W).astype(F32)
    col_ids = jnp.tile(jnp.arange(GRID_W, dtype=jnp.int32), rows).astype(F32)
    axis_dim = HEAD_DIM // 2
    inv = ROPE_THETA ** (-jnp.arange(0, axis_dim, 2, dtype=F32) / axis_dim)
    ang_r = row_ids[:, None] * inv[None, :]
    ang_c = col_ids[:, None] * inv[None, :]
    cos_r, sin_r, cos_c, sin_c = jnp.cos(ang_r), jnp.sin(ang_r), jnp.cos(ang_c), jnp.sin(ang_c)
    zero = jnp.zeros_like(sin_r)
    cos_h = jnp.concatenate([cos_r, cos_r, cos_c, cos_c], axis=-1)
    sa_h = jnp.concatenate([-sin_r, zero, -sin_c, zero], axis=-1)
    sb_h = jnp.concatenate([zero, sin_r, zero, sin_c], axis=-1)

    def full(t, ctx_value):
        t = jnp.tile(t, (1, LANES // HEAD_DIM))
        return jnp.concatenate([jnp.full((n_ctx, LANES), ctx_value, F32), t], axis=0)

    return full(cos_h, 1.0), full(sa_h, 0.0), full(sb_h, 0.0)


def _head_ones(n_cols):
    head = jnp.arange(n_cols, dtype=jnp.int32) // HEAD_DIM
    return (head[:, None] == head[None, :]).astype(BF16)


def kernel(x, c, ctx, c_ctx, w_ada, b_ada, g_pre_mix, g_post_mix, g_pre_mlp, g_post_mlp,
           w_in, q_norm, k_norm, sink, w_out, w_up, w_down):
    bsz, n_lat, d = x.shape
    n_ctx = ctx.shape[1]
    depth = w_in.shape[0]
    n_tok = n_ctx + n_lat
    assert d == D_MODEL and n_ctx == TOK_TILE and n_lat % TOK_TILE == 0 and bsz + 1 <= ADA_ROWS
    assert w_in.shape[2] == IN_COLS and w_up.shape[2] == D_FF

    cc = jnp.concatenate([c, c_ctx[None, :], jnp.zeros((ADA_ROWS - bsz - 1, d), F32)], axis=0)
    mod_all = _ada(cc, w_ada, b_ada)
    mod_all = mod_all.reshape(depth, ADA_ROWS, N_MOD, d)
    mod_ctx = jnp.broadcast_to(mod_all[:, bsz:bsz + 1], (depth, bsz, N_MOD, d))
    mods = jnp.stack([mod_ctx, mod_all[:, :bsz]], axis=2)

    cos_t, sa_t, sb_t = _rope_tables(n_ctx, n_lat)
    eq = _head_ones(Q_COLS)
    ek = _head_ones(KV_COLS)
    eh = _head_ones(KV_COLS)
    row = lambda v: v.reshape(1, -1)

    w_in_b = w_in.astype(BF16)
    w_out_b = w_out.astype(BF16)
    w_up_b = w_up.astype(BF16)
    w_down_b = w_down.astype(BF16)

    ctx_tiles = n_ctx // TOK_TILE
    assert depth >= 1 and ctx_tiles == 1
    ctx_qblk = n_ctx // Q_BLOCK
    lat_qblk = n_lat // Q_BLOCK

    h_lat, h_ctx = x, ctx
    lat_off = 0
    for l in range(depth):
        last = l == depth - 1
        gq = row(jnp.tile(q_norm[l], N_HEADS))
        gk = row(jnp.tile(k_norm[l], N_KV))
        sink_rows = jnp.repeat(sink[l].reshape(N_KV, 1, GROUP) * LOG2_E, Q_BLOCK, axis=2)

        qa, kda, vta, qb, kdb, vtb = _proj(h_lat, h_ctx, 0, n_tok // TOK_TILE, mods[l],
                                           row(g_pre_mix[l]), w_in_b, l, gq, gk, eq, ek,
                                           cos_t, sa_t, sb_t)
        bound_a = (HEAD_DIM * Q_SCALE * BOUND_SLACK * jnp.max(jnp.abs(q_norm[l]))
                   * jnp.max(jnp.abs(k_norm[l]))).reshape(1)
        oa = _attention(qa, kda, vta, None, eh, bound_a, q_blk_off=ctx_qblk, n_qblk=lat_qblk,
                        n_ctx=n_ctx, n_full=n_tok, band=False, q_per_step=Q_PER_STEP_GLOBAL,
                        name="attn_global")
        ob = _attention(qb, kdb, vtb, sink_rows, eh, None, q_blk_off=ctx_qblk, n_qblk=lat_qblk,
                        n_ctx=n_ctx, n_full=n_ctx, band=True, q_per_step=Q_PER_STEP_WINDOW,
                        name="attn_window")
        post = functools.partial(_post, mods=mods[l], layer=l, w_out=w_out_b, g_post_mix=row(g_post_mix[l]),
                                 g_pre_mlp=row(g_pre_mlp[l]), w_up=w_up_b, w_down=w_down_b,
                                 g_post_mlp=row(g_post_mlp[l]))
        if not last:
            oac = _attention(qa, kda, vta, None, eh, bound_a, q_blk_off=0, n_qblk=ctx_qblk, n_ctx=n_ctx,
                             n_full=n_ctx, band=False, q_per_step=ctx_qblk, name="attn_ctx_global")
            obc = _attention(qb, kdb, vtb, sink_rows, eh, None, q_blk_off=0, n_qblk=ctx_qblk, n_ctx=n_ctx,
                             n_full=n_ctx, band=False, q_per_step=ctx_qblk, name="attn_ctx_sink")
            assert h_ctx is not None
            h_lat = post(h_lat, h_ctx, lat_off, n_tok // TOK_TILE, oa, ob, oac, obc)
            h_ctx, lat_off = None, ctx_tiles
        else:
            if h_ctx is not None:
                h_lat = post(h_lat, None, 0, n_lat // TOK_TILE, oa, ob, None, None)
            else:
                h_lat = post(h_lat, None, lat_off, n_lat // TOK_TILE, oa, ob, None, None)
    return h_lat
```

```python
import functools

import jax
import jax.numpy as jnp
from jax import lax
from jax.experimental import pallas as pl
from jax.experimental.pallas import tpu as pltpu

F32 = jnp.float32
BF16 = jnp.bfloat16

D_MODEL = 1024
HEAD_DIM = 64
N_HEADS = 8
N_KV = 2
GROUP = N_HEADS // N_KV
Q_COLS = N_HEADS * HEAD_DIM
KV_COLS = N_KV * HEAD_DIM
IN_COLS = 2 * (Q_COLS + 2 * KV_COLS)
D_FF = 4 * D_MODEL
GRID_W = 64
WINDOW = 128
Q_BLOCK = 128
ROPE_THETA = 10000.0
EPS = 1e-6
NEG_BIG = -1e30
N_MOD = 6
ROPE_SHIFT = HEAD_DIM // 4
LOG2_E = 1.4426950408889634
Q_SCALE = HEAD_DIM ** -0.5 * LOG2_E

LANES = 128
TOK_TILE = 256
PROJ_SUB = 9
POST_SUBS = (1, 2, 3, 4)
VT_ROWS = HEAD_DIM + 16
KEY_CHUNK = 768
Q_PER_STEP_GLOBAL = 8
Q_PER_STEP_WINDOW = 16
LOGIT_BOUND_LIMIT = 48.0
BOUND_SLACK = 1.01
FF_CHUNK = 1024
ADA_ROWS = 32
ADA_COLS = 1536
VMEM_LIMIT = 56 * 1024 * 1024


def _rms(x, g):
    return x * lax.rsqrt(jnp.mean(x * x, axis=-1, keepdims=True) + EPS) * g


def _ada_kernel(c_ref, w_ref, b_ref, o_ref):
    c = c_ref[...]
    a = c * jax.nn.sigmoid(c)
    w = w_ref[...]
    a_hi = a.astype(BF16)
    a_lo = (a - a_hi.astype(F32)).astype(BF16)
    w_hi = w.astype(BF16)
    w_lo = (w - w_hi.astype(F32)).astype(BF16)
    acc = jnp.dot(a_hi, w_hi, preferred_element_type=F32)
    acc += jnp.dot(a_lo, w_hi, preferred_element_type=F32)
    acc += jnp.dot(a_hi, w_lo, preferred_element_type=F32)
    o_ref[...] = acc + b_ref[...]


def _ada(cc, w_ada, b_ada):
    depth = w_ada.shape[0]
    n_out = w_ada.shape[2]
    return pl.pallas_call(
        _ada_kernel,
        grid=(depth, n_out // ADA_COLS),
        in_specs=[
            pl.BlockSpec((ADA_ROWS, D_MODEL), lambda l, j: (0, 0)),
            pl.BlockSpec((None, D_MODEL, ADA_COLS), lambda l, j: (l, 0, j)),
            pl.BlockSpec((None, 1, ADA_COLS), lambda l, j: (l, 0, j)),
        ],
        out_specs=pl.BlockSpec((None, ADA_ROWS, ADA_COLS), lambda l, j: (l, 0, j)),
        out_shape=jax.ShapeDtypeStruct((depth, ADA_ROWS, n_out), F32),
        compiler_params=pltpu.CompilerParams(
            dimension_semantics=("arbitrary", "arbitrary"), vmem_limit_bytes=VMEM_LIMIT),
        name="ada",
    )(cc, w_ada, b_ada.reshape(depth, 1, n_out))


def _stream_specs(n_sub, block_cols, lat_tile_off, has_ctx):
    specs = []
    for i in range(n_sub):
        if has_ctx:
            idx = lambda b, t, i=i: (b, jnp.maximum(n_sub * t + i - 1, 0), 0)
        else:
            idx = lambda b, t, i=i: (b, lat_tile_off + n_sub * t + i, 0)
        specs.append(pl.BlockSpec((None, TOK_TILE, block_cols), idx))
    if has_ctx:
        specs.append(pl.BlockSpec((None, TOK_TILE, block_cols), lambda b, t: (b, 0, 0)))
    return specs


def _sub_tile(refs, ctx_ref, i):
    v = refs[i][...]
    if ctx_ref is not None and i == 0:
        v = jnp.where(pl.program_id(1) == 0, ctx_ref[...], v)
    return v


def _sub_mod(mod_ref, ctx_first, i):
    if ctx_first and i == 0:
        return jnp.where(pl.program_id(1) == 0, mod_ref[0], mod_ref[1])
    return mod_ref[1]


def _proj_kernel(*refs, n_sub, has_ctx):
    h_refs = refs[:n_sub]
    refs = refs[n_sub:]
    ctx_ref = None
    if has_ctx:
        ctx_ref, refs = refs[0], refs[1:]
    (mod_ref, gpre_ref, w_ref, gq_ref, gk_ref, eq_ref, ek_ref, cos_ref, sa_ref, sb_ref,
     qa_ref, kda_ref, vta_ref, qb_ref, kdb_ref, vtb_ref) = refs
    kv = 2 * KV_COLS

    def prenorm(i):
        mod = _sub_mod(mod_ref, True, i)
        x = _sub_tile(h_refs, ctx_ref, i)
        return (_rms(x, gpre_ref[...]) * (1 + mod[1:2, :]) + mod[0:1, :]).astype(BF16)

    def stages(i, u):
        rows = slice(i * TOK_TILE, (i + 1) * TOK_TILE)
        cos = cos_ref[rows, :]
        sa = sa_ref[rows, :]
        sb = sb_ref[rows, :]

        def rope(t):
            up = pltpu.roll(t, LANES - ROPE_SHIFT, 1)
            dn = pltpu.roll(t, ROPE_SHIFT, 1)
            return t * cos + up * sa + dn * sb

        def head_norm(t, e_ref, g_ref):
            ss = jnp.dot((t * t).astype(BF16), e_ref[...], preferred_element_type=F32)
            return t * lax.rsqrt(ss * (1.0 / HEAD_DIM) + EPS) * g_ref[...]

        def store_q(q, qt_ref):
            q = jnp.concatenate([rope(q[:, j * LANES:(j + 1) * LANES]) for j in range(Q_COLS // LANES)],
                                axis=1) * Q_SCALE
            q_t = q.T.astype(BF16)
            for h in range(N_HEADS):
                qt_ref[h, :, rows] = q_t[h * HEAD_DIM:(h + 1) * HEAD_DIM, :]

        def store_k(k, k_ref):
            k_ref[rows, :] = rope(k).astype(BF16)

        def store_vt(v, vt_ref):
            v_t = v.T.astype(BF16)
            ones = jnp.ones((VT_ROWS - HEAD_DIM, TOK_TILE), BF16)
            for h in range(N_KV):
                vt_ref[h, 0:HEAD_DIM, rows] = v_t[h * HEAD_DIM:(h + 1) * HEAD_DIM, :]
                vt_ref[h, HEAD_DIM:VT_ROWS, rows] = ones

        def project(c0, n):
            return lambda: jnp.dot(u, w_ref[:, c0:c0 + n], preferred_element_type=F32)

        def epi_qa(z):
            store_q(head_norm(z, eq_ref, gq_ref), qa_ref)

        def epi_kva(z):
            store_k(head_norm(z[:, 0:KV_COLS], ek_ref, gk_ref), kda_ref)
            store_vt(z[:, KV_COLS:kv], vta_ref)

        def epi_qb(z):
            store_q(z, qb_ref)

        def epi_kvb(z):
            store_k(z[:, 0:KV_COLS], kdb_ref)
            store_vt(z[:, KV_COLS:kv], vtb_ref)

        return [(project(0, Q_COLS), epi_qa), (project(Q_COLS, kv), epi_kva),
                (project(Q_COLS + kv, Q_COLS), epi_qb), (project(2 * Q_COLS + kv, kv), epi_kvb)]

    u_next = prenorm(0)
    pending = []
    for i in range(n_sub):
        u = u_next
        results = []
        for g, (matmul, epilogue) in enumerate(stages(i, u)):
            results.append((epilogue, matmul()))
            if g == 0 and i + 1 < n_sub:
                u_next = prenorm(i + 1)
            if pending:
                epi, z = pending.pop(0)
                epi(z)
        for epi, z in pending:
            epi(z)
        pending = results
    for epi, z in pending:
        epi(z)


def _proj(h_lat, h_ctx, lat_tile_off, n_tiles, mods, g_pre, w_in, layer, gq, gk, eq, ek, cos_t, sa_t,
          sb_t):
    bsz = h_lat.shape[0]
    has_ctx = h_ctx is not None
    n_sub = PROJ_SUB
    assert n_tiles % n_sub == 0
    n_tok = n_tiles * TOK_TILE
    rows = n_sub * TOK_TILE
    const = lambda b, t: (0, 0)
    tok = lambda b, t: (b, t, 0)
    q_shape = jax.ShapeDtypeStruct((bsz, N_HEADS, HEAD_DIM, n_tok), BF16)
    kd_shape = jax.ShapeDtypeStruct((bsz, n_tok, KV_COLS), BF16)
    vt_shape = jax.ShapeDtypeStruct((bsz, N_KV, VT_ROWS, n_tok), BF16)
    q_spec = pl.BlockSpec((None, N_HEADS, HEAD_DIM, rows), lambda b, t: (b, 0, 0, t))
    kd_spec = pl.BlockSpec((None, rows, KV_COLS), tok)
    vt_spec = pl.BlockSpec((None, N_KV, VT_ROWS, rows), lambda b, t: (b, 0, 0, t))
    rope_spec = pl.BlockSpec((rows, LANES), lambda b, t: (t, 0))
    h_args = [h_lat] * n_sub + ([h_ctx] if has_ctx else [])
    return pl.pallas_call(
        functools.partial(_proj_kernel, n_sub=n_sub, has_ctx=has_ctx),
        grid=(bsz, n_tiles // n_sub),
        in_specs=_stream_specs(n_sub, D_MODEL, lat_tile_off, has_ctx) + [
            pl.BlockSpec((None, 2, N_MOD, D_MODEL), lambda b, t: (b, 0, 0, 0)),
            pl.BlockSpec((1, D_MODEL), const),
            pl.BlockSpec((None, D_MODEL, IN_COLS), lambda b, t: (layer, 0, 0)),
            pl.BlockSpec((1, Q_COLS), const),
            pl.BlockSpec((1, KV_COLS), const),
            pl.BlockSpec((Q_COLS, Q_COLS), const),
            pl.BlockSpec((KV_COLS, KV_COLS), const),
            rope_spec, rope_spec, rope_spec,
        ],
        out_specs=[q_spec, kd_spec, vt_spec, q_spec, kd_spec, vt_spec],
        out_shape=[q_shape, kd_shape, vt_shape, q_shape, kd_shape, vt_shape],
        compiler_params=pltpu.CompilerParams(
            dimension_semantics=("parallel", "parallel"), vmem_limit_bytes=VMEM_LIMIT),
        name="proj",
    )(*h_args, mods, g_pre, w_in, gq, gk, eq, ek, cos_t, sa_t, sb_t)


def _band_start(blk, n_ctx, n_tok):
    return jnp.minimum(n_ctx + (blk - 1) * Q_BLOCK, n_tok - (Q_BLOCK + 2 * WINDOW))


def _window_bias_variants(n_ctx, n_tok, n_lat_blk):
    n_band = Q_BLOCK + 2 * WINDOW
    kj = jnp.arange(n_band, dtype=jnp.int32)[:, None]
    qr = jnp.arange(GROUP * Q_BLOCK, dtype=jnp.int32)[None, :] % Q_BLOCK
    variants = []
    for blk in (0, 1, n_lat_blk - 1):
        kpos = kj + (_band_start(blk, n_ctx, n_tok) - n_ctx)
        qpos = qr + blk * Q_BLOCK
        valid = (jnp.abs(kpos - qpos) <= WINDOW) & (kpos >= 0)
        variants.append(jnp.where(valid, 0.0, NEG_BIG).astype(F32))
    return jnp.stack(variants)


def _attn_kernel(*refs, n_ctx, n_full, band, use_sink, n_tok, n_lat_blk, q_row0, q_per_step,
                 given_bound):
    refs = list(refs)
    qt_ref, k_ref, vt_ref, eh_ref = refs[:4]
    rest = refs[4:]
    sink_ref = rest.pop(0) if use_sink else None
    bias_ref = rest.pop(0) if band else None
    if given_bound:
        bound_ref, o_ref = rest
    else:
        o_ref, bound_ref = rest
    nq = GROUP * Q_BLOCK
    n_band = Q_BLOCK + 2 * WINDOW if band else 0
    chunks = [(c0, min(c0 + KEY_CHUNK, n_full), False) for c0 in range(0, n_full, KEY_CHUNK)]
    if band:
        chunks.append((n_full, n_full + n_band, True))
    kv_head = pl.program_id(1)
    step = pl.program_id(2)

    if not given_bound:
        @pl.when(step == 0)
        def _():
            q = qt_ref[...].astype(F32)
            qq = jnp.sum(q * q, axis=1)
            k = k_ref[...]
            kk = jnp.dot(k * k, eh_ref[...], preferred_element_type=F32)
            bound_ref[0] = jnp.sqrt(jnp.max(qq) * jnp.max(kk)) * BOUND_SLACK

    def band_start(blk):
        return pl.multiple_of(_band_start(blk, n_ctx, n_tok), LANES)

    def extend_q(blk):
        cols = pl.ds(pl.multiple_of(q_row0 + blk * Q_BLOCK, Q_BLOCK), Q_BLOCK)
        top = jnp.concatenate([qt_ref[g, :, cols] for g in range(GROUP)], axis=1)
        zero = jnp.zeros_like(top)
        return jnp.where(kv_head == 0, jnp.concatenate([top, zero], axis=0),
                         jnp.concatenate([zero, top], axis=0))

    def logits(qm, blk, chunk):
        c0, c1, in_band = chunk
        if in_band:
            variant = jnp.where(blk == 0, 0, jnp.where(blk >= n_lat_blk - 1, 2, 1))
            return jnp.dot(k_ref[pl.ds(band_start(blk), n_band), :], qm,
                           preferred_element_type=F32) + bias_ref[variant]
        return jnp.dot(k_ref[c0:c1, :], qm, preferred_element_type=F32)

    def values_t(blk, chunk):
        c0, c1, in_band = chunk
        if in_band:
            return vt_ref[:, pl.ds(band_start(blk), n_band)]
        return vt_ref[:, c0:c1]

    def column_max(qm, blk):
        mx = None
        for chunk in chunks:
            s = logits(qm, blk, chunk)
            cm = jnp.max(s.reshape(s.shape[0] // 8, 8, nq), axis=0)
            mx = cm if mx is None else jnp.maximum(mx, cm)
        return jnp.max(mx, axis=0, keepdims=True)

    def finalize(t, acc, m):
        l = acc[HEAD_DIM:HEAD_DIM + 1, :]
        if use_sink:
            l = l + jnp.exp2(sink_ref[...] - m)
        o_t = acc[0:HEAD_DIM, :] / l
        o_t = jnp.concatenate([o_t[:, g * Q_BLOCK:(g + 1) * Q_BLOCK] for g in range(GROUP)], axis=0)
        o_ref[t * Q_BLOCK:(t + 1) * Q_BLOCK, :] = o_t.T.astype(BF16)

    def attend_all(qms, shifts):
        items = [(t, c) for t in range(q_per_step) for c in range(len(chunks))]
        blk_of = lambda t: step * q_per_step + t
        s_next = logits(qms[0], blk_of(0), chunks[0])
        acc = [None] * q_per_step
        pv_prev = None

        def retire(prev):
            t, c, pv = prev
            acc[t] = pv if acc[t] is None else acc[t] + pv
            if c == len(chunks) - 1:
                finalize(t, acc[t], shifts[t])

        for n, (t, c) in enumerate(items):
            s = s_next
            if n + 1 < len(items):
                t2, c2 = items[n + 1]
                s_next = logits(qms[t2], blk_of(t2), chunks[c2])
            p = jnp.exp2(s - shifts[t]).astype(BF16)
            pv = jnp.dot(values_t(blk_of(t), chunks[c]), p, preferred_element_type=F32)
            if pv_prev is not None:
                retire(pv_prev)
            pv_prev = (t, c, pv)
        retire(pv_prev)

    def with_sink(m):
        return jnp.maximum(m, sink_ref[...]) if use_sink else m

    bound = bound_ref[0]
    one_pass = bound <= LOGIT_BOUND_LIMIT

    @pl.when(one_pass)
    def _():
        m = with_sink(jnp.full((1, nq), bound, F32))
        qms = [extend_q(step * q_per_step + t) for t in range(q_per_step)]
        attend_all(qms, [m] * q_per_step)

    @pl.when(jnp.logical_not(one_pass))
    def _():
        qms = [extend_q(step * q_per_step + t) for t in range(q_per_step)]
        shifts = [with_sink(column_max(qms[t], step * q_per_step + t)) for t in range(q_per_step)]
        attend_all(qms, shifts)


def _attention(q, kd, vt, sink_rows, head_ones, logit_bound, *, q_blk_off, n_qblk, n_ctx, n_full, band,
               q_per_step, name):
    bsz, n_tok, _ = kd.shape
    q_per_step = min(q_per_step, n_qblk)
    n_rows = n_tok if (band or n_full == n_tok) else max(n_full, (q_blk_off + n_qblk) * Q_BLOCK)
    assert n_qblk % q_per_step == 0
    use_sink = sink_rows is not None
    nq = GROUP * Q_BLOCK
    in_specs = [
        pl.BlockSpec((None, GROUP, HEAD_DIM, n_rows), lambda b, h, j: (b, h, 0, 0)),
        pl.BlockSpec((None, n_rows, KV_COLS), lambda b, h, j: (b, 0, 0)),
        pl.BlockSpec((None, None, VT_ROWS, n_rows), lambda b, h, j: (b, h, 0, 0)),
        pl.BlockSpec(head_ones.shape, lambda b, h, j: (0, 0)),
    ]
    args = [q, kd, vt, head_ones]
    if use_sink:
        in_specs.append(pl.BlockSpec((None, 1, nq), lambda b, h, j: (h, 0, 0)))
        args.append(sink_rows)
    if band:
        bias = _window_bias_variants(n_ctx, n_tok, n_qblk)
        in_specs.append(pl.BlockSpec(bias.shape, lambda b, h, j: (0, 0, 0)))
        args.append(bias)
    given_bound = logit_bound is not None
    if given_bound:
        in_specs.append(pl.BlockSpec(memory_space=pltpu.SMEM))
        args.append(logit_bound)
    kern = functools.partial(_attn_kernel, n_ctx=n_ctx, n_full=n_full, band=band, use_sink=use_sink,
                             n_tok=n_tok, n_lat_blk=n_qblk, q_row0=q_blk_off * Q_BLOCK,
                             q_per_step=q_per_step, given_bound=given_bound)
    rows = q_per_step * Q_BLOCK
    return pl.pallas_call(
        kern,
        grid=(bsz, N_KV, n_qblk // q_per_step),
        in_specs=in_specs,
        out_specs=pl.BlockSpec((None, rows, GROUP * HEAD_DIM), lambda b, h, j: (b, j, h)),
        out_shape=jax.ShapeDtypeStruct((bsz, n_qblk * Q_BLOCK, Q_COLS), BF16),
        scratch_shapes=[] if given_bound else [pltpu.SMEM((1,), F32)],
        compiler_params=pltpu.CompilerParams(
            dimension_semantics=("parallel", "parallel", "arbitrary"), vmem_limit_bytes=VMEM_LIMIT),
        name=name,
    )(*args)


def _post_kernel(*refs, n_sub, has_ctx):
    n_in = n_sub + (1 if has_ctx else 0)
    h_refs, ma_refs, mb_refs = refs[:n_sub], refs[n_in:n_in + n_sub], refs[2 * n_in:2 * n_in + n_sub]
    ctx_h = refs[n_sub] if has_ctx else None
    ctx_ma = refs[n_in + n_sub] if has_ctx else None
    ctx_mb = refs[2 * n_in + n_sub] if has_ctx else None
    mod_ref, wout_ref, gpm_ref, gpre_ref, wup_ref, wdn_ref, gpl_ref, o_ref = refs[3 * n_in:]
    mods = [_sub_mod(mod_ref, has_ctx, i) for i in range(n_sub)]

    def out_proj(i):
        mix = jnp.concatenate([_sub_tile(ma_refs, ctx_ma, i), _sub_tile(mb_refs, ctx_mb, i)], axis=1)
        return jnp.dot(mix, wout_ref[...], preferred_element_type=F32)

    def mixer_residual(i, mix):
        mod = mods[i]
        h = _sub_tile(h_refs, ctx_h, i) + mod[2:3, :] * _rms(mix, gpm_ref[...])
        u = (_rms(h, gpre_ref[...]) * (1 + mod[4:5, :]) + mod[3:4, :]).astype(BF16)
        return h, u

    def mlp_chunk(u, y, c0):
        a = jnp.dot(u, wup_ref[:, c0:c0 + FF_CHUNK], preferred_element_type=F32)
        a = jnp.square(jnp.maximum(a, 0.0)).astype(BF16)
        part = jnp.dot(a, wdn_ref[c0:c0 + FF_CHUNK, :], preferred_element_type=F32)
        return part if y is None else y + part

    def mlp_residual(i, h, y):
        o_ref[i * TOK_TILE:(i + 1) * TOK_TILE, :] = h + mods[i][5:6, :] * _rms(y, gpl_ref[...])

    hu = [None] * n_sub
    mix_prev = None
    for i in range(n_sub):
        mix = out_proj(i)
        if i > 0:
            hu[i - 1] = mixer_residual(i - 1, mix_prev)
        mix_prev = mix
    for i in range(n_sub):
        y = None
        for c, c0 in enumerate(range(0, D_FF, FF_CHUNK)):
            if i == 0 and c == 0:
                hu[n_sub - 1] = mixer_residual(n_sub - 1, mix_prev)
            y = mlp_chunk(hu[i][1], y, c0)
            if i > 0 and c == 0:
                mlp_residual(i - 1, hu[i - 1][0], y_prev)
        y_prev = y
    mlp_residual(n_sub - 1, hu[n_sub - 1][0], y_prev)


def _post(h_lat, h_ctx, lat_tile_off, n_tiles, mix_a, mix_b, ctx_mix_a, ctx_mix_b, mods, layer, w_out,
          g_post_mix, g_pre_mlp, w_up, w_down, g_post_mlp):
    bsz = h_lat.shape[0]
    has_ctx = h_ctx is not None
    n_sub = max(s for s in POST_SUBS if n_tiles % s == 0)
    const = lambda b, t: (0, 0)
    resident = lambda rows, cols: pl.BlockSpec((None, rows, cols), lambda b, t: (layer, 0, 0),
                                               pipeline_mode=pl.Buffered(1))
    ctx_args = lambda v: [v] if has_ctx else []
    args = ([h_lat] * n_sub + ctx_args(h_ctx) + [mix_a] * n_sub + ctx_args(ctx_mix_a)
            + [mix_b] * n_sub + ctx_args(ctx_mix_b))
    return pl.pallas_call(
        functools.partial(_post_kernel, n_sub=n_sub, has_ctx=has_ctx),
        grid=(bsz, n_tiles // n_sub),
        in_specs=(_stream_specs(n_sub, D_MODEL, lat_tile_off, has_ctx)
                  + _stream_specs(n_sub, Q_COLS, 0, has_ctx)
                  + _stream_specs(n_sub, Q_COLS, 0, has_ctx) + [
            pl.BlockSpec((None, 2, N_MOD, D_MODEL), lambda b, t: (b, 0, 0, 0)),
            resident(D_MODEL, D_MODEL),
            pl.BlockSpec((1, D_MODEL), const),
            pl.BlockSpec((1, D_MODEL), const),
            resident(D_MODEL, D_FF),
            resident(D_FF, D_MODEL),
            pl.BlockSpec((1, D_MODEL), const),
        ]),
        out_specs=pl.BlockSpec((None, n_sub * TOK_TILE, D_MODEL), lambda b, t: (b, t, 0)),
        out_shape=jax.ShapeDtypeStruct((bsz, n_tiles * TOK_TILE, D_MODEL), F32),
        compiler_params=pltpu.CompilerParams(
            dimension_semantics=("parallel", "parallel"), vmem_limit_bytes=VMEM_LIMIT),
        name="post",
    )(*args, mods, w_out, g_post_mix, g_pre_mlp, w_up, w_down, g_post_mlp)


def _rope_tables(n_ctx, n_lat):
    rows = n_lat // GRID_W
    row_ids = jnp.repeat(jnp.arange(rows, dtype=jnp.int32), GRID_W).astype(F32)
    col_ids = jnp.tile(jnp.arange(GRID_W, dtype=jnp.int32), rows).astype(F32)
    axis_dim = HEAD_DIM // 2
    inv = ROPE_THETA ** (-jnp.arange(0, axis_dim, 2, dtype=F32) / axis_dim)
    ang_r = row_ids[:, None] * inv[None, :]
    ang_c = col_ids[:, None] * inv[None, :]
    cos_r, sin_r, cos_c, sin_c = jnp.cos(ang_r), jnp.sin(ang_r), jnp.cos(ang_c), jnp.sin(ang_c)
    zero = jnp.zeros_like(sin_r)
    cos_h = jnp.concatenate([cos_r, cos_r, cos_c, cos_c], axis=-1)
    sa_h = jnp.concatenate([-sin_r, zero, -sin_c, zero], axis=-1)
    sb_h = jnp.concatenate([zero, sin_r, zero, sin_c], axis=-1)

    def full(t, ctx_value):
        t = jnp.tile(t, (1, LANES // HEAD_DIM))
        return jnp.concatenate([jnp.full((n_ctx, LANES), ctx_value, F32), t], axis=0)

    return full(cos_h, 1.0), full(sa_h, 0.0), full(sb_h, 0.0)


def _head_ones(n_cols):
    head = jnp.arange(n_cols, dtype=jnp.int32) // HEAD_DIM
    return (head[:, None] == head[None, :]).astype(BF16)


def kernel(x, c, ctx, c_ctx, w_ada, b_ada, g_pre_mix, g_post_mix, g_pre_mlp, g_post_mlp,
           w_in, q_norm, k_norm, sink, w_out, w_up, w_down):
    bsz, n_lat, d = x.shape
    n_ctx = ctx.shape[1]
    depth = w_in.shape[0]
    n_tok = n_ctx + n_lat
    assert d == D_MODEL and n_ctx == TOK_TILE and n_lat % TOK_TILE == 0 and bsz + 1 <= ADA_ROWS
    assert w_in.shape[2] == IN_COLS and w_up.shape[2] == D_FF

    cc = jnp.concatenate([c, c_ctx[None, :], jnp.zeros((ADA_ROWS - bsz - 1, d), F32)], axis=0)
    mod_all = _ada(cc, w_ada, b_ada)
    mod_all = mod_all.reshape(depth, ADA_ROWS, N_MOD, d)
    mod_ctx = jnp.broadcast_to(mod_all[:, bsz:bsz + 1], (depth, bsz, N_MOD, d))
    mods = jnp.stack([mod_ctx, mod_all[:, :bsz]], axis=2)

    cos_t, sa_t, sb_t = _rope_tables(n_ctx, n_lat)
    eq = _head_ones(Q_COLS)
    ek = _head_ones(KV_COLS)
    eh = _head_ones(KV_COLS)
    row = lambda v: v.reshape(1, -1)

    w_in_b = w_in.astype(BF16)
    w_out_b = w_out.astype(BF16)
    w_up_b = w_up.astype(BF16)
    w_down_b = w_down.astype(BF16)

    ctx_tiles = n_ctx // TOK_TILE
    assert depth >= 1 and ctx_tiles == 1
    ctx_qblk = n_ctx // Q_BLOCK
    lat_qblk = n_lat // Q_BLOCK

    h_lat, h_ctx = x, ctx
    lat_off = 0
    for l in range(depth):
        last = l == depth - 1
        gq = row(jnp.tile(q_norm[l], N_HEADS))
        gk = row(jnp.tile(k_norm[l], N_KV))
        sink_rows = jnp.repeat(sink[l].reshape(N_KV, 1, GROUP) * LOG2_E, Q_BLOCK, axis=2)

        qa, kda, vta, qb, kdb, vtb = _proj(h_lat, h_ctx, 0, n_tok // TOK_TILE, mods[l],
                                           row(g_pre_mix[l]), w_in_b, l, gq, gk, eq, ek,
                                           cos_t, sa_t, sb_t)
        bound_a = (HEAD_DIM * Q_SCALE * BOUND_SLACK * jnp.max(jnp.abs(q_norm[l]))
                   * jnp.max(jnp.abs(k_norm[l]))).reshape(1)
        oa = _attention(qa, kda, vta, None, eh, bound_a, q_blk_off=ctx_qblk, n_qblk=lat_qblk,
                        n_ctx=n_ctx, n_full=n_tok, band=False, q_per_step=Q_PER_STEP_GLOBAL,
                        name="attn_global")
        ob = _attention(qb, kdb, vtb, sink_rows, eh, None, q_blk_off=ctx_qblk, n_qblk=lat_qblk,
                        n_ctx=n_ctx, n_full=n_ctx, band=True, q_per_step=Q_PER_STEP_WINDOW,
                        name="attn_window")
        post = functools.partial(_post, mods=mods[l], layer=l, w_out=w_out_b, g_post_mix=row(g_post_mix[l]),
                                 g_pre_mlp=row(g_pre_mlp[l]), w_up=w_up_b, w_down=w_down_b,
                                 g_post_mlp=row(g_post_mlp[l]))
        if not last:
            oac = _attention(qa, kda, vta, None, eh, bound_a, q_blk_off=0, n_qblk=ctx_qblk, n_ctx=n_ctx,
                             n_full=n_ctx, band=False, q_per_step=ctx_qblk, name="attn_ctx_global")
            obc = _attention(qb, kdb, vtb, sink_rows, eh, None, q_blk_off=0, n_qblk=ctx_qblk, n_ctx=n_ctx,
                             n_full=n_ctx, band=False, q_per_step=ctx_qblk, name="attn_ctx_sink")
            assert h_ctx is not None
            h_lat = post(h_lat, h_ctx, lat_off, n_tok // TOK_TILE, oa, ob, oac, obc)
            h_ctx, lat_off = None, ctx_tiles
        else:
            if h_ctx is not None:
                h_lat = post(h_lat, None, 0, n_lat // TOK_TILE, oa, ob, None, None)
            else:
                h_lat = post(h_lat, None, lat_off, n_lat // TOK_TILE, oa, ob, None, None)
    return h_lat
```

```python
import functools

import jax
import jax.numpy as jnp
from jax import lax
from jax.experimental import pallas as pl
from jax.experimental.pallas import tpu as pltpu

F32 = jnp.float32
BF16 = jnp.bfloat16

D_MODEL = 1024
HEAD_DIM = 64
N_HEADS = 8
N_KV = 2
GROUP = N_HEADS // N_KV
Q_COLS = N_HEADS * HEAD_DIM
KV_COLS = N_KV * HEAD_DIM
IN_COLS = 2 * (Q_COLS + 2 * KV_COLS)
D_FF = 4 * D_MODEL
GRID_W = 64
WINDOW = 128
Q_BLOCK = 128
ROPE_THETA = 10000.0
EPS = 1e-6
NEG_BIG = -1e30
N_MOD = 6
ROPE_SHIFT = HEAD_DIM // 4
LOG2_E = 1.4426950408889634
Q_SCALE = HEAD_DIM ** -0.5 * LOG2_E

LANES = 128
TOK_TILE = 256
BF16_SUBLANES = 16
PROJ_SUBS = (1, 3, 9)
POST_SUBS = (1, 2, 3, 4)
VT_ROWS = HEAD_DIM + BF16_SUBLANES
KEY_CHUNK = 4096
Q_PER_STEP_GLOBAL = 8
Q_PER_STEP_WINDOW = 16
LOGIT_BOUND_LIMIT = 48.0
BOUND_SLACK = 1.01
FF_CHUNK = 1024
ADA_ROWS = 32
ADA_COLS = 1536
VMEM_LIMIT = 56 * 1024 * 1024


def _rms(x, g):
    return x * lax.rsqrt(jnp.mean(x * x, axis=-1, keepdims=True) + EPS) * g


def _ada_kernel(c_ref, w_ref, b_ref, o_ref):
    c = c_ref[...]
    a = c * jax.nn.sigmoid(c)
    w = w_ref[...]
    a_hi = a.astype(BF16)
    a_lo = (a - a_hi.astype(F32)).astype(BF16)
    w_hi = w.astype(BF16)
    w_lo = (w - w_hi.astype(F32)).astype(BF16)
    acc = jnp.dot(a_hi, w_hi, preferred_element_type=F32)
    acc += jnp.dot(a_lo, w_hi, preferred_element_type=F32)
    acc += jnp.dot(a_hi, w_lo, preferred_element_type=F32)
    o_ref[...] = acc + b_ref[...]


def _ada(cc, w_ada, b_ada):
    depth = w_ada.shape[0]
    n_out = w_ada.shape[2]
    return pl.pallas_call(
        _ada_kernel,
        grid=(depth, n_out // ADA_COLS),
        in_specs=[
            pl.BlockSpec((ADA_ROWS, D_MODEL), lambda l, j: (0, 0)),
            pl.BlockSpec((None, D_MODEL, ADA_COLS), lambda l, j: (l, 0, j)),
            pl.BlockSpec((None, 1, ADA_COLS), lambda l, j: (l, 0, j)),
        ],
        out_specs=pl.BlockSpec((None, ADA_ROWS, ADA_COLS), lambda l, j: (l, 0, j)),
        out_shape=jax.ShapeDtypeStruct((depth, ADA_ROWS, n_out), F32),
        compiler_params=pltpu.CompilerParams(
            dimension_semantics=("arbitrary", "arbitrary"), vmem_limit_bytes=VMEM_LIMIT),
        name="ada",
    )(cc, w_ada, b_ada.reshape(depth, 1, n_out))


def _stream_specs(n_sub, block_cols, lat_tile_off, has_ctx):
    specs = []
    for i in range(n_sub):
        if has_ctx:
            idx = lambda b, t, i=i: (b, jnp.maximum(n_sub * t + i - 1, 0), 0)
        else:
            idx = lambda b, t, i=i: (b, lat_tile_off + n_sub * t + i, 0)
        specs.append(pl.BlockSpec((None, TOK_TILE, block_cols), idx))
    if has_ctx:
        specs.append(pl.BlockSpec((None, TOK_TILE, block_cols), lambda b, t: (b, 0, 0)))
    return specs


def _sub_tile(refs, ctx_ref, i):
    v = refs[i][...]
    if ctx_ref is not None and i == 0:
        v = jnp.where(pl.program_id(1) == 0, ctx_ref[...], v)
    return v


def _sub_mod(mod_ref, ctx_first, i):
    if ctx_first and i == 0:
        return jnp.where(pl.program_id(1) == 0, mod_ref[0], mod_ref[1])
    return mod_ref[1]


def _proj_kernel(*refs, n_sub, has_ctx):
    h_refs = refs[:n_sub]
    refs = refs[n_sub:]
    ctx_ref = None
    if has_ctx:
        ctx_ref, refs = refs[0], refs[1:]
    (mod_ref, gpre_ref, w_ref, gq_ref, gk_ref, eq_ref, ek_ref, cos_ref, sa_ref, sb_ref,
     qa_ref, kda_ref, vta_ref, qb_ref, kdb_ref, vtb_ref) = refs
    kv = 2 * KV_COLS

    def prenorm(i):
        mod = _sub_mod(mod_ref, True, i)
        x = _sub_tile(h_refs, ctx_ref, i)
        return (_rms(x, gpre_ref[...]) * (1 + mod[1:2, :]) + mod[0:1, :]).astype(BF16)

    def stages(i, u):
        rows = slice(i * TOK_TILE, (i + 1) * TOK_TILE)
        cos = cos_ref[rows, :]
        sa = sa_ref[rows, :]
        sb = sb_ref[rows, :]

        def rope(t):
            up = pltpu.roll(t, LANES - ROPE_SHIFT, 1)
            dn = pltpu.roll(t, ROPE_SHIFT, 1)
            return t * cos + up * sa + dn * sb

        def head_norm(t, e_ref, g_ref):
            ss = jnp.dot((t * t).astype(BF16), e_ref[...], preferred_element_type=F32)
            return t * lax.rsqrt(ss * (1.0 / HEAD_DIM) + EPS) * g_ref[...]

        def store_q(q, qt_ref):
            q = jnp.concatenate([rope(q[:, j * LANES:(j + 1) * LANES]) for j in range(Q_COLS // LANES)],
                                axis=1) * Q_SCALE
            q_t = q.T.astype(BF16)
            for h in range(N_HEADS):
                qt_ref[h, :, rows] = q_t[h * HEAD_DIM:(h + 1) * HEAD_DIM, :]

        def store_k(k, k_ref):
            k_ref[rows, :] = rope(k).astype(BF16)

        def store_vt(v, vt_ref):
            v_t = v.T.astype(BF16)
            ones = jnp.ones((VT_ROWS - HEAD_DIM, TOK_TILE), BF16)
            for h in range(N_KV):
                vt_ref[h, 0:HEAD_DIM, rows] = v_t[h * HEAD_DIM:(h + 1) * HEAD_DIM, :]
                vt_ref[h, HEAD_DIM:VT_ROWS, rows] = ones

        def project(c0, n):
            return lambda: jnp.dot(u, w_ref[:, c0:c0 + n], preferred_element_type=F32)

        def epi_qa(z):
            store_q(head_norm(z, eq_ref, gq_ref), qa_ref)

        def epi_kva(z):
            store_k(head_norm(z[:, 0:KV_COLS], ek_ref, gk_ref), kda_ref)
            store_vt(z[:, KV_COLS:kv], vta_ref)

        def epi_qb(z):
            store_q(z, qb_ref)

        def epi_kvb(z):
            store_k(z[:, 0:KV_COLS], kdb_ref)
            store_vt(z[:, KV_COLS:kv], vtb_ref)

        return [(project(0, Q_COLS), epi_qa), (project(Q_COLS, kv), epi_kva),
                (project(Q_COLS + kv, Q_COLS), epi_qb), (project(2 * Q_COLS + kv, kv), epi_kvb)]

    u_next = prenorm(0)
    pending = []
    for i in range(n_sub):
        u = u_next
        results = []
        for g, (matmul, epilogue) in enumerate(stages(i, u)):
            results.append((epilogue, matmul()))
            if g == 0 and i + 1 < n_sub:
                u_next = prenorm(i + 1)
            if pending:
                epi, z = pending.pop(0)
                epi(z)
        for epi, z in pending:
            epi(z)
        pending = results
    for epi, z in pending:
        epi(z)


def _proj(h_lat, h_ctx, lat_tile_off, n_tiles, mods, g_pre, w_in, layer, gq, gk, eq, ek, cos_t, sa_t,
          sb_t):
    bsz = h_lat.shape[0]
    has_ctx = h_ctx is not None
    n_sub = max(s for s in PROJ_SUBS if n_tiles % s == 0)
    n_tok = n_tiles * TOK_TILE
    rows = n_sub * TOK_TILE
    const = lambda b, t: (0, 0)
    tok = lambda b, t: (b, t, 0)
    q_shape = jax.ShapeDtypeStruct((bsz, N_HEADS, HEAD_DIM, n_tok), BF16)
    kd_shape = jax.ShapeDtypeStruct((bsz, n_tok, KV_COLS), BF16)
    vt_shape = jax.ShapeDtypeStruct((bsz, N_KV, VT_ROWS, n_tok), BF16)
    q_spec = pl.BlockSpec((None, N_HEADS, HEAD_DIM, rows), lambda b, t: (b, 0, 0, t))
    kd_spec = pl.BlockSpec((None, rows, KV_COLS), tok)
    vt_spec = pl.BlockSpec((None, N_KV, VT_ROWS, rows), lambda b, t: (b, 0, 0, t))
    rope_spec = pl.BlockSpec((rows, LANES), lambda b, t: (t, 0))
    h_args = [h_lat] * n_sub + ([h_ctx] if has_ctx else [])
    return pl.pallas_call(
        functools.partial(_proj_kernel, n_sub=n_sub, has_ctx=has_ctx),
        grid=(bsz, n_tiles // n_sub),
        in_specs=_stream_specs(n_sub, D_MODEL, lat_tile_off, has_ctx) + [
            pl.BlockSpec((None, 2, N_MOD, D_MODEL), lambda b, t: (b, 0, 0, 0)),
            pl.BlockSpec((1, D_MODEL), const),
            pl.BlockSpec((None, D_MODEL, IN_COLS), lambda b, t: (layer, 0, 0)),
            pl.BlockSpec((1, Q_COLS), const),
            pl.BlockSpec((1, KV_COLS), const),
            pl.BlockSpec((Q_COLS, Q_COLS), const),
            pl.BlockSpec((KV_COLS, KV_COLS), const),
            rope_spec, rope_spec, rope_spec,
        ],
        out_specs=[q_spec, kd_spec, vt_spec, q_spec, kd_spec, vt_spec],
        out_shape=[q_shape, kd_shape, vt_shape, q_shape, kd_shape, vt_shape],
        compiler_params=pltpu.CompilerParams(
            dimension_semantics=("parallel", "parallel"), vmem_limit_bytes=VMEM_LIMIT),
        name="proj",
    )(*h_args, mods, g_pre, w_in, gq, gk, eq, ek, cos_t, sa_t, sb_t)


def _band_start(blk, n_ctx, n_tok):
    return jnp.minimum(n_ctx + (blk - 1) * Q_BLOCK, n_tok - (Q_BLOCK + 2 * WINDOW))


def _window_bias_variants(n_ctx, n_tok, n_lat_blk):
    n_band = Q_BLOCK + 2 * WINDOW
    kj = jnp.arange(n_band, dtype=jnp.int32)[:, None]
    qr = jnp.arange(GROUP * Q_BLOCK, dtype=jnp.int32)[None, :] % Q_BLOCK
    variants = []
    for blk in (0, 1, n_lat_blk - 1):
        kpos = kj + (_band_start(blk, n_ctx, n_tok) - n_ctx)
        qpos = qr + blk * Q_BLOCK
        valid = (jnp.abs(kpos - qpos) <= WINDOW) & (kpos >= 0)
        variants.append(jnp.where(valid, 0.0, NEG_BIG).astype(F32))
    return jnp.stack(variants)


def _attn_kernel(*refs, n_ctx, n_full, band, use_sink, n_tok, n_lat_blk, q_row0, q_per_step,
                 given_bound):
    refs = list(refs)
    qt_ref, k_ref, vt_ref, eh_ref = refs[:4]
    rest = refs[4:]
    sink_ref = rest.pop(0) if use_sink else None
    bias_ref = rest.pop(0) if band else None
    if given_bound:
        bound_ref, o_ref = rest
    else:
        o_ref, bound_ref = rest
    nq = GROUP * Q_BLOCK
    n_band = Q_BLOCK + 2 * WINDOW if band else 0
    chunks = [(c0, min(c0 + KEY_CHUNK, n_full), False) for c0 in range(0, n_full, KEY_CHUNK)]
    if band:
        chunks.append((n_full, n_full + n_band, True))
    kv_head = pl.program_id(1)
    step = pl.program_id(2)

    if not given_bound:
        @pl.when(step == 0)
        def _():
            q = qt_ref[...].astype(F32)
            qq = jnp.sum(q * q, axis=1)
            k = k_ref[...]
            kk = jnp.dot(k * k, eh_ref[...], preferred_element_type=F32)
            bound_ref[0] = jnp.sqrt(jnp.max(qq) * jnp.max(kk)) * BOUND_SLACK

    def band_start(blk):
        return pl.multiple_of(_band_start(blk, n_ctx, n_tok), LANES)

    def extend_q(blk):
        cols = pl.ds(pl.multiple_of(q_row0 + blk * Q_BLOCK, Q_BLOCK), Q_BLOCK)
        top = jnp.concatenate([qt_ref[g, :, cols] for g in range(GROUP)], axis=1)
        zero = jnp.zeros_like(top)
        return jnp.where(kv_head == 0, jnp.concatenate([top, zero], axis=0),
                         jnp.concatenate([zero, top], axis=0))

    def logits(qm, blk, chunk):
        c0, c1, in_band = chunk
        if in_band:
            variant = jnp.where(blk == 0, 0, jnp.where(blk >= n_lat_blk - 1, 2, 1))
            return jnp.dot(k_ref[pl.ds(band_start(blk), n_band), :], qm,
                           preferred_element_type=F32) + bias_ref[variant]
        return jnp.dot(k_ref[c0:c1, :], qm, preferred_element_type=F32)

    def values_t(blk, chunk):
        c0, c1, in_band = chunk
        if in_band:
            return vt_ref[:, pl.ds(band_start(blk), n_band)]
        return vt_ref[:, c0:c1]

    def column_max(qm, blk):
        mx = None
        for chunk in chunks:
            s = logits(qm, blk, chunk)
            cm = jnp.max(s.reshape(s.shape[0] // 8, 8, nq), axis=0)
            mx = cm if mx is None else jnp.maximum(mx, cm)
        return jnp.max(mx, axis=0, keepdims=True)

    def finalize(t, acc, m):
        l = acc[HEAD_DIM:HEAD_DIM + 1, :]
        if use_sink:
            l = l + jnp.exp2(sink_ref[...] - m)
        o_t = acc[0:HEAD_DIM, :] / l
        o_t = jnp.concatenate([o_t[:, g * Q_BLOCK:(g + 1) * Q_BLOCK] for g in range(GROUP)], axis=0)
        o_ref[t * Q_BLOCK:(t + 1) * Q_BLOCK, :] = o_t.T.astype(BF16)

    def attend_all(qms, shifts):
        items = [(t, c) for t in range(q_per_step) for c in range(len(chunks))]
        blk_of = lambda t: step * q_per_step + t
        s_next = logits(qms[0], blk_of(0), chunks[0])
        acc = [None] * q_per_step
        pv_prev = None

        def retire(prev):
            t, c, pv = prev
            acc[t] = pv if acc[t] is None else acc[t] + pv
            if c == len(chunks) - 1:
                finalize(t, acc[t], shifts[t])

        for n, (t, c) in enumerate(items):
            s = s_next
            if n + 1 < len(items):
                t2, c2 = items[n + 1]
                s_next = logits(qms[t2], blk_of(t2), chunks[c2])
            p = jnp.exp2(s - shifts[t]).astype(BF16)
            pv = jnp.dot(values_t(blk_of(t), chunks[c]), p, preferred_element_type=F32)
            if pv_prev is not None:
                retire(pv_prev)
            pv_prev = (t, c, pv)
        retire(pv_prev)

    def with_sink(m):
        return jnp.maximum(m, sink_ref[...]) if use_sink else m

    bound = bound_ref[0]
    one_pass = bound <= LOGIT_BOUND_LIMIT

    @pl.when(one_pass)
    def _():
        m = with_sink(jnp.full((1, nq), bound, F32))
        qms = [extend_q(step * q_per_step + t) for t in range(q_per_step)]
        attend_all(qms, [m] * q_per_step)

    @pl.when(jnp.logical_not(one_pass))
    def _():
        qms = [extend_q(step * q_per_step + t) for t in range(q_per_step)]
        shifts = [with_sink(column_max(qms[t], step * q_per_step + t)) for t in range(q_per_step)]
        attend_all(qms, shifts)


def _attention(q, kd, vt, sink_rows, head_ones, logit_bound, *, q_blk_off, n_qblk, n_ctx, n_full, band,
               q_per_step, name):
    bsz, n_tok, _ = kd.shape
    q_per_step = min(q_per_step, n_qblk)
    n_rows = n_tok if (band or n_full == n_tok) else max(n_full, (q_blk_off + n_qblk) * Q_BLOCK)
    assert n_qblk % q_per_step == 0
    use_sink = sink_rows is not None
    nq = GROUP * Q_BLOCK
    in_specs = [
        pl.BlockSpec((None, GROUP, HEAD_DIM, n_rows), lambda b, h, j: (b, h, 0, 0)),
        pl.BlockSpec((None, n_rows, KV_COLS), lambda b, h, j: (b, 0, 0)),
        pl.BlockSpec((None, None, VT_ROWS, n_rows), lambda b, h, j: (b, h, 0, 0)),
        pl.BlockSpec(head_ones.shape, lambda b, h, j: (0, 0)),
    ]
    args = [q, kd, vt, head_ones]
    if use_sink:
        in_specs.append(pl.BlockSpec((None, 1, nq), lambda b, h, j: (h, 0, 0)))
        args.append(sink_rows)
    if band:
        bias = _window_bias_variants(n_ctx, n_tok, n_qblk)
        in_specs.append(pl.BlockSpec(bias.shape, lambda b, h, j: (0, 0, 0)))
        args.append(bias)
    given_bound = logit_bound is not None
    if given_bound:
        in_specs.append(pl.BlockSpec(memory_space=pltpu.SMEM))
        args.append(logit_bound)
    kern = functools.partial(_attn_kernel, n_ctx=n_ctx, n_full=n_full, band=band, use_sink=use_sink,
                             n_tok=n_tok, n_lat_blk=n_qblk, q_row0=q_blk_off * Q_BLOCK,
                             q_per_step=q_per_step, given_bound=given_bound)
    rows = q_per_step * Q_BLOCK
    return pl.pallas_call(
        kern,
        grid=(bsz, N_KV, n_qblk // q_per_step),
        in_specs=in_specs,
        out_specs=pl.BlockSpec((None, rows, GROUP * HEAD_DIM), lambda b, h, j: (b, j, h)),
        out_shape=jax.ShapeDtypeStruct((bsz, n_qblk * Q_BLOCK, Q_COLS), BF16),
        scratch_shapes=[] if given_bound else [pltpu.SMEM((1,), F32)],
        compiler_params=pltpu.CompilerParams(
            dimension_semantics=("parallel", "parallel", "arbitrary"), vmem_limit_bytes=VMEM_LIMIT),
        name=name,
    )(*args)


def _post_kernel(*refs, n_sub, has_ctx):
    n_in = n_sub + (1 if has_ctx else 0)
    h_refs, ma_refs, mb_refs = refs[:n_sub], refs[n_in:n_in + n_sub], refs[2 * n_in:2 * n_in + n_sub]
    ctx_h = refs[n_sub] if has_ctx else None
    ctx_ma = refs[n_in + n_sub] if has_ctx else None
    ctx_mb = refs[2 * n_in + n_sub] if has_ctx else None
    mod_ref, wout_ref, gpm_ref, gpre_ref, wup_ref, wdn_ref, gpl_ref, o_ref = refs[3 * n_in:]
    mods = [_sub_mod(mod_ref, has_ctx, i) for i in range(n_sub)]

    def out_proj(i):
        mix = jnp.concatenate([_sub_tile(ma_refs, ctx_ma, i), _sub_tile(mb_refs, ctx_mb, i)], axis=1)
        return jnp.dot(mix, wout_ref[...], preferred_element_type=F32)

    def mixer_residual(i, mix):
        mod = mods[i]
        h = _sub_tile(h_refs, ctx_h, i) + mod[2:3, :] * _rms(mix, gpm_ref[...])
        u = (_rms(h, gpre_ref[...]) * (1 + mod[4:5, :]) + mod[3:4, :]).astype(BF16)
        return h, u

    def mlp_chunk(u, y, c0):
        a = jnp.dot(u, wup_ref[:, c0:c0 + FF_CHUNK], preferred_element_type=F32)
        a = jnp.square(jnp.maximum(a, 0.0)).astype(BF16)
        part = jnp.dot(a, wdn_ref[c0:c0 + FF_CHUNK, :], preferred_element_type=F32)
        return part if y is None else y + part

    def mlp_residual(i, h, y):
        o_ref[i * TOK_TILE:(i + 1) * TOK_TILE, :] = h + mods[i][5:6, :] * _rms(y, gpl_ref[...])

    hu = [None] * n_sub
    mix_prev = None
    for i in range(n_sub):
        mix = out_proj(i)
        if i > 0:
            hu[i - 1] = mixer_residual(i - 1, mix_prev)
        mix_prev = mix
    for i in range(n_sub):
        y = None
        for c, c0 in enumerate(range(0, D_FF, FF_CHUNK)):
            if i == 0 and c == 0:
                hu[n_sub - 1] = mixer_residual(n_sub - 1, mix_prev)
            y = mlp_chunk(hu[i][1], y, c0)
            if i > 0 and c == 0:
                mlp_residual(i - 1, hu[i - 1][0], y_prev)
        y_prev = y
    mlp_residual(n_sub - 1, hu[n_sub - 1][0], y_prev)


def _post(h_lat, h_ctx, lat_tile_off, n_tiles, mix_a, mix_b, ctx_mix_a, ctx_mix_b, mods, layer, w_out,
          g_post_mix, g_pre_mlp, w_up, w_down, g_post_mlp):
    bsz = h_lat.shape[0]
    has_ctx = h_ctx is not None
    n_sub = max(s for s in POST_SUBS if n_tiles % s == 0)
    const = lambda b, t: (0, 0)
    resident = lambda rows, cols: pl.BlockSpec((None, rows, cols), lambda b, t: (layer, 0, 0),
                                               pipeline_mode=pl.Buffered(1))
    ctx_args = lambda v: [v] if has_ctx else []
    args = ([h_lat] * n_sub + ctx_args(h_ctx) + [mix_a] * n_sub + ctx_args(ctx_mix_a)
            + [mix_b] * n_sub + ctx_args(ctx_mix_b))
    return pl.pallas_call(
        functools.partial(_post_kernel, n_sub=n_sub, has_ctx=has_ctx),
        grid=(bsz, n_tiles // n_sub),
        in_specs=(_stream_specs(n_sub, D_MODEL, lat_tile_off, has_ctx)
                  + _stream_specs(n_sub, Q_COLS, 0, has_ctx)
                  + _stream_specs(n_sub, Q_COLS, 0, has_ctx) + [
            pl.BlockSpec((None, 2, N_MOD, D_MODEL), lambda b, t: (b, 0, 0, 0)),
            resident(D_MODEL, D_MODEL),
            pl.BlockSpec((1, D_MODEL), const),
            pl.BlockSpec((1, D_MODEL), const),
            resident(D_MODEL, D_FF),
            resident(D_FF, D_MODEL),
            pl.BlockSpec((1, D_MODEL), const),
        ]),
        out_specs=pl.BlockSpec((None, n_sub * TOK_TILE, D_MODEL), lambda b, t: (b, t, 0)),
        out_shape=jax.ShapeDtypeStruct((bsz, n_tiles * TOK_TILE, D_MODEL), F32),
        compiler_params=pltpu.CompilerParams(
            dimension_semantics=("parallel", "parallel"), vmem_limit_bytes=VMEM_LIMIT),
        name="post",
    )(*args, mods, w_out, g_post_mix, g_pre_mlp, w_up, w_down, g_post_mlp)


def _rope_tables(n_ctx, n_lat):
    rows = n_lat // GRID_W
    row_ids = jnp.repeat(jnp.arange(rows, dtype=jnp.int32), GRID_W).astype(F32)
    col_ids = jnp.tile(jnp.arange(GRID_W, dtype=jnp.int32), rows).astype(F32)
    axis_dim = HEAD_DIM // 2
    inv = ROPE_THETA ** (-jnp.arange(0, axis_dim, 2, dtype=F32) / axis_dim)
    ang_r = row_ids[:, None] * inv[None, :]
    ang_c = col_ids[:, None] * inv[None, :]
    cos_r, sin_r, cos_c, sin_c = jnp.cos(ang_r), jnp.sin(ang_r), jnp.cos(ang_c), jnp.sin(ang_c)
    zero = jnp.zeros_like(sin_r)
    cos_h = jnp.concatenate([cos_r, cos_r, cos_c, cos_c], axis=-1)
    sa_h = jnp.concatenate([-sin_r, zero, -sin_c, zero], axis=-1)
    sb_h = jnp.concatenate([zero, sin_r, zero, sin_c], axis=-1)

    def full(t, ctx_value):
        t = jnp.tile(t, (1, LANES // HEAD_DIM))
        return jnp.concatenate([jnp.full((n_ctx, LANES), ctx_value, F32), t], axis=0)

    return full(cos_h, 1.0), full(sa_h, 0.0), full(sb_h, 0.0)


def _head_ones(n_cols):
    head = jnp.arange(n_cols, dtype=jnp.int32) // HEAD_DIM
    return (head[:, None] == head[None, :]).astype(BF16)


def kernel(x, c, ctx, c_ctx, w_ada, b_ada, g_pre_mix, g_post_mix, g_pre_mlp, g_post_mlp,
           w_in, q_norm, k_norm, sink, w_out, w_up, w_down):
    bsz, n_lat, d = x.shape
    n_ctx = ctx.shape[1]
    depth = w_in.shape[0]
    n_tok = n_ctx + n_lat
    assert d == D_MODEL and n_ctx == TOK_TILE and n_lat % TOK_TILE == 0 and bsz + 1 <= ADA_ROWS
    assert w_in.shape[2] == IN_COLS and w_up.shape[2] == D_FF

    cc = jnp.concatenate([c, c_ctx[None, :], jnp.zeros((ADA_ROWS - bsz - 1, d), F32)], axis=0)
    mod_all = _ada(cc, w_ada, b_ada)
    mod_all = mod_all.reshape(depth, ADA_ROWS, N_MOD, d)
    mod_ctx = jnp.broadcast_to(mod_all[:, bsz:bsz + 1], (depth, bsz, N_MOD, d))
    mods = jnp.stack([mod_ctx, mod_all[:, :bsz]], axis=2)

    cos_t, sa_t, sb_t = _rope_tables(n_ctx, n_lat)
    eq = _head_ones(Q_COLS)
    ek = _head_ones(KV_COLS)
    eh = _head_ones(KV_COLS)
    row = lambda v: v.reshape(1, -1)

    w_in_b = w_in.astype(BF16)
    w_out_b = w_out.astype(BF16)
    w_up_b = w_up.astype(BF16)
    w_down_b = w_down.astype(BF16)

    ctx_tiles = n_ctx // TOK_TILE
    assert depth >= 1 and ctx_tiles == 1
    ctx_qblk = n_ctx // Q_BLOCK
    lat_qblk = n_lat // Q_BLOCK

    h_lat, h_ctx = x, ctx
    lat_off = 0
    for l in range(depth):
        last = l == depth - 1
        gq = row(jnp.tile(q_norm[l], N_HEADS))
        gk = row(jnp.tile(k_norm[l], N_KV))
        sink_rows = jnp.repeat(sink[l].reshape(N_KV, 1, GROUP) * LOG2_E, Q_BLOCK, axis=2)

        qa, kda, vta, qb, kdb, vtb = _proj(h_lat, h_ctx, 0, n_tok // TOK_TILE, mods[l],
                                           row(g_pre_mix[l]), w_in_b, l, gq, gk, eq, ek,
                                           cos_t, sa_t, sb_t)
        bound_a = (HEAD_DIM * Q_SCALE * BOUND_SLACK * jnp.max(jnp.abs(q_norm[l]))
                   * jnp.max(jnp.abs(k_norm[l]))).reshape(1)
        oa = _attention(qa, kda, vta, None, eh, bound_a, q_blk_off=ctx_qblk, n_qblk=lat_qblk,
                        n_ctx=n_ctx, n_full=n_tok, band=False, q_per_step=Q_PER_STEP_GLOBAL,
                        name="attn_global")
        ob = _attention(qb, kdb, vtb, sink_rows, eh, None, q_blk_off=ctx_qblk, n_qblk=lat_qblk,
                        n_ctx=n_ctx, n_full=n_ctx, band=True, q_per_step=Q_PER_STEP_WINDOW,
                        name="attn_window")
        post = functools.partial(_post, mods=mods[l], layer=l, w_out=w_out_b, g_post_mix=row(g_post_mix[l]),
                                 g_pre_mlp=row(g_pre_mlp[l]), w_up=w_up_b, w_down=w_down_b,
                                 g_post_mlp=row(g_post_mlp[l]))
        if not last:
            oac = _attention(qa, kda, vta, None, eh, bound_a, q_blk_off=0, n_qblk=ctx_qblk, n_ctx=n_ctx,
                             n_full=n_ctx, band=False, q_per_step=ctx_qblk, name="attn_ctx_global")
            obc = _attention(qb, kdb, vtb, sink_rows, eh, None, q_blk_off=0, n_qblk=ctx_qblk, n_ctx=n_ctx,
                             n_full=n_ctx, band=False, q_per_step=ctx_qblk, name="attn_ctx_sink")
            assert h_ctx is not None
            h_lat = post(h_lat, h_ctx, lat_off, n_tok // TOK_TILE, oa, ob, oac, obc)
            h_ctx, lat_off = None, ctx_tiles
        else:
            if h_ctx is not None:
                h_lat = post(h_lat, None, 0, n_lat // TOK_TILE, oa, ob, None, None)
            else:
                h_lat = post(h_lat, None, lat_off, n_lat // TOK_TILE, oa, ob, None, None)
    return h_lat
```

```python
import functools

import jax
import jax.numpy as jnp
from jax import lax
from jax.experimental import pallas as pl
from jax.experimental.pallas import tpu as pltpu

F32 = jnp.float32
BF16 = jnp.bfloat16

D_MODEL = 1024
HEAD_DIM = 64
N_HEADS = 8
N_KV = 2
GROUP = N_HEADS // N_KV
Q_COLS = N_HEADS * HEAD_DIM
KV_COLS = N_KV * HEAD_DIM
IN_COLS = 2 * (Q_COLS + 2 * KV_COLS)
D_FF = 4 * D_MODEL
GRID_W = 64
WINDOW = 128
Q_BLOCK = 128
ROPE_THETA = 10000.0
EPS = 1e-6
NEG_BIG = -1e30
N_MOD = 6
ROPE_SHIFT = HEAD_DIM // 4
LOG2_E = 1.4426950408889634
Q_SCALE = HEAD_DIM ** -0.5 * LOG2_E

LANES = 128
TOK_TILE = 256
BF16_SUBLANES = 16
PROJ_SUBS = (1, 3, 9)
POST_SUBS = (1, 2, 3, 4)
VT_ROWS = HEAD_DIM + BF16_SUBLANES
KEY_CHUNK = 4096
Q_PER_STEP_GLOBAL = 16
Q_PER_STEP_WINDOW = 16
LOGIT_BOUND_LIMIT = 48.0
BOUND_SLACK = 1.01
FF_CHUNK = 1024
ADA_ROWS = 32
ADA_COLS = 1536
VMEM_LIMIT = 56 * 1024 * 1024


def _rms(x, g):
    return x * lax.rsqrt(jnp.mean(x * x, axis=-1, keepdims=True) + EPS) * g


def _ada_kernel(c_ref, w_ref, b_ref, o_ref):
    c = c_ref[...]
    a = c * jax.nn.sigmoid(c)
    w = w_ref[...]
    a_hi = a.astype(BF16)
    a_lo = (a - a_hi.astype(F32)).astype(BF16)
    w_hi = w.astype(BF16)
    w_lo = (w - w_hi.astype(F32)).astype(BF16)
    acc = jnp.dot(a_hi, w_hi, preferred_element_type=F32)
    acc += jnp.dot(a_lo, w_hi, preferred_element_type=F32)
    acc += jnp.dot(a_hi, w_lo, preferred_element_type=F32)
    o_ref[...] = acc + b_ref[...]


def _ada(cc, w_ada, b_ada):
    depth = w_ada.shape[0]
    n_out = w_ada.shape[2]
    return pl.pallas_call(
        _ada_kernel,
        grid=(depth, n_out // ADA_COLS),
        in_specs=[
            pl.BlockSpec((ADA_ROWS, D_MODEL), lambda l, j: (0, 0)),
            pl.BlockSpec((None, D_MODEL, ADA_COLS), lambda l, j: (l, 0, j)),
            pl.BlockSpec((None, 1, ADA_COLS), lambda l, j: (l, 0, j)),
        ],
        out_specs=pl.BlockSpec((None, ADA_ROWS, ADA_COLS), lambda l, j: (l, 0, j)),
        out_shape=jax.ShapeDtypeStruct((depth, ADA_ROWS, n_out), F32),
        compiler_params=pltpu.CompilerParams(
            dimension_semantics=("arbitrary", "arbitrary"), vmem_limit_bytes=VMEM_LIMIT),
        name="ada",
    )(cc, w_ada, b_ada.reshape(depth, 1, n_out))


def _stream_specs(n_sub, block_cols, lat_tile_off, has_ctx):
    specs = []
    for i in range(n_sub):
        if has_ctx:
            idx = lambda b, t, i=i: (b, jnp.maximum(n_sub * t + i - 1, 0), 0)
        else:
            idx = lambda b, t, i=i: (b, lat_tile_off + n_sub * t + i, 0)
        specs.append(pl.BlockSpec((None, TOK_TILE, block_cols), idx))
    if has_ctx:
        specs.append(pl.BlockSpec((None, TOK_TILE, block_cols), lambda b, t: (b, 0, 0)))
    return specs


def _sub_tile(refs, ctx_ref, i):
    v = refs[i][...]
    if ctx_ref is not None and i == 0:
        v = jnp.where(pl.program_id(1) == 0, ctx_ref[...], v)
    return v


def _sub_mod(mod_ref, ctx_first, i):
    if ctx_first and i == 0:
        return jnp.where(pl.program_id(1) == 0, mod_ref[0], mod_ref[1])
    return mod_ref[1]


def _proj_kernel(*refs, n_sub, has_ctx):
    h_refs = refs[:n_sub]
    refs = refs[n_sub:]
    ctx_ref = None
    if has_ctx:
        ctx_ref, refs = refs[0], refs[1:]
    (mod_ref, gpre_ref, w_ref, gq_ref, gk_ref, eq_ref, ek_ref, cos_ref, sa_ref, sb_ref,
     qa_ref, kda_ref, vta_ref, qb_ref, kdb_ref, vtb_ref) = refs
    kv = 2 * KV_COLS

    def prenorm(i):
        mod = _sub_mod(mod_ref, True, i)
        x = _sub_tile(h_refs, ctx_ref, i)
        return (_rms(x, gpre_ref[...]) * (1 + mod[1:2, :]) + mod[0:1, :]).astype(BF16)

    def stages(i, u):
        rows = slice(i * TOK_TILE, (i + 1) * TOK_TILE)
        cos = cos_ref[rows, :]
        sa = sa_ref[rows, :]
        sb = sb_ref[rows, :]

        def rope(t):
            up = pltpu.roll(t, LANES - ROPE_SHIFT, 1)
            dn = pltpu.roll(t, ROPE_SHIFT, 1)
            return t * cos + up * sa + dn * sb

        def head_norm(t, e_ref, g_ref):
            ss = jnp.dot((t * t).astype(BF16), e_ref[...], preferred_element_type=F32)
            return t * lax.rsqrt(ss * (1.0 / HEAD_DIM) + EPS) * g_ref[...]

        def store_q(q, qt_ref):
            q = jnp.concatenate([rope(q[:, j * LANES:(j + 1) * LANES]) for j in range(Q_COLS // LANES)],
                                axis=1) * Q_SCALE
            q_t = q.T.astype(BF16)
            for h in range(N_HEADS):
                qt_ref[h, :, rows] = q_t[h * HEAD_DIM:(h + 1) * HEAD_DIM, :]

        def store_k(k, k_ref):
            k_ref[rows, :] = rope(k).astype(BF16)

        def store_vt(v, vt_ref):
            v_t = v.T.astype(BF16)
            ones = jnp.ones((VT_ROWS - HEAD_DIM, TOK_TILE), BF16)
            for h in range(N_KV):
                vt_ref[h, 0:HEAD_DIM, rows] = v_t[h * HEAD_DIM:(h + 1) * HEAD_DIM, :]
                vt_ref[h, HEAD_DIM:VT_ROWS, rows] = ones

        def project(c0, n):
            return lambda: jnp.dot(u, w_ref[:, c0:c0 + n], preferred_element_type=F32)

        def epi_qa(z):
            store_q(head_norm(z, eq_ref, gq_ref), qa_ref)

        def epi_kva(z):
            store_k(head_norm(z[:, 0:KV_COLS], ek_ref, gk_ref), kda_ref)
            store_vt(z[:, KV_COLS:kv], vta_ref)

        def epi_qb(z):
            store_q(z, qb_ref)

        def epi_kvb(z):
            store_k(z[:, 0:KV_COLS], kdb_ref)
            store_vt(z[:, KV_COLS:kv], vtb_ref)

        return [(project(0, Q_COLS), epi_qa), (project(Q_COLS, kv), epi_kva),
                (project(Q_COLS + kv, Q_COLS), epi_qb), (project(2 * Q_COLS + kv, kv), epi_kvb)]

    u_next = prenorm(0)
    pending = []
    for i in range(n_sub):
        u = u_next
        results = []
        for g, (matmul, epilogue) in enumerate(stages(i, u)):
            results.append((epilogue, matmul()))
            if g == 0 and i + 1 < n_sub:
                u_next = prenorm(i + 1)
            if pending:
                epi, z = pending.pop(0)
                epi(z)
        for epi, z in pending:
            epi(z)
        pending = results
    for epi, z in pending:
        epi(z)


def _proj(h_lat, h_ctx, lat_tile_off, n_tiles, mods, g_pre, w_in, layer, gq, gk, eq, ek, cos_t, sa_t,
          sb_t):
    bsz = h_lat.shape[0]
    has_ctx = h_ctx is not None
    n_sub = max(s for s in PROJ_SUBS if n_tiles % s == 0)
    n_tok = n_tiles * TOK_TILE
    rows = n_sub * TOK_TILE
    const = lambda b, t: (0, 0)
    tok = lambda b, t: (b, t, 0)
    q_shape = jax.ShapeDtypeStruct((bsz, N_HEADS, HEAD_DIM, n_tok), BF16)
    kd_shape = jax.ShapeDtypeStruct((bsz, n_tok, KV_COLS), BF16)
    vt_shape = jax.ShapeDtypeStruct((bsz, N_KV, VT_ROWS, n_tok), BF16)
    q_spec = pl.BlockSpec((None, N_HEADS, HEAD_DIM, rows), lambda b, t: (b, 0, 0, t))
    kd_spec = pl.BlockSpec((None, rows, KV_COLS), tok)
    vt_spec = pl.BlockSpec((None, N_KV, VT_ROWS, rows), lambda b, t: (b, 0, 0, t))
    rope_spec = pl.BlockSpec((rows, LANES), lambda b, t: (t, 0))
    h_args = [h_lat] * n_sub + ([h_ctx] if has_ctx else [])
    return pl.pallas_call(
        functools.partial(_proj_kernel, n_sub=n_sub, has_ctx=has_ctx),
        grid=(bsz, n_tiles // n_sub),
        in_specs=_stream_specs(n_sub, D_MODEL, lat_tile_off, has_ctx) + [
            pl.BlockSpec((None, 2, N_MOD, D_MODEL), lambda b, t: (b, 0, 0, 0)),
            pl.BlockSpec((1, D_MODEL), const),
            pl.BlockSpec((None, D_MODEL, IN_COLS), lambda b, t: (layer, 0, 0)),
            pl.BlockSpec((1, Q_COLS), const),
            pl.BlockSpec((1, KV_COLS), const),
            pl.BlockSpec((Q_COLS, Q_COLS), const),
            pl.BlockSpec((KV_COLS, KV_COLS), const),
            rope_spec, rope_spec, rope_spec,
        ],
        out_specs=[q_spec, kd_spec, vt_spec, q_spec, kd_spec, vt_spec],
        out_shape=[q_shape, kd_shape, vt_shape, q_shape, kd_shape, vt_shape],
        compiler_params=pltpu.CompilerParams(
            dimension_semantics=("parallel", "parallel"), vmem_limit_bytes=VMEM_LIMIT),
        name="proj",
    )(*h_args, mods, g_pre, w_in, gq, gk, eq, ek, cos_t, sa_t, sb_t)


def _band_start(blk, n_ctx, n_tok):
    return jnp.minimum(n_ctx + (blk - 1) * Q_BLOCK, n_tok - (Q_BLOCK + 2 * WINDOW))


def _window_bias_variants(n_ctx, n_tok, n_lat_blk):
    n_band = Q_BLOCK + 2 * WINDOW
    kj = jnp.arange(n_band, dtype=jnp.int32)[:, None]
    qr = jnp.arange(GROUP * Q_BLOCK, dtype=jnp.int32)[None, :] % Q_BLOCK
    variants = []
    for blk in (0, 1, n_lat_blk - 1):
        kpos = kj + (_band_start(blk, n_ctx, n_tok) - n_ctx)
        qpos = qr + blk * Q_BLOCK
        valid = (jnp.abs(kpos - qpos) <= WINDOW) & (kpos >= 0)
        variants.append(jnp.where(valid, 0.0, NEG_BIG).astype(F32))
    return jnp.stack(variants)


def _attn_kernel(*refs, n_ctx, n_full, band, use_sink, n_tok, n_lat_blk, q_row0, q_per_step,
                 given_bound):
    refs = list(refs)
    qt_ref, k_ref, vt_ref, eh_ref = refs[:4]
    rest = refs[4:]
    sink_ref = rest.pop(0) if use_sink else None
    bias_ref = rest.pop(0) if band else None
    if given_bound:
        bound_ref, o_ref = rest
    else:
        o_ref, bound_ref = rest
    nq = GROUP * Q_BLOCK
    n_band = Q_BLOCK + 2 * WINDOW if band else 0
    chunks = [(c0, min(c0 + KEY_CHUNK, n_full), False) for c0 in range(0, n_full, KEY_CHUNK)]
    if band:
        chunks.append((n_full, n_full + n_band, True))
    kv_head = pl.program_id(1)
    step = pl.program_id(2)

    if not given_bound:
        @pl.when(step == 0)
        def _():
            q = qt_ref[...].astype(F32)
            qq = jnp.sum(q * q, axis=1)
            k = k_ref[...]
            kk = jnp.dot(k * k, eh_ref[...], preferred_element_type=F32)
            bound_ref[0] = jnp.sqrt(jnp.max(qq) * jnp.max(kk)) * BOUND_SLACK

    def band_start(blk):
        return pl.multiple_of(_band_start(blk, n_ctx, n_tok), LANES)

    def extend_q(blk):
        cols = pl.ds(pl.multiple_of(q_row0 + blk * Q_BLOCK, Q_BLOCK), Q_BLOCK)
        top = jnp.concatenate([qt_ref[g, :, cols] for g in range(GROUP)], axis=1)
        zero = jnp.zeros_like(top)
        return jnp.where(kv_head == 0, jnp.concatenate([top, zero], axis=0),
                         jnp.concatenate([zero, top], axis=0))

    def logits(qm, blk, chunk):
        c0, c1, in_band = chunk
        if in_band:
            variant = jnp.where(blk == 0, 0, jnp.where(blk >= n_lat_blk - 1, 2, 1))
            return jnp.dot(k_ref[pl.ds(band_start(blk), n_band), :], qm,
                           preferred_element_type=F32) + bias_ref[variant]
        return jnp.dot(k_ref[c0:c1, :], qm, preferred_element_type=F32)

    def values_t(blk, chunk):
        c0, c1, in_band = chunk
        if in_band:
            return vt_ref[:, pl.ds(band_start(blk), n_band)]
        return vt_ref[:, c0:c1]

    def column_max(qm, blk):
        mx = None
        for chunk in chunks:
            s = logits(qm, blk, chunk)
            cm = jnp.max(s.reshape(s.shape[0] // 8, 8, nq), axis=0)
            mx = cm if mx is None else jnp.maximum(mx, cm)
        return jnp.max(mx, axis=0, keepdims=True)

    def finalize(t, acc, m):
        l = acc[HEAD_DIM:HEAD_DIM + 1, :]
        if use_sink:
            l = l + jnp.exp2(sink_ref[...] - m)
        o_t = acc[0:HEAD_DIM, :] / l
        o_t = jnp.concatenate([o_t[:, g * Q_BLOCK:(g + 1) * Q_BLOCK] for g in range(GROUP)], axis=0)
        o_ref[t * Q_BLOCK:(t + 1) * Q_BLOCK, :] = o_t.T.astype(BF16)

    def attend_all(qms, shifts):
        items = [(t, c) for t in range(q_per_step) for c in range(len(chunks))]
        blk_of = lambda t: step * q_per_step + t
        s_next = logits(qms[0], blk_of(0), chunks[0])
        acc = [None] * q_per_step
        pv_prev = None

        def retire(prev):
            t, c, pv = prev
            acc[t] = pv if acc[t] is None else acc[t] + pv
            if c == len(chunks) - 1:
                finalize(t, acc[t], shifts[t])

        for n, (t, c) in enumerate(items):
            s = s_next
            if n + 1 < len(items):
                t2, c2 = items[n + 1]
                s_next = logits(qms[t2], blk_of(t2), chunks[c2])
            p = jnp.exp2(s - shifts[t]).astype(BF16)
            pv = jnp.dot(values_t(blk_of(t), chunks[c]), p, preferred_element_type=F32)
            if pv_prev is not None:
                retire(pv_prev)
            pv_prev = (t, c, pv)
        retire(pv_prev)

    def with_sink(m):
        return jnp.maximum(m, sink_ref[...]) if use_sink else m

    bound = bound_ref[0]
    one_pass = bound <= LOGIT_BOUND_LIMIT

    @pl.when(one_pass)
    def _():
        m = with_sink(jnp.full((1, nq), bound, F32))
        qms = [extend_q(step * q_per_step + t) for t in range(q_per_step)]
        attend_all(qms, [m] * q_per_step)

    @pl.when(jnp.logical_not(one_pass))
    def _():
        qms = [extend_q(step * q_per_step + t) for t in range(q_per_step)]
        shifts = [with_sink(column_max(qms[t], step * q_per_step + t)) for t in range(q_per_step)]
        attend_all(qms, shifts)


def _attention(q, kd, vt, sink_rows, head_ones, logit_bound, *, q_blk_off, n_qblk, n_ctx, n_full, band,
               q_per_step, name):
    bsz, n_tok, _ = kd.shape
    q_per_step = min(q_per_step, n_qblk)
    n_rows = n_tok if (band or n_full == n_tok) else max(n_full, (q_blk_off + n_qblk) * Q_BLOCK)
    assert n_qblk % q_per_step == 0
    use_sink = sink_rows is not None
    nq = GROUP * Q_BLOCK
    in_specs = [
        pl.BlockSpec((None, GROUP, HEAD_DIM, n_rows), lambda b, h, j: (b, h, 0, 0)),
        pl.BlockSpec((None, n_rows, KV_COLS), lambda b, h, j: (b, 0, 0)),
        pl.BlockSpec((None, None, VT_ROWS, n_rows), lambda b, h, j: (b, h, 0, 0)),
        pl.BlockSpec(head_ones.shape, lambda b, h, j: (0, 0)),
    ]
    args = [q, kd, vt, head_ones]
    if use_sink:
        in_specs.append(pl.BlockSpec((None, 1, nq), lambda b, h, j: (h, 0, 0)))
        args.append(sink_rows)
    if band:
        bias = _window_bias_variants(n_ctx, n_tok, n_qblk)
        in_specs.append(pl.BlockSpec(bias.shape, lambda b, h, j: (0, 0, 0)))
        args.append(bias)
    given_bound = logit_bound is not None
    if given_bound:
        in_specs.append(pl.BlockSpec(memory_space=pltpu.SMEM))
        args.append(logit_bound)
    kern = functools.partial(_attn_kernel, n_ctx=n_ctx, n_full=n_full, band=band, use_sink=use_sink,
                             n_tok=n_tok, n_lat_blk=n_qblk, q_row0=q_blk_off * Q_BLOCK,
                             q_per_step=q_per_step, given_bound=given_bound)
    rows = q_per_step * Q_BLOCK
    return pl.pallas_call(
        kern,
        grid=(bsz, N_KV, n_qblk // q_per_step),
        in_specs=in_specs,
        out_specs=pl.BlockSpec((None, rows, GROUP * HEAD_DIM), lambda b, h, j: (b, j, h)),
        out_shape=jax.ShapeDtypeStruct((bsz, n_qblk * Q_BLOCK, Q_COLS), BF16),
        scratch_shapes=[] if given_bound else [pltpu.SMEM((1,), F32)],
        compiler_params=pltpu.CompilerParams(
            dimension_semantics=("parallel", "parallel", "arbitrary"), vmem_limit_bytes=VMEM_LIMIT),
        name=name,
    )(*args)


def _post_kernel(*refs, n_sub, has_ctx):
    n_in = n_sub + (1 if has_ctx else 0)
    h_refs, ma_refs, mb_refs = refs[:n_sub], refs[n_in:n_in + n_sub], refs[2 * n_in:2 * n_in + n_sub]
    ctx_h = refs[n_sub] if has_ctx else None
    ctx_ma = refs[n_in + n_sub] if has_ctx else None
    ctx_mb = refs[2 * n_in + n_sub] if has_ctx else None
    mod_ref, wout_ref, gpm_ref, gpre_ref, wup_ref, wdn_ref, gpl_ref, o_ref = refs[3 * n_in:]
    mods = [_sub_mod(mod_ref, has_ctx, i) for i in range(n_sub)]

    def out_proj(i):
        mix = jnp.concatenate([_sub_tile(ma_refs, ctx_ma, i), _sub_tile(mb_refs, ctx_mb, i)], axis=1)
        return jnp.dot(mix, wout_ref[...], preferred_element_type=F32)

    def mixer_residual(i, mix):
        mod = mods[i]
        h = _sub_tile(h_refs, ctx_h, i) + mod[2:3, :] * _rms(mix, gpm_ref[...])
        u = (_rms(h, gpre_ref[...]) * (1 + mod[4:5, :]) + mod[3:4, :]).astype(BF16)
        return h, u

    def mlp_chunk(u, y, c0):
        a = jnp.dot(u, wup_ref[:, c0:c0 + FF_CHUNK], preferred_element_type=F32)
        a = jnp.square(jnp.maximum(a, 0.0)).astype(BF16)
        part = jnp.dot(a, wdn_ref[c0:c0 + FF_CHUNK, :], preferred_element_type=F32)
        return part if y is None else y + part

    def mlp_residual(i, h, y):
        o_ref[i * TOK_TILE:(i + 1) * TOK_TILE, :] = h + mods[i][5:6, :] * _rms(y, gpl_ref[...])

    hu = [None] * n_sub
    mix_prev = None
    for i in range(n_sub):
        mix = out_proj(i)
        if i > 0:
            hu[i - 1] = mixer_residual(i - 1, mix_prev)
        mix_prev = mix
    for i in range(n_sub):
        y = None
        for c, c0 in enumerate(range(0, D_FF, FF_CHUNK)):
            if i == 0 and c == 0:
                hu[n_sub - 1] = mixer_residual(n_sub - 1, mix_prev)
            y = mlp_chunk(hu[i][1], y, c0)
            if i > 0 and c == 0:
                mlp_residual(i - 1, hu[i - 1][0], y_prev)
        y_prev = y
    mlp_residual(n_sub - 1, hu[n_sub - 1][0], y_prev)


def _post(h_lat, h_ctx, lat_tile_off, n_tiles, mix_a, mix_b, ctx_mix_a, ctx_mix_b, mods, layer, w_out,
          g_post_mix, g_pre_mlp, w_up, w_down, g_post_mlp):
    bsz = h_lat.shape[0]
    has_ctx = h_ctx is not None
    n_sub = max(s for s in POST_SUBS if n_tiles % s == 0)
    const = lambda b, t: (0, 0)
    resident = lambda rows, cols: pl.BlockSpec((None, rows, cols), lambda b, t: (layer, 0, 0),
                                               pipeline_mode=pl.Buffered(1))
    ctx_args = lambda v: [v] if has_ctx else []
    args = ([h_lat] * n_sub + ctx_args(h_ctx) + [mix_a] * n_sub + ctx_args(ctx_mix_a)
            + [mix_b] * n_sub + ctx_args(ctx_mix_b))
    return pl.pallas_call(
        functools.partial(_post_kernel, n_sub=n_sub, has_ctx=has_ctx),
        grid=(bsz, n_tiles // n_sub),
        in_specs=(_stream_specs(n_sub, D_MODEL, lat_tile_off, has_ctx)
                  + _stream_specs(n_sub, Q_COLS, 0, has_ctx)
                  + _stream_specs(n_sub, Q_COLS, 0, has_ctx) + [
            pl.BlockSpec((None, 2, N_MOD, D_MODEL), lambda b, t: (b, 0, 0, 0)),
            resident(D_MODEL, D_MODEL),
            pl.BlockSpec((1, D_MODEL), const),
            pl.BlockSpec((1, D_MODEL), const),
            resident(D_MODEL, D_FF),
            resident(D_FF, D_MODEL),
            pl.BlockSpec((1, D_MODEL), const),
        ]),
        out_specs=pl.BlockSpec((None, n_sub * TOK_TILE, D_MODEL), lambda b, t: (b, t, 0)),
        out_shape=jax.ShapeDtypeStruct((bsz, n_tiles * TOK_TILE, D_MODEL), F32),
        compiler_params=pltpu.CompilerParams(
            dimension_semantics=("parallel", "parallel"), vmem_limit_bytes=VMEM_LIMIT),
        name="post",
    )(*args, mods, w_out, g_post_mix, g_pre_mlp, w_up, w_down, g_post_mlp)


def _rope_tables(n_ctx, n_lat):
    rows = n_lat // GRID_W
    row_ids = jnp.repeat(jnp.arange(rows, dtype=jnp.int32), GRID_W).astype(F32)
    col_ids = jnp.tile(jnp.arange(GRID_W, dtype=jnp.int32), rows).astype(F32)
    axis_dim = HEAD_DIM // 2
    inv = ROPE_THETA ** (-jnp.arange(0, axis_dim, 2, dtype=F32) / axis_dim)
    ang_r = row_ids[:, None] * inv[None, :]
    ang_c = col_ids[:, None] * inv[None, :]
    cos_r, sin_r, cos_c, sin_c = jnp.cos(ang_r), jnp.sin(ang_r), jnp.cos(ang_c), jnp.sin(ang_c)
    zero = jnp.zeros_like(sin_r)
    cos_h = jnp.concatenate([cos_r, cos_r, cos_c, cos_c], axis=-1)
    sa_h = jnp.concatenate([-sin_r, zero, -sin_c, zero], axis=-1)
    sb_h = jnp.concatenate([zero, sin_r, zero, sin_c], axis=-1)

    def full(t, ctx_value):
        t = jnp.tile(t, (1, LANES // HEAD_DIM))
        return jnp.concatenate([jnp.full((n_ctx, LANES), ctx_value, F32), t], axis=0)

    return full(cos_h, 1.0), full(sa_h, 0.0), full(sb_h, 0.0)


def _head_ones(n_cols):
    head = jnp.arange(n_cols, dtype=jnp.int32) // HEAD_DIM
    return (head[:, None] == head[None, :]).astype(BF16)


def kernel(x, c, ctx, c_ctx, w_ada, b_ada, g_pre_mix, g_post_mix, g_pre_mlp, g_post_mlp,
           w_in, q_norm, k_norm, sink, w_out, w_up, w_down):
    bsz, n_lat, d = x.shape
    n_ctx = ctx.shape[1]
    depth = w_in.shape[0]
    n_tok = n_ctx + n_lat
    assert d == D_MODEL and n_ctx == TOK_TILE and n_lat % TOK_TILE == 0 and bsz + 1 <= ADA_ROWS
    assert w_in.shape[2] == IN_COLS and w_up.shape[2] == D_FF

    cc = jnp.concatenate([c, c_ctx[None, :], jnp.zeros((ADA_ROWS - bsz - 1, d), F32)], axis=0)
    mod_all = _ada(cc, w_ada, b_ada)
    mod_all = mod_all.reshape(depth, ADA_ROWS, N_MOD, d)
    mod_ctx = jnp.broadcast_to(mod_all[:, bsz:bsz + 1], (depth, bsz, N_MOD, d))
    mods = jnp.stack([mod_ctx, mod_all[:, :bsz]], axis=2)

    cos_t, sa_t, sb_t = _rope_tables(n_ctx, n_lat)
    eq = _head_ones(Q_COLS)
    ek = _head_ones(KV_COLS)
    eh = _head_ones(KV_COLS)
    row = lambda v: v.reshape(1, -1)

    w_in_b = w_in.astype(BF16)
    w_out_b = w_out.astype(BF16)
    w_up_b = w_up.astype(BF16)
    w_down_b = w_down.astype(BF16)

    ctx_tiles = n_ctx // TOK_TILE
    assert depth >= 1 and ctx_tiles == 1
    ctx_qblk = n_ctx // Q_BLOCK
    lat_qblk = n_lat // Q_BLOCK

    h_lat, h_ctx = x, ctx
    lat_off = 0
    for l in range(depth):
        last = l == depth - 1
        gq = row(jnp.tile(q_norm[l], N_HEADS))
        gk = row(jnp.tile(k_norm[l], N_KV))
        sink_rows = jnp.repeat(sink[l].reshape(N_KV, 1, GROUP) * LOG2_E, Q_BLOCK, axis=2)

        qa, kda, vta, qb, kdb, vtb = _proj(h_lat, h_ctx, 0, n_tok // TOK_TILE, mods[l],
                                           row(g_pre_mix[l]), w_in_b, l, gq, gk, eq, ek,
                                           cos_t, sa_t, sb_t)
        bound_a = (HEAD_DIM * Q_SCALE * BOUND_SLACK * jnp.max(jnp.abs(q_norm[l]))
                   * jnp.max(jnp.abs(k_norm[l]))).reshape(1)
        oa = _attention(qa, kda, vta, None, eh, bound_a, q_blk_off=ctx_qblk, n_qblk=lat_qblk,
                        n_ctx=n_ctx, n_full=n_tok, band=False, q_per_step=Q_PER_STEP_GLOBAL,
                        name="attn_global")
        ob = _attention(qb, kdb, vtb, sink_rows, eh, None, q_blk_off=ctx_qblk, n_qblk=lat_qblk,
                        n_ctx=n_ctx, n_full=n_ctx, band=True, q_per_step=Q_PER_STEP_WINDOW,
                        name="attn_window")
        post = functools.partial(_post, mods=mods[l], layer=l, w_out=w_out_b, g_post_mix=row(g_post_mix[l]),
                                 g_pre_mlp=row(g_pre_mlp[l]), w_up=w_up_b, w_down=w_down_b,
                                 g_post_mlp=row(g_post_mlp[l]))
        if not last:
            oac = _attention(qa, kda, vta, None, eh, bound_a, q_blk_off=0, n_qblk=ctx_qblk, n_ctx=n_ctx,
                             n_full=n_ctx, band=False, q_per_step=ctx_qblk, name="attn_ctx_global")
            obc = _attention(qb, kdb, vtb, sink_rows, eh, None, q_blk_off=0, n_qblk=ctx_qblk, n_ctx=n_ctx,
                             n_full=n_ctx, band=False, q_per_step=ctx_qblk, name="attn_ctx_sink")
            assert h_ctx is not None
            h_lat = post(h_lat, h_ctx, lat_off, n_tok // TOK_TILE, oa, ob, oac, obc)
            h_ctx, lat_off = None, ctx_tiles
        else:
            if h_ctx is not None:
                h_lat = post(h_lat, None, 0, n_lat // TOK_TILE, oa, ob, None, None)
            else:
                h_lat = post(h_lat, None, lat_off, n_lat // TOK_TILE, oa, ob, None, None)
    return h_lat
```

```python
import functools

import jax
import jax.numpy as jnp
from jax import lax
from jax.experimental import pallas as pl
from jax.experimental.pallas import tpu as pltpu

F32 = jnp.float32
BF16 = jnp.bfloat16

D_MODEL = 1024
HEAD_DIM = 64
N_HEADS = 8
N_KV = 2
GROUP = N_HEADS // N_KV
Q_COLS = N_HEADS * HEAD_DIM
KV_COLS = N_KV * HEAD_DIM
IN_COLS = 2 * (Q_COLS + 2 * KV_COLS)
D_FF = 4 * D_MODEL
GRID_W = 64
WINDOW = 128
Q_BLOCK = 128
ROPE_THETA = 10000.0
EPS = 1e-6
NEG_BIG = -1e30
N_MOD = 6
ROPE_SHIFT = HEAD_DIM // 4
LOG2_E = 1.4426950408889634
Q_SCALE = HEAD_DIM ** -0.5 * LOG2_E

LANES = 128
TOK_TILE = 256
BF16_SUBLANES = 16
PROJ_SUBS = (1, 3, 9)
POST_SUBS = (1, 2, 3, 4)
VT_ROWS = HEAD_DIM + BF16_SUBLANES
KEY_CHUNK = 4096
Q_PER_STEP_GLOBAL = 8
Q_PER_STEP_WINDOW = 16
LOGIT_BOUND_LIMIT = 48.0
BOUND_SLACK = 1.01
FF_CHUNK = 1024
ADA_ROWS = 32
ADA_COLS = 1536
VMEM_LIMIT = 56 * 1024 * 1024


def _rms(x, g):
    return x * lax.rsqrt(jnp.mean(x * x, axis=-1, keepdims=True) + EPS) * g


def _ada_kernel(c_ref, w_ref, b_ref, o_ref):
    c = c_ref[...]
    a = c * jax.nn.sigmoid(c)
    w = w_ref[...]
    a_hi = a.astype(BF16)
    a_lo = (a - a_hi.astype(F32)).astype(BF16)
    w_hi = w.astype(BF16)
    w_lo = (w - w_hi.astype(F32)).astype(BF16)
    acc = jnp.dot(a_hi, w_hi, preferred_element_type=F32)
    acc += jnp.dot(a_lo, w_hi, preferred_element_type=F32)
    acc += jnp.dot(a_hi, w_lo, preferred_element_type=F32)
    o_ref[...] = acc + b_ref[...]


def _ada(cc, w_ada, b_ada):
    depth = w_ada.shape[0]
    n_out = w_ada.shape[2]
    return pl.pallas_call(
        _ada_kernel,
        grid=(depth, n_out // ADA_COLS),
        in_specs=[
            pl.BlockSpec((ADA_ROWS, D_MODEL), lambda l, j: (0, 0)),
            pl.BlockSpec((None, D_MODEL, ADA_COLS), lambda l, j: (l, 0, j)),
            pl.BlockSpec((None, 1, ADA_COLS), lambda l, j: (l, 0, j)),
        ],
        out_specs=pl.BlockSpec((None, ADA_ROWS, ADA_COLS), lambda l, j: (l, 0, j)),
        out_shape=jax.ShapeDtypeStruct((depth, ADA_ROWS, n_out), F32),
        compiler_params=pltpu.CompilerParams(
            dimension_semantics=("arbitrary", "arbitrary"), vmem_limit_bytes=VMEM_LIMIT),
        name="ada",
    )(cc, w_ada, b_ada.reshape(depth, 1, n_out))


def _stream_specs(n_sub, block_cols, lat_tile_off, has_ctx):
    specs = []
    for i in range(n_sub):
        if has_ctx:
            idx = lambda b, t, i=i: (b, jnp.maximum(n_sub * t + i - 1, 0), 0)
        else:
            idx = lambda b, t, i=i: (b, lat_tile_off + n_sub * t + i, 0)
        specs.append(pl.BlockSpec((None, TOK_TILE, block_cols), idx))
    if has_ctx:
        specs.append(pl.BlockSpec((None, TOK_TILE, block_cols), lambda b, t: (b, 0, 0)))
    return specs


def _sub_tile(refs, ctx_ref, i):
    v = refs[i][...]
    if ctx_ref is not None and i == 0:
        v = jnp.where(pl.program_id(1) == 0, ctx_ref[...], v)
    return v


def _sub_mod(mod_ref, ctx_first, i):
    if ctx_first and i == 0:
        return jnp.where(pl.program_id(1) == 0, mod_ref[0], mod_ref[1])
    return mod_ref[1]


def _proj_kernel(*refs, n_sub, has_ctx):
    h_refs = refs[:n_sub]
    refs = refs[n_sub:]
    ctx_ref = None
    if has_ctx:
        ctx_ref, refs = refs[0], refs[1:]
    (mod_ref, gpre_ref, w_ref, gq_ref, gk_ref, eq_ref, ek_ref, cos_ref, sa_ref, sb_ref,
     qa_ref, kda_ref, vta_ref, qb_ref, kdb_ref, vtb_ref) = refs
    kv = 2 * KV_COLS

    def prenorm(i):
        mod = _sub_mod(mod_ref, True, i)
        x = _sub_tile(h_refs, ctx_ref, i)
        return (_rms(x, gpre_ref[...]) * (1 + mod[1:2, :]) + mod[0:1, :]).astype(BF16)

    def stages(i, u):
        rows = slice(i * TOK_TILE, (i + 1) * TOK_TILE)
        cos = cos_ref[rows, :]
        sa = sa_ref[rows, :]
        sb = sb_ref[rows, :]

        def rope(t):
            up = pltpu.roll(t, LANES - ROPE_SHIFT, 1)
            dn = pltpu.roll(t, ROPE_SHIFT, 1)
            return t * cos + up * sa + dn * sb

        def head_norm(t, e_ref, g_ref):
            ss = jnp.dot((t * t).astype(BF16), e_ref[...], preferred_element_type=F32)
            return t * lax.rsqrt(ss * (1.0 / HEAD_DIM) + EPS) * g_ref[...]

        def store_q(q, qt_ref):
            q = jnp.concatenate([rope(q[:, j * LANES:(j + 1) * LANES]) for j in range(Q_COLS // LANES)],
                                axis=1) * Q_SCALE
            q_t = q.T.astype(BF16)
            for h in range(N_HEADS):
                qt_ref[h, :, rows] = q_t[h * HEAD_DIM:(h + 1) * HEAD_DIM, :]

        def store_k(k, k_ref):
            k_ref[rows, :] = rope(k).astype(BF16)

        def store_vt(v, vt_ref):
            v_t = v.T.astype(BF16)
            ones = jnp.ones((VT_ROWS - HEAD_DIM, TOK_TILE), BF16)
            for h in range(N_KV):
                vt_ref[h, 0:HEAD_DIM, rows] = v_t[h * HEAD_DIM:(h + 1) * HEAD_DIM, :]
                vt_ref[h, HEAD_DIM:VT_ROWS, rows] = ones

        def project(c0, n):
            return lambda: jnp.dot(u, w_ref[:, c0:c0 + n], preferred_element_type=F32)

        def epi_qa(z):
            store_q(head_norm(z, eq_ref, gq_ref), qa_ref)

        def epi_kva(z):
            store_k(head_norm(z[:, 0:KV_COLS], ek_ref, gk_ref), kda_ref)
            store_vt(z[:, KV_COLS:kv], vta_ref)

        def epi_qb(z):
            store_q(z, qb_ref)

        def epi_kvb(z):
            store_k(z[:, 0:KV_COLS], kdb_ref)
            store_vt(z[:, KV_COLS:kv], vtb_ref)

        return [(project(0, Q_COLS), epi_qa), (project(Q_COLS, kv), epi_kva),
                (project(Q_COLS + kv, Q_COLS), epi_qb), (project(2 * Q_COLS + kv, kv), epi_kvb)]

    u_next = prenorm(0)
    pending = []
    for i in range(n_sub):
        u = u_next
        results = []
        for g, (matmul, epilogue) in enumerate(stages(i, u)):
            results.append((epilogue, matmul()))
            if g == 0 and i + 1 < n_sub:
                u_next = prenorm(i + 1)
            if pending:
                epi, z = pending.pop(0)
                epi(z)
        for epi, z in pending:
            epi(z)
        pending = results
    for epi, z in pending:
        epi(z)


def _proj(h_lat, h_ctx, lat_tile_off, n_tiles, mods, g_pre, w_in, layer, gq, gk, eq, ek, cos_t, sa_t,
          sb_t):
    bsz = h_lat.shape[0]
    has_ctx = h_ctx is not None
    n_sub = max(s for s in PROJ_SUBS if n_tiles % s == 0)
    n_tok = n_tiles * TOK_TILE
    rows = n_sub * TOK_TILE
    const = lambda b, t: (0, 0)
    tok = lambda b, t: (b, t, 0)
    q_shape = jax.ShapeDtypeStruct((bsz, N_HEADS, HEAD_DIM, n_tok), BF16)
    kd_shape = jax.ShapeDtypeStruct((bsz, n_tok, KV_COLS), BF16)
    vt_shape = jax.ShapeDtypeStruct((bsz, N_KV, VT_ROWS, n_tok), BF16)
    q_spec = pl.BlockSpec((None, N_HEADS, HEAD_DIM, rows), lambda b, t: (b, 0, 0, t))
    kd_spec = pl.BlockSpec((None, rows, KV_COLS), tok)
    vt_spec = pl.BlockSpec((None, N_KV, VT_ROWS, rows), lambda b, t: (b, 0, 0, t))
    rope_spec = pl.BlockSpec((rows, LANES), lambda b, t: (t, 0))
    h_args = [h_lat] * n_sub + ([h_ctx] if has_ctx else [])
    return pl.pallas_call(
        functools.partial(_proj_kernel, n_sub=n_sub, has_ctx=has_ctx),
        grid=(bsz, n_tiles // n_sub),
        in_specs=_stream_specs(n_sub, D_MODEL, lat_tile_off, has_ctx) + [
            pl.BlockSpec((None, 2, N_MOD, D_MODEL), lambda b, t: (b, 0, 0, 0)),
            pl.BlockSpec((1, D_MODEL), const),
            pl.BlockSpec((None, D_MODEL, IN_COLS), lambda b, t: (layer, 0, 0)),
            pl.BlockSpec((1, Q_COLS), const),
            pl.BlockSpec((1, KV_COLS), const),
            pl.BlockSpec((Q_COLS, Q_COLS), const),
            pl.BlockSpec((KV_COLS, KV_COLS), const),
            rope_spec, rope_spec, rope_spec,
        ],
        out_specs=[q_spec, kd_spec, vt_spec, q_spec, kd_spec, vt_spec],
        out_shape=[q_shape, kd_shape, vt_shape, q_shape, kd_shape, vt_shape],
        compiler_params=pltpu.CompilerParams(
            dimension_semantics=("parallel", "parallel"), vmem_limit_bytes=VMEM_LIMIT),
        name="proj",
    )(*h_args, mods, g_pre, w_in, gq, gk, eq, ek, cos_t, sa_t, sb_t)


def _band_start(blk, n_ctx, n_tok):
    return jnp.minimum(n_ctx + (blk - 1) * Q_BLOCK, n_tok - (Q_BLOCK + 2 * WINDOW))


def _window_bias_variants(n_ctx, n_tok, n_lat_blk):
    n_band = Q_BLOCK + 2 * WINDOW
    kj = jnp.arange(n_band, dtype=jnp.int32)[:, None]
    qr = jnp.arange(GROUP * Q_BLOCK, dtype=jnp.int32)[None, :] % Q_BLOCK
    variants = []
    for blk in (0, 1, n_lat_blk - 1):
        kpos = kj + (_band_start(blk, n_ctx, n_tok) - n_ctx)
        qpos = qr + blk * Q_BLOCK
        valid = (jnp.abs(kpos - qpos) <= WINDOW) & (kpos >= 0)
        variants.append(jnp.where(valid, 0.0, NEG_BIG).astype(F32))
    return jnp.stack(variants)


def _attn_kernel(*refs, n_ctx, n_full, band, use_sink, n_tok, n_lat_blk, q_row0, q_per_step,
                 given_bound):
    refs = list(refs)
    qt_ref, k_ref, vt_ref, eh_ref = refs[:4]
    rest = refs[4:]
    sink_ref = rest.pop(0) if use_sink else None
    bias_ref = rest.pop(0) if band else None
    if given_bound:
        bound_ref, o_ref = rest
    else:
        o_ref, bound_ref = rest
    nq = GROUP * Q_BLOCK
    n_band = Q_BLOCK + 2 * WINDOW if band else 0
    chunks = [(c0, min(c0 + KEY_CHUNK, n_full), False) for c0 in range(0, n_full, KEY_CHUNK)]
    if band:
        chunks.append((n_full, n_full + n_band, True))
    kv_head = pl.program_id(1)
    step = pl.program_id(2)

    if not given_bound:
        @pl.when(step == 0)
        def _():
            q = qt_ref[...].astype(F32)
            qq = jnp.sum(q * q, axis=1)
            k = k_ref[...]
            kk = jnp.dot(k * k, eh_ref[...], preferred_element_type=F32)
            bound_ref[0] = jnp.sqrt(jnp.max(qq) * jnp.max(kk)) * BOUND_SLACK

    def band_start(blk):
        return pl.multiple_of(_band_start(blk, n_ctx, n_tok), LANES)

    def extend_q(blk):
        cols = pl.ds(pl.multiple_of(q_row0 + blk * Q_BLOCK, Q_BLOCK), Q_BLOCK)
        top = jnp.concatenate([qt_ref[g, :, cols] for g in range(GROUP)], axis=1)
        zero = jnp.zeros_like(top)
        return jnp.where(kv_head == 0, jnp.concatenate([top, zero], axis=0),
                         jnp.concatenate([zero, top], axis=0))

    def logits(qm, blk, chunk):
        c0, c1, in_band = chunk
        if in_band:
            variant = jnp.where(blk == 0, 0, jnp.where(blk >= n_lat_blk - 1, 2, 1))
            return jnp.dot(k_ref[pl.ds(band_start(blk), n_band), :], qm,
                           preferred_element_type=F32) + bias_ref[variant]
        return jnp.dot(k_ref[c0:c1, :], qm, preferred_element_type=F32)

    def values_t(blk, chunk):
        c0, c1, in_band = chunk
        if in_band:
            return vt_ref[:, pl.ds(band_start(blk), n_band)]
        return vt_ref[:, c0:c1]

    def column_max(qm, blk):
        mx = None
        for chunk in chunks:
            s = logits(qm, blk, chunk)
            cm = jnp.max(s.reshape(s.shape[0] // 8, 8, nq), axis=0)
            mx = cm if mx is None else jnp.maximum(mx, cm)
        return jnp.max(mx, axis=0, keepdims=True)

    def finalize(t, acc, m):
        l = acc[HEAD_DIM:HEAD_DIM + 1, :]
        if use_sink:
            l = l + jnp.exp2(sink_ref[...] - m)
        o_t = acc[0:HEAD_DIM, :] / l
        o_t = jnp.concatenate([o_t[:, g * Q_BLOCK:(g + 1) * Q_BLOCK] for g in range(GROUP)], axis=0)
        o_ref[pl.ds(pl.multiple_of(t * Q_BLOCK, Q_BLOCK), Q_BLOCK), :] = o_t.T.astype(BF16)

    def attend(blocks):
        items = [(i, c) for i in range(len(blocks)) for c in range(len(chunks))]
        blks = [step * q_per_step + t for t, _ in blocks]
        qms = [extend_q(blk) for blk in blks]
        s_next = logits(qms[0], blks[0], chunks[0])
        acc = [None] * len(blocks)
        pv_prev = None

        def retire(prev):
            i, c, pv = prev
            acc[i] = pv if acc[i] is None else acc[i] + pv
            if c == len(chunks) - 1:
                finalize(blocks[i][0], acc[i], blocks[i][1])

        for n, (i, c) in enumerate(items):
            s = s_next
            if n + 1 < len(items):
                i2, c2 = items[n + 1]
                s_next = logits(qms[i2], blks[i2], chunks[c2])
            p = jnp.exp2(s - blocks[i][1]).astype(BF16)
            pv = jnp.dot(values_t(blks[i], chunks[c]), p, preferred_element_type=F32)
            if pv_prev is not None:
                retire(pv_prev)
            pv_prev = (i, c, pv)
        retire(pv_prev)

    def with_sink(m):
        return jnp.maximum(m, sink_ref[...]) if use_sink else m

    bound = bound_ref[0]
    one_pass = bound <= LOGIT_BOUND_LIMIT

    @pl.when(one_pass)
    def _():
        m = with_sink(jnp.full((1, nq), bound, F32))
        attend([(t, m) for t in range(q_per_step)])

    @pl.when(jnp.logical_not(one_pass))
    def _():
        def one_block(t, carry):
            blk = step * q_per_step + t
            attend([(t, with_sink(column_max(extend_q(blk), blk)))])
            return carry

        lax.fori_loop(0, q_per_step, one_block, 0)


def _attention(q, kd, vt, sink_rows, head_ones, logit_bound, *, q_blk_off, n_qblk, n_ctx, n_full, band,
               q_per_step, name):
    bsz, n_tok, _ = kd.shape
    q_per_step = min(q_per_step, n_qblk)
    n_rows = n_tok if (band or n_full == n_tok) else max(n_full, (q_blk_off + n_qblk) * Q_BLOCK)
    assert n_qblk % q_per_step == 0
    use_sink = sink_rows is not None
    nq = GROUP * Q_BLOCK
    in_specs = [
        pl.BlockSpec((None, GROUP, HEAD_DIM, n_rows), lambda b, h, j: (b, h, 0, 0)),
        pl.BlockSpec((None, n_rows, KV_COLS), lambda b, h, j: (b, 0, 0)),
        pl.BlockSpec((None, None, VT_ROWS, n_rows), lambda b, h, j: (b, h, 0, 0)),
        pl.BlockSpec(head_ones.shape, lambda b, h, j: (0, 0)),
    ]
    args = [q, kd, vt, head_ones]
    if use_sink:
        in_specs.append(pl.BlockSpec((None, 1, nq), lambda b, h, j: (h, 0, 0)))
        args.append(sink_rows)
    if band:
        bias = _window_bias_variants(n_ctx, n_tok, n_qblk)
        in_specs.append(pl.BlockSpec(bias.shape, lambda b, h, j: (0, 0, 0)))
        args.append(bias)
    given_bound = logit_bound is not None
    if given_bound:
        in_specs.append(pl.BlockSpec(memory_space=pltpu.SMEM))
        args.append(logit_bound)
    kern = functools.partial(_attn_kernel, n_ctx=n_ctx, n_full=n_full, band=band, use_sink=use_sink,
                             n_tok=n_tok, n_lat_blk=n_qblk, q_row0=q_blk_off * Q_BLOCK,
                             q_per_step=q_per_step, given_bound=given_bound)
    rows = q_per_step * Q_BLOCK
    return pl.pallas_call(
        kern,
        grid=(bsz, N_KV, n_qblk // q_per_step),
        in_specs=in_specs,
        out_specs=pl.BlockSpec((None, rows, GROUP * HEAD_DIM), lambda b, h, j: (b, j, h)),
        out_shape=jax.ShapeDtypeStruct((bsz, n_qblk * Q_BLOCK, Q_COLS), BF16),
        scratch_shapes=[] if given_bound else [pltpu.SMEM((1,), F32)],
        compiler_params=pltpu.CompilerParams(
            dimension_semantics=("parallel", "parallel", "arbitrary"), vmem_limit_bytes=VMEM_LIMIT),
        name=name,
    )(*args)


def _post_kernel(*refs, n_sub, has_ctx):
    n_in = n_sub + (1 if has_ctx else 0)
    h_refs, ma_refs, mb_refs = refs[:n_sub], refs[n_in:n_in + n_sub], refs[2 * n_in:2 * n_in + n_sub]
    ctx_h = refs[n_sub] if has_ctx else None
    ctx_ma = refs[n_in + n_sub] if has_ctx else None
    ctx_mb = refs[2 * n_in + n_sub] if has_ctx else None
    mod_ref, wout_ref, gpm_ref, gpre_ref, wup_ref, wdn_ref, gpl_ref, o_ref = refs[3 * n_in:]
    mods = [_sub_mod(mod_ref, has_ctx, i) for i in range(n_sub)]

    def out_proj(i):
        mix = jnp.concatenate([_sub_tile(ma_refs, ctx_ma, i), _sub_tile(mb_refs, ctx_mb, i)], axis=1)
        return jnp.dot(mix, wout_ref[...], preferred_element_type=F32)

    def mixer_residual(i, mix):
        mod = mods[i]
        h = _sub_tile(h_refs, ctx_h, i) + mod[2:3, :] * _rms(mix, gpm_ref[...])
        u = (_rms(h, gpre_ref[...]) * (1 + mod[4:5, :]) + mod[3:4, :]).astype(BF16)
        return h, u

    def mlp_chunk(u, y, c0):
        a = jnp.dot(u, wup_ref[:, c0:c0 + FF_CHUNK], preferred_element_type=F32)
        a = jnp.square(jnp.maximum(a, 0.0)).astype(BF16)
        part = jnp.dot(a, wdn_ref[c0:c0 + FF_CHUNK, :], preferred_element_type=F32)
        return part if y is None else y + part

    def mlp_residual(i, h, y):
        o_ref[i * TOK_TILE:(i + 1) * TOK_TILE, :] = h + mods[i][5:6, :] * _rms(y, gpl_ref[...])

    hu = [None] * n_sub
    mix_prev = None
    for i in range(n_sub):
        mix = out_proj(i)
        if i > 0:
            hu[i - 1] = mixer_residual(i - 1, mix_prev)
        mix_prev = mix
    for i in range(n_sub):
        y = None
        for c, c0 in enumerate(range(0, D_FF, FF_CHUNK)):
            if i == 0 and c == 0:
                hu[n_sub - 1] = mixer_residual(n_sub - 1, mix_prev)
            y = mlp_chunk(hu[i][1], y, c0)
            if i > 0 and c == 0:
                mlp_residual(i - 1, hu[i - 1][0], y_prev)
        y_prev = y
    mlp_residual(n_sub - 1, hu[n_sub - 1][0], y_prev)


def _post(h_lat, h_ctx, lat_tile_off, n_tiles, mix_a, mix_b, ctx_mix_a, ctx_mix_b, mods, layer, w_out,
          g_post_mix, g_pre_mlp, w_up, w_down, g_post_mlp):
    bsz = h_lat.shape[0]
    has_ctx = h_ctx is not None
    n_sub = max(s for s in POST_SUBS if n_tiles % s == 0)
    const = lambda b, t: (0, 0)
    resident = lambda rows, cols: pl.BlockSpec((None, rows, cols), lambda b, t: (layer, 0, 0),
                                               pipeline_mode=pl.Buffered(1))
    ctx_args = lambda v: [v] if has_ctx else []
    args = ([h_lat] * n_sub + ctx_args(h_ctx) + [mix_a] * n_sub + ctx_args(ctx_mix_a)
            + [mix_b] * n_sub + ctx_args(ctx_mix_b))
    return pl.pallas_call(
        functools.partial(_post_kernel, n_sub=n_sub, has_ctx=has_ctx),
        grid=(bsz, n_tiles // n_sub),
        in_specs=(_stream_specs(n_sub, D_MODEL, lat_tile_off, has_ctx)
                  + _stream_specs(n_sub, Q_COLS, 0, has_ctx)
                  + _stream_specs(n_sub, Q_COLS, 0, has_ctx) + [
            pl.BlockSpec((None, 2, N_MOD, D_MODEL), lambda b, t: (b, 0, 0, 0)),
            resident(D_MODEL, D_MODEL),
            pl.BlockSpec((1, D_MODEL), const),
            pl.BlockSpec((1, D_MODEL), const),
            resident(D_MODEL, D_FF),
            resident(D_FF, D_MODEL),
            pl.BlockSpec((1, D_MODEL), const),
        ]),
        out_specs=pl.BlockSpec((None, n_sub * TOK_TILE, D_MODEL), lambda b, t: (b, t, 0)),
        out_shape=jax.ShapeDtypeStruct((bsz, n_tiles * TOK_TILE, D_MODEL), F32),
        compiler_params=pltpu.CompilerParams(
            dimension_semantics=("parallel", "parallel"), vmem_limit_bytes=VMEM_LIMIT),
        name="post",
    )(*args, mods, w_out, g_post_mix, g_pre_mlp, w_up, w_down, g_post_mlp)


def _rope_tables(n_ctx, n_lat):
    rows = n_lat // GRID_W
    row_ids = jnp.repeat(jnp.arange(rows, dtype=jnp.int32), GRID_W).astype(F32)
    col_ids = jnp.tile(jnp.arange(GRID_W, dtype=jnp.int32), rows).astype(F32)
    axis_dim = HEAD_DIM // 2
    inv = ROPE_THETA ** (-jnp.arange(0, axis_dim, 2, dtype=F32) / axis_dim)
    ang_r = row_ids[:, None] * inv[None, :]
    ang_c = col_ids[:, None] * inv[None, :]
    cos_r, sin_r, cos_c, sin_c = jnp.cos(ang_r), jnp.sin(ang_r), jnp.cos(ang_c), jnp.sin(ang_c)
    zero = jnp.zeros_like(sin_r)
    cos_h = jnp.concatenate([cos_r, cos_r, cos_c, cos_c], axis=-1)
    sa_h = jnp.concatenate([-sin_r, zero, -sin_c, zero], axis=-1)
    sb_h = jnp.concatenate([zero, sin_r, zero, sin_c], axis=-1)

    def full(t, ctx_value):
        t = jnp.tile(t, (1, LANES // HEAD_DIM))
        return jnp.concatenate([jnp.full((n_ctx, LANES), ctx_value, F32), t], axis=0)

    return full(cos_h, 1.0), full(sa_h, 0.0), full(sb_h, 0.0)


def _head_ones(n_cols):
    head = jnp.arange(n_cols, dtype=jnp.int32) // HEAD_DIM
    return (head[:, None] == head[None, :]).astype(BF16)


def kernel(x, c, ctx, c_ctx, w_ada, b_ada, g_pre_mix, g_post_mix, g_pre_mlp, g_post_mlp,
           w_in, q_norm, k_norm, sink, w_out, w_up, w_down):
    bsz, n_lat, d = x.shape
    n_ctx = ctx.shape[1]
    depth = w_in.shape[0]
    n_tok = n_ctx + n_lat
    assert d == D_MODEL and n_ctx == TOK_TILE and n_lat % TOK_TILE == 0 and bsz + 1 <= ADA_ROWS
    assert w_in.shape[2] == IN_COLS and w_up.shape[2] == D_FF

    cc = jnp.concatenate([c, c_ctx[None, :], jnp.zeros((ADA_ROWS - bsz - 1, d), F32)], axis=0)
    mod_all = _ada(cc, w_ada, b_ada)
    mod_all = mod_all.reshape(depth, ADA_ROWS, N_MOD, d)
    mod_ctx = jnp.broadcast_to(mod_all[:, bsz:bsz + 1], (depth, bsz, N_MOD, d))
    mods = jnp.stack([mod_ctx, mod_all[:, :bsz]], axis=2)

    cos_t, sa_t, sb_t = _rope_tables(n_ctx, n_lat)
    eq = _head_ones(Q_COLS)
    ek = _head_ones(KV_COLS)
    eh = _head_ones(KV_COLS)
    row = lambda v: v.reshape(1, -1)

    w_in_b = w_in.astype(BF16)
    w_out_b = w_out.astype(BF16)
    w_up_b = w_up.astype(BF16)
    w_down_b = w_down.astype(BF16)

    ctx_tiles = n_ctx // TOK_TILE
    assert depth >= 1 and ctx_tiles == 1
    ctx_qblk = n_ctx // Q_BLOCK
    lat_qblk = n_lat // Q_BLOCK

    h_lat, h_ctx = x, ctx
    lat_off = 0
    for l in range(depth):
        last = l == depth - 1
        gq = row(jnp.tile(q_norm[l], N_HEADS))
        gk = row(jnp.tile(k_norm[l], N_KV))
        sink_rows = jnp.repeat(sink[l].reshape(N_KV, 1, GROUP) * LOG2_E, Q_BLOCK, axis=2)

        qa, kda, vta, qb, kdb, vtb = _proj(h_lat, h_ctx, 0, n_tok // TOK_TILE, mods[l],
                                           row(g_pre_mix[l]), w_in_b, l, gq, gk, eq, ek,
                                           cos_t, sa_t, sb_t)
        bound_a = (HEAD_DIM * Q_SCALE * BOUND_SLACK * jnp.max(jnp.abs(q_norm[l]))
                   * jnp.max(jnp.abs(k_norm[l]))).reshape(1)
        oa = _attention(qa, kda, vta, None, eh, bound_a, q_blk_off=ctx_qblk, n_qblk=lat_qblk,
                        n_ctx=n_ctx, n_full=n_tok, band=False, q_per_step=Q_PER_STEP_GLOBAL,
                        name="attn_global")
        ob = _attention(qb, kdb, vtb, sink_rows, eh, None, q_blk_off=ctx_qblk, n_qblk=lat_qblk,
                        n_ctx=n_ctx, n_full=n_ctx, band=True, q_per_step=Q_PER_STEP_WINDOW,
                        name="attn_window")
        post = functools.partial(_post, mods=mods[l], layer=l, w_out=w_out_b, g_post_mix=row(g_post_mix[l]),
                                 g_pre_mlp=row(g_pre_mlp[l]), w_up=w_up_b, w_down=w_down_b,
                                 g_post_mlp=row(g_post_mlp[l]))
        if not last:
            oac = _attention(qa, kda, vta, None, eh, bound_a, q_blk_off=0, n_qblk=ctx_qblk, n_ctx=n_ctx,
                             n_full=n_ctx, band=False, q_per_step=ctx_qblk, name="attn_ctx_global")
            obc = _attention(qb, kdb, vtb, sink_rows, eh, None, q_blk_off=0, n_qblk=ctx_qblk, n_ctx=n_ctx,
                             n_full=n_ctx, band=False, q_per_step=ctx_qblk, name="attn_ctx_sink")
            assert h_ctx is not None
            h_lat = post(h_lat, h_ctx, lat_off, n_tok // TOK_TILE, oa, ob, oac, obc)
            h_ctx, lat_off = None, ctx_tiles
        else:
            if h_ctx is not None:
                h_lat = post(h_lat, None, 0, n_lat // TOK_TILE, oa, ob, None, None)
            else:
                h_lat = post(h_lat, None, lat_off, n_lat // TOK_TILE, oa, ob, None, None)
    return h_lat
```

```python
import functools

import jax
import jax.numpy as jnp
from jax import lax
from jax.experimental import pallas as pl
from jax.experimental.pallas import tpu as pltpu

F32 = jnp.float32
BF16 = jnp.bfloat16

D_MODEL = 1024
HEAD_DIM = 64
N_HEADS = 8
N_KV = 2
GROUP = N_HEADS // N_KV
Q_COLS = N_HEADS * HEAD_DIM
KV_COLS = N_KV * HEAD_DIM
IN_COLS = 2 * (Q_COLS + 2 * KV_COLS)
D_FF = 4 * D_MODEL
GRID_W = 64
WINDOW = 128
Q_BLOCK = 128
ROPE_THETA = 10000.0
EPS = 1e-6
NEG_BIG = -1e30
N_MOD = 6
ROPE_SHIFT = HEAD_DIM // 4
LOG2_E = 1.4426950408889634
Q_SCALE = HEAD_DIM ** -0.5 * LOG2_E

LANES = 128
TOK_TILE = 256
BF16_SUBLANES = 16
PROJ_SUBS = (1, 3, 9)
POST_SUBS = (1, 2, 3, 4)
VT_ROWS = HEAD_DIM + BF16_SUBLANES
KEY_CHUNK = 4096
Q_PER_STEP_GLOBAL = 8
Q_PER_STEP_WINDOW = 16
LOGIT_BOUND_LIMIT = 48.0
BOUND_SLACK = 1.01
FF_CHUNK = 1024
ADA_ROWS = 32
ADA_COLS = 1536
VMEM_LIMIT = 56 * 1024 * 1024


def _rms(x, g):
    return x * lax.rsqrt(jnp.mean(x * x, axis=-1, keepdims=True) + EPS) * g


def _ada_kernel(c_ref, w_ref, b_ref, o_ref):
    c = c_ref[...]
    a = c * jax.nn.sigmoid(c)
    w = w_ref[...]
    a_hi = a.astype(BF16)
    a_lo = (a - a_hi.astype(F32)).astype(BF16)
    w_hi = w.astype(BF16)
    w_lo = (w - w_hi.astype(F32)).astype(BF16)
    acc = jnp.dot(a_hi, w_hi, preferred_element_type=F32)
    acc += jnp.dot(a_lo, w_hi, preferred_element_type=F32)
    acc += jnp.dot(a_hi, w_lo, preferred_element_type=F32)
    o_ref[...] = acc + b_ref[...]


def _ada(cc, w_ada, b_ada):
    depth = w_ada.shape[0]
    n_out = w_ada.shape[2]
    return pl.pallas_call(
        _ada_kernel,
        grid=(depth, n_out // ADA_COLS),
        in_specs=[
            pl.BlockSpec((ADA_ROWS, D_MODEL), lambda l, j: (0, 0)),
            pl.BlockSpec((None, D_MODEL, ADA_COLS), lambda l, j: (l, 0, j)),
            pl.BlockSpec((None, 1, ADA_COLS), lambda l, j: (l, 0, j)),
        ],
        out_specs=pl.BlockSpec((None, ADA_ROWS, ADA_COLS), lambda l, j: (l, 0, j)),
        out_shape=jax.ShapeDtypeStruct((depth, ADA_ROWS, n_out), F32),
        compiler_params=pltpu.CompilerParams(
            dimension_semantics=("arbitrary", "arbitrary"), vmem_limit_bytes=VMEM_LIMIT),
        name="ada",
    )(cc, w_ada, b_ada.reshape(depth, 1, n_out))


def _stream_specs(n_sub, block_cols, lat_tile_off, has_ctx):
    specs = []
    for i in range(n_sub):
        if has_ctx:
            idx = lambda b, t, i=i: (b, jnp.maximum(n_sub * t + i - 1, 0), 0)
        else:
            idx = lambda b, t, i=i: (b, lat_tile_off + n_sub * t + i, 0)
        specs.append(pl.BlockSpec((None, TOK_TILE, block_cols), idx))
    if has_ctx:
        specs.append(pl.BlockSpec((None, TOK_TILE, block_cols), lambda b, t: (b, 0, 0)))
    return specs


def _sub_tile(refs, ctx_ref, i):
    v = refs[i][...]
    if ctx_ref is not None and i == 0:
        v = jnp.where(pl.program_id(1) == 0, ctx_ref[...], v)
    return v


def _sub_mod(mod_ref, ctx_first, i):
    if ctx_first and i == 0:
        return jnp.where(pl.program_id(1) == 0, mod_ref[0], mod_ref[1])
    return mod_ref[1]


def _proj_kernel(*refs, n_sub, has_ctx):
    h_refs = refs[:n_sub]
    refs = refs[n_sub:]
    ctx_ref = None
    if has_ctx:
        ctx_ref, refs = refs[0], refs[1:]
    (mod_ref, gpre_ref, w_ref, gqt_ref, gk_ref, ek_ref, cos_ref, sa_ref, sb_ref, cost_ref, sat_ref, sbt_ref,
     qa_ref, kda_ref, vta_ref, qb_ref, kdb_ref, vtb_ref) = refs
    kv = 2 * KV_COLS

    def prenorm(i):
        mod = _sub_mod(mod_ref, True, i)
        x = _sub_tile(h_refs, ctx_ref, i)
        return (_rms(x, gpre_ref[...]) * (1 + mod[1:2, :]) + mod[0:1, :]).astype(BF16)

    def stages(i, u):
        rows = slice(i * TOK_TILE, (i + 1) * TOK_TILE)
        cos = cos_ref[rows, :]
        sa = sa_ref[rows, :]
        sb = sb_ref[rows, :]

        def rope(t):
            up = pltpu.roll(t, LANES - ROPE_SHIFT, 1)
            dn = pltpu.roll(t, ROPE_SHIFT, 1)
            return t * cos + up * sa + dn * sb

        def head_norm(t, e_ref, g_ref):
            ss = jnp.dot((t * t).astype(BF16), e_ref[...], preferred_element_type=F32)
            return t * lax.rsqrt(ss * (1.0 / HEAD_DIM) + EPS) * g_ref[...]

        def store_q(q, qt_ref, head_normed):
            q_t = q.T.reshape(N_HEADS, HEAD_DIM, TOK_TILE)
            if head_normed:
                ms = jnp.sum(q_t * q_t, axis=1, keepdims=True) * (1.0 / HEAD_DIM)
                q_t = q_t * lax.rsqrt(ms + EPS) * gqt_ref[...][None]
            up = jnp.concatenate([q_t[:, ROPE_SHIFT:, :], q_t[:, :ROPE_SHIFT, :]], axis=1)
            dn = jnp.concatenate([q_t[:, -ROPE_SHIFT:, :], q_t[:, :-ROPE_SHIFT, :]], axis=1)
            q_t = q_t * cost_ref[:, rows][None] + up * sat_ref[:, rows][None] + dn * sbt_ref[:, rows][None]
            qt_ref[:, :, rows] = (q_t * Q_SCALE).astype(BF16)

        def store_k(k, k_ref):
            k_ref[rows, :] = rope(k).astype(BF16)

        def store_vt(v, vt_ref):
            v_t = v.T.astype(BF16)
            ones = jnp.ones((VT_ROWS - HEAD_DIM, TOK_TILE), BF16)
            for h in range(N_KV):
                vt_ref[h, 0:HEAD_DIM, rows] = v_t[h * HEAD_DIM:(h + 1) * HEAD_DIM, :]
                vt_ref[h, HEAD_DIM:VT_ROWS, rows] = ones

        def project(c0, n):
            return lambda: jnp.dot(u, w_ref[:, c0:c0 + n], preferred_element_type=F32)

        def epi_qa(z):
            store_q(z, qa_ref, head_normed=True)

        def epi_kva(z):
            store_k(head_norm(z[:, 0:KV_COLS], ek_ref, gk_ref), kda_ref)
            store_vt(z[:, KV_COLS:kv], vta_ref)

        def epi_qb(z):
            store_q(z, qb_ref, head_normed=False)

        def epi_kvb(z):
            store_k(z[:, 0:KV_COLS], kdb_ref)
            store_vt(z[:, KV_COLS:kv], vtb_ref)

        return [(project(0, Q_COLS), epi_qa), (project(Q_COLS, kv), epi_kva),
                (project(Q_COLS + kv, Q_COLS), epi_qb), (project(2 * Q_COLS + kv, kv), epi_kvb)]

    u_next = prenorm(0)
    pending = []
    for i in range(n_sub):
        u = u_next
        results = []
        for g, (matmul, epilogue) in enumerate(stages(i, u)):
            results.append((epilogue, matmul()))
            if g == 0 and i + 1 < n_sub:
                u_next = prenorm(i + 1)
            if pending:
                epi, z = pending.pop(0)
                epi(z)
        for epi, z in pending:
            epi(z)
        pending = results
    for epi, z in pending:
        epi(z)


def _proj(h_lat, h_ctx, lat_tile_off, n_tiles, mods, g_pre, w_in, layer, gq_t, gk, ek, rope, rope_t):
    bsz = h_lat.shape[0]
    has_ctx = h_ctx is not None
    n_sub = max(s for s in PROJ_SUBS if n_tiles % s == 0)
    n_tok = n_tiles * TOK_TILE
    rows = n_sub * TOK_TILE
    const = lambda b, t: (0, 0)
    tok = lambda b, t: (b, t, 0)
    q_shape = jax.ShapeDtypeStruct((bsz, N_HEADS, HEAD_DIM, n_tok), BF16)
    kd_shape = jax.ShapeDtypeStruct((bsz, n_tok, KV_COLS), BF16)
    vt_shape = jax.ShapeDtypeStruct((bsz, N_KV, VT_ROWS, n_tok), BF16)
    q_spec = pl.BlockSpec((None, N_HEADS, HEAD_DIM, rows), lambda b, t: (b, 0, 0, t))
    kd_spec = pl.BlockSpec((None, rows, KV_COLS), tok)
    vt_spec = pl.BlockSpec((None, N_KV, VT_ROWS, rows), lambda b, t: (b, 0, 0, t))
    rope_spec = pl.BlockSpec((rows, LANES), lambda b, t: (t, 0))
    rope_t_spec = pl.BlockSpec((HEAD_DIM, rows), lambda b, t: (0, t))
    h_args = [h_lat] * n_sub + ([h_ctx] if has_ctx else [])
    return pl.pallas_call(
        functools.partial(_proj_kernel, n_sub=n_sub, has_ctx=has_ctx),
        grid=(bsz, n_tiles // n_sub),
        in_specs=_stream_specs(n_sub, D_MODEL, lat_tile_off, has_ctx) + [
            pl.BlockSpec((None, 2, N_MOD, D_MODEL), lambda b, t: (b, 0, 0, 0)),
            pl.BlockSpec((1, D_MODEL), const),
            pl.BlockSpec((None, D_MODEL, IN_COLS), lambda b, t: (layer, 0, 0)),
            pl.BlockSpec((HEAD_DIM, TOK_TILE), const),
            pl.BlockSpec((1, KV_COLS), const),
            pl.BlockSpec((KV_COLS, KV_COLS), const),
            rope_spec, rope_spec, rope_spec, rope_t_spec, rope_t_spec, rope_t_spec,
        ],
        out_specs=[q_spec, kd_spec, vt_spec, q_spec, kd_spec, vt_spec],
        out_shape=[q_shape, kd_shape, vt_shape, q_shape, kd_shape, vt_shape],
        compiler_params=pltpu.CompilerParams(
            dimension_semantics=("parallel", "parallel"), vmem_limit_bytes=VMEM_LIMIT),
        name="proj",
    )(*h_args, mods, g_pre, w_in, gq_t, gk, ek, *rope, *rope_t)


def _band_start(blk, n_ctx, n_tok):
    return jnp.minimum(n_ctx + (blk - 1) * Q_BLOCK, n_tok - (Q_BLOCK + 2 * WINDOW))


def _window_bias_variants(n_ctx, n_tok, n_lat_blk):
    n_band = Q_BLOCK + 2 * WINDOW
    kj = jnp.arange(n_band, dtype=jnp.int32)[:, None]
    qr = jnp.arange(GROUP * Q_BLOCK, dtype=jnp.int32)[None, :] % Q_BLOCK
    variants = []
    for blk in (0, 1, n_lat_blk - 1):
        kpos = kj + (_band_start(blk, n_ctx, n_tok) - n_ctx)
        qpos = qr + blk * Q_BLOCK
        valid = (jnp.abs(kpos - qpos) <= WINDOW) & (kpos >= 0)
        variants.append(jnp.where(valid, 0.0, NEG_BIG).astype(F32))
    return jnp.stack(variants)


def _attn_kernel(*refs, n_ctx, n_full, band, use_sink, n_tok, n_lat_blk, q_row0, q_per_step,
                 given_bound):
    refs = list(refs)
    qt_ref, k_ref, vt_ref, eh_ref = refs[:4]
    rest = refs[4:]
    sink_ref = rest.pop(0) if use_sink else None
    bias_ref = rest.pop(0) if band else None
    if given_bound:
        bound_ref, o_ref = rest
    else:
        o_ref, bound_ref = rest
    nq = GROUP * Q_BLOCK
    n_band = Q_BLOCK + 2 * WINDOW if band else 0
    chunks = [(c0, min(c0 + KEY_CHUNK, n_full), False) for c0 in range(0, n_full, KEY_CHUNK)]
    if band:
        chunks.append((n_full, n_full + n_band, True))
    kv_head = pl.program_id(1)
    step = pl.program_id(2)

    if not given_bound:
        @pl.when(step == 0)
        def _():
            q = qt_ref[...].astype(F32)
            qq = jnp.sum(q * q, axis=1)
            k = k_ref[...]
            kk = jnp.dot(k * k, eh_ref[...], preferred_element_type=F32)
            bound_ref[0] = jnp.sqrt(jnp.max(qq) * jnp.max(kk)) * BOUND_SLACK

    def band_start(blk):
        return pl.multiple_of(_band_start(blk, n_ctx, n_tok), LANES)

    def extend_q(blk):
        cols = pl.ds(pl.multiple_of(q_row0 + blk * Q_BLOCK, Q_BLOCK), Q_BLOCK)
        top = jnp.concatenate([qt_ref[g, :, cols] for g in range(GROUP)], axis=1)
        zero = jnp.zeros_like(top)
        return jnp.where(kv_head == 0, jnp.concatenate([top, zero], axis=0),
                         jnp.concatenate([zero, top], axis=0))

    def logits(qm, blk, chunk):
        c0, c1, in_band = chunk
        if in_band:
            variant = jnp.where(blk == 0, 0, jnp.where(blk >= n_lat_blk - 1, 2, 1))
            return jnp.dot(k_ref[pl.ds(band_start(blk), n_band), :], qm,
                           preferred_element_type=F32) + bias_ref[variant]
        return jnp.dot(k_ref[c0:c1, :], qm, preferred_element_type=F32)

    def values_t(blk, chunk):
        c0, c1, in_band = chunk
        if in_band:
            return vt_ref[:, pl.ds(band_start(blk), n_band)]
        return vt_ref[:, c0:c1]

    def column_max(qm, blk):
        mx = None
        for chunk in chunks:
            s = logits(qm, blk, chunk)
            cm = jnp.max(s.reshape(s.shape[0] // 8, 8, nq), axis=0)
            mx = cm if mx is None else jnp.maximum(mx, cm)
        return jnp.max(mx, axis=0, keepdims=True)

    def finalize(t, acc, m):
        l = acc[HEAD_DIM:HEAD_DIM + 1, :]
        if use_sink:
            l = l + jnp.exp2(sink_ref[...] - m)
        o_t = acc[0:HEAD_DIM, :] / l
        o_t = jnp.concatenate([o_t[:, g * Q_BLOCK:(g + 1) * Q_BLOCK] for g in range(GROUP)], axis=0)
        o_ref[pl.ds(pl.multiple_of(t * Q_BLOCK, Q_BLOCK), Q_BLOCK), :] = o_t.T.astype(BF16)

    def attend(blocks):
        items = [(i, c) for i in range(len(blocks)) for c in range(len(chunks))]
        blks = [step * q_per_step + t for t, _ in blocks]
        qms = [extend_q(blk) for blk in blks]
        s_next = logits(qms[0], blks[0], chunks[0])
        acc = [None] * len(blocks)
        pv_prev = None

        def retire(prev):
            i, c, pv = prev
            acc[i] = pv if acc[i] is None else acc[i] + pv
            if c == len(chunks) - 1:
                finalize(blocks[i][0], acc[i], blocks[i][1])

        for n, (i, c) in enumerate(items):
            s = s_next
            if n + 1 < len(items):
                i2, c2 = items[n + 1]
                s_next = logits(qms[i2], blks[i2], chunks[c2])
            p = jnp.exp2(s - blocks[i][1]).astype(BF16)
            pv = jnp.dot(values_t(blks[i], chunks[c]), p, preferred_element_type=F32)
            if pv_prev is not None:
                retire(pv_prev)
            pv_prev = (i, c, pv)
        retire(pv_prev)

    def with_sink(m):
        return jnp.maximum(m, sink_ref[...]) if use_sink else m

    bound = bound_ref[0]
    one_pass = bound <= LOGIT_BOUND_LIMIT

    @pl.when(one_pass)
    def _():
        m = with_sink(jnp.full((1, nq), bound, F32))
        attend([(t, m) for t in range(q_per_step)])

    @pl.when(jnp.logical_not(one_pass))
    def _():
        def one_block(t, carry):
            blk = step * q_per_step + t
            attend([(t, with_sink(column_max(extend_q(blk), blk)))])
            return carry

        lax.fori_loop(0, q_per_step, one_block, 0)


def _attention(q, kd, vt, sink_rows, head_ones, logit_bound, *, q_blk_off, n_qblk, n_ctx, n_full, band,
               q_per_step, name):
    bsz, n_tok, _ = kd.shape
    q_per_step = min(q_per_step, n_qblk)
    n_rows = n_tok if (band or n_full == n_tok) else max(n_full, (q_blk_off + n_qblk) * Q_BLOCK)
    assert n_qblk % q_per_step == 0
    use_sink = sink_rows is not None
    nq = GROUP * Q_BLOCK
    in_specs = [
        pl.BlockSpec((None, GROUP, HEAD_DIM, n_rows), lambda b, h, j: (b, h, 0, 0)),
        pl.BlockSpec((None, n_rows, KV_COLS), lambda b, h, j: (b, 0, 0)),
        pl.BlockSpec((None, None, VT_ROWS, n_rows), lambda b, h, j: (b, h, 0, 0)),
        pl.BlockSpec(head_ones.shape, lambda b, h, j: (0, 0)),
    ]
    args = [q, kd, vt, head_ones]
    if use_sink:
        in_specs.append(pl.BlockSpec((None, 1, nq), lambda b, h, j: (h, 0, 0)))
        args.append(sink_rows)
    if band:
        bias = _window_bias_variants(n_ctx, n_tok, n_qblk)
        in_specs.append(pl.BlockSpec(bias.shape, lambda b, h, j: (0, 0, 0)))
        args.append(bias)
    given_bound = logit_bound is not None
    if given_bound:
        in_specs.append(pl.BlockSpec(memory_space=pltpu.SMEM))
        args.append(logit_bound)
    kern = functools.partial(_attn_kernel, n_ctx=n_ctx, n_full=n_full, band=band, use_sink=use_sink,
                             n_tok=n_tok, n_lat_blk=n_qblk, q_row0=q_blk_off * Q_BLOCK,
                             q_per_step=q_per_step, given_bound=given_bound)
    rows = q_per_step * Q_BLOCK
    return pl.pallas_call(
        kern,
        grid=(bsz, N_KV, n_qblk // q_per_step),
        in_specs=in_specs,
        out_specs=pl.BlockSpec((None, rows, GROUP * HEAD_DIM), lambda b, h, j: (b, j, h)),
        out_shape=jax.ShapeDtypeStruct((bsz, n_qblk * Q_BLOCK, Q_COLS), BF16),
        scratch_shapes=[] if given_bound else [pltpu.SMEM((1,), F32)],
        compiler_params=pltpu.CompilerParams(
            dimension_semantics=("parallel", "parallel", "arbitrary"), vmem_limit_bytes=VMEM_LIMIT),
        name=name,
    )(*args)


def _post_kernel(*refs, n_sub, has_ctx):
    n_in = n_sub + (1 if has_ctx else 0)
    h_refs, ma_refs, mb_refs = refs[:n_sub], refs[n_in:n_in + n_sub], refs[2 * n_in:2 * n_in + n_sub]
    ctx_h = refs[n_sub] if has_ctx else None
    ctx_ma = refs[n_in + n_sub] if has_ctx else None
    ctx_mb = refs[2 * n_in + n_sub] if has_ctx else None
    mod_ref, wout_ref, gpm_ref, gpre_ref, wup_ref, wdn_ref, gpl_ref, o_ref = refs[3 * n_in:]
    mods = [_sub_mod(mod_ref, has_ctx, i) for i in range(n_sub)]

    def out_proj(i):
        mix = jnp.concatenate([_sub_tile(ma_refs, ctx_ma, i), _sub_tile(mb_refs, ctx_mb, i)], axis=1)
        return jnp.dot(mix, wout_ref[...], preferred_element_type=F32)

    def mixer_residual(i, mix):
        mod = mods[i]
        h = _sub_tile(h_refs, ctx_h, i) + mod[2:3, :] * _rms(mix, gpm_ref[...])
        u = (_rms(h, gpre_ref[...]) * (1 + mod[4:5, :]) + mod[3:4, :]).astype(BF16)
        return h, u

    def mlp_chunk(u, y, c0):
        a = jnp.dot(u, wup_ref[:, c0:c0 + FF_CHUNK], preferred_element_type=F32)
        a = jnp.square(jnp.maximum(a, 0.0)).astype(BF16)
        part = jnp.dot(a, wdn_ref[c0:c0 + FF_CHUNK, :], preferred_element_type=F32)
        return part if y is None else y + part

    def mlp_residual(i, h, y):
        o_ref[i * TOK_TILE:(i + 1) * TOK_TILE, :] = h + mods[i][5:6, :] * _rms(y, gpl_ref[...])

    hu = [None] * n_sub
    mix_prev = None
    for i in range(n_sub):
        mix = out_proj(i)
        if i > 0:
            hu[i - 1] = mixer_residual(i - 1, mix_prev)
        mix_prev = mix
    for i in range(n_sub):
        y = None
        for c, c0 in enumerate(range(0, D_FF, FF_CHUNK)):
            if i == 0 and c == 0:
                hu[n_sub - 1] = mixer_residual(n_sub - 1, mix_prev)
            y = mlp_chunk(hu[i][1], y, c0)
            if i > 0 and c == 0:
                mlp_residual(i - 1, hu[i - 1][0], y_prev)
        y_prev = y
    mlp_residual(n_sub - 1, hu[n_sub - 1][0], y_prev)


def _post(h_lat, h_ctx, lat_tile_off, n_tiles, mix_a, mix_b, ctx_mix_a, ctx_mix_b, mods, layer, w_out,
          g_post_mix, g_pre_mlp, w_up, w_down, g_post_mlp):
    bsz = h_lat.shape[0]
    has_ctx = h_ctx is not None
    n_sub = max(s for s in POST_SUBS if n_tiles % s == 0)
    const = lambda b, t: (0, 0)
    resident = lambda rows, cols: pl.BlockSpec((None, rows, cols), lambda b, t: (layer, 0, 0),
                                               pipeline_mode=pl.Buffered(1))
    ctx_args = lambda v: [v] if has_ctx else []
    args = ([h_lat] * n_sub + ctx_args(h_ctx) + [mix_a] * n_sub + ctx_args(ctx_mix_a)
            + [mix_b] * n_sub + ctx_args(ctx_mix_b))
    return pl.pallas_call(
        functools.partial(_post_kernel, n_sub=n_sub, has_ctx=has_ctx),
        grid=(bsz, n_tiles // n_sub),
        in_specs=(_stream_specs(n_sub, D_MODEL, lat_tile_off, has_ctx)
                  + _stream_specs(n_sub, Q_COLS, 0, has_ctx)
                  + _stream_specs(n_sub, Q_COLS, 0, has_ctx) + [
            pl.BlockSpec((None, 2, N_MOD, D_MODEL), lambda b, t: (b, 0, 0, 0)),
            resident(D_MODEL, D_MODEL),
            pl.BlockSpec((1, D_MODEL), const),
            pl.BlockSpec((1, D_MODEL), const),
            resident(D_MODEL, D_FF),
            resident(D_FF, D_MODEL),
            pl.BlockSpec((1, D_MODEL), const),
        ]),
        out_specs=pl.BlockSpec((None, n_sub * TOK_TILE, D_MODEL), lambda b, t: (b, t, 0)),
        out_shape=jax.ShapeDtypeStruct((bsz, n_tiles * TOK_TILE, D_MODEL), F32),
        compiler_params=pltpu.CompilerParams(
            dimension_semantics=("parallel", "parallel"), vmem_limit_bytes=VMEM_LIMIT),
        name="post",
    )(*args, mods, w_out, g_post_mix, g_pre_mlp, w_up, w_down, g_post_mlp)


def _rope_tables(n_ctx, n_lat):
    rows = n_lat // GRID_W
    row_ids = jnp.repeat(jnp.arange(rows, dtype=jnp.int32), GRID_W).astype(F32)
    col_ids = jnp.tile(jnp.arange(GRID_W, dtype=jnp.int32), rows).astype(F32)
    axis_dim = HEAD_DIM // 2
    inv = ROPE_THETA ** (-jnp.arange(0, axis_dim, 2, dtype=F32) / axis_dim)
    ang_r = row_ids[:, None] * inv[None, :]
    ang_c = col_ids[:, None] * inv[None, :]
    cos_r, sin_r, cos_c, sin_c = jnp.cos(ang_r), jnp.sin(ang_r), jnp.cos(ang_c), jnp.sin(ang_c)
    zero = jnp.zeros_like(sin_r)
    cos_h = jnp.concatenate([cos_r, cos_r, cos_c, cos_c], axis=-1)
    sa_h = jnp.concatenate([-sin_r, zero, -sin_c, zero], axis=-1)
    sb_h = jnp.concatenate([zero, sin_r, zero, sin_c], axis=-1)

    def full(t, ctx_value):
        t = jnp.tile(t, (1, LANES // HEAD_DIM))
        return jnp.concatenate([jnp.full((n_ctx, LANES), ctx_value, F32), t], axis=0)

    return full(cos_h, 1.0), full(sa_h, 0.0), full(sb_h, 0.0)


def _head_ones(n_cols):
    head = jnp.arange(n_cols, dtype=jnp.int32) // HEAD_DIM
    return (head[:, None] == head[None, :]).astype(BF16)


def kernel(x, c, ctx, c_ctx, w_ada, b_ada, g_pre_mix, g_post_mix, g_pre_mlp, g_post_mlp,
           w_in, q_norm, k_norm, sink, w_out, w_up, w_down):
    bsz, n_lat, d = x.shape
    n_ctx = ctx.shape[1]
    depth = w_in.shape[0]
    n_tok = n_ctx + n_lat
    assert d == D_MODEL and n_ctx == TOK_TILE and n_lat % TOK_TILE == 0 and bsz + 1 <= ADA_ROWS
    assert w_in.shape[2] == IN_COLS and w_up.shape[2] == D_FF

    cc = jnp.concatenate([c, c_ctx[None, :], jnp.zeros((ADA_ROWS - bsz - 1, d), F32)], axis=0)
    mod_all = _ada(cc, w_ada, b_ada)
    mod_all = mod_all.reshape(depth, ADA_ROWS, N_MOD, d)
    mod_ctx = jnp.broadcast_to(mod_all[:, bsz:bsz + 1], (depth, bsz, N_MOD, d))
    mods = jnp.stack([mod_ctx, mod_all[:, :bsz]], axis=2)

    rope = _rope_tables(n_ctx, n_lat)
    rope_t = tuple(t[:, :HEAD_DIM].T for t in rope)
    ek = _head_ones(KV_COLS)
    eh = _head_ones(KV_COLS)
    row = lambda v: v.reshape(1, -1)

    w_in_b = w_in.astype(BF16)
    w_out_b = w_out.astype(BF16)
    w_up_b = w_up.astype(BF16)
    w_down_b = w_down.astype(BF16)

    ctx_tiles = n_ctx // TOK_TILE
    assert depth >= 1 and ctx_tiles == 1
    ctx_qblk = n_ctx // Q_BLOCK
    lat_qblk = n_lat // Q_BLOCK

    h_lat, h_ctx = x, ctx
    lat_off = 0
    for l in range(depth):
        last = l == depth - 1
        gq_t = jnp.broadcast_to(q_norm[l][:, None], (HEAD_DIM, TOK_TILE))
        gk = row(jnp.tile(k_norm[l], N_KV))
        sink_rows = jnp.repeat(sink[l].reshape(N_KV, 1, GROUP) * LOG2_E, Q_BLOCK, axis=2)

        qa, kda, vta, qb, kdb, vtb = _proj(h_lat, h_ctx, 0, n_tok // TOK_TILE, mods[l],
                                           row(g_pre_mix[l]), w_in_b, l, gq_t, gk, ek, rope, rope_t)
        bound_a = (HEAD_DIM * Q_SCALE * BOUND_SLACK * jnp.max(jnp.abs(q_norm[l]))
                   * jnp.max(jnp.abs(k_norm[l]))).reshape(1)
        oa = _attention(qa, kda, vta, None, eh, bound_a, q_blk_off=ctx_qblk, n_qblk=lat_qblk,
                        n_ctx=n_ctx, n_full=n_tok, band=False, q_per_step=Q_PER_STEP_GLOBAL,
                        name="attn_global")
        ob = _attention(qb, kdb, vtb, sink_rows, eh, None, q_blk_off=ctx_qblk, n_qblk=lat_qblk,
                        n_ctx=n_ctx, n_full=n_ctx, band=True, q_per_step=Q_PER_STEP_WINDOW,
                        name="attn_window")
        post = functools.partial(_post, mods=mods[l], layer=l, w_out=w_out_b, g_post_mix=row(g_post_mix[l]),
                                 g_pre_mlp=row(g_pre_mlp[l]), w_up=w_up_b, w_down=w_down_b,
                                 g_post_mlp=row(g_post_mlp[l]))
        if not last:
            oac = _attention(qa, kda, vta, None, eh, bound_a, q_blk_off=0, n_qblk=ctx_qblk, n_ctx=n_ctx,
                             n_full=n_ctx, band=False, q_per_step=ctx_qblk, name="attn_ctx_global")
            obc = _attention(qb, kdb, vtb, sink_rows, eh, None, q_blk_off=0, n_qblk=ctx_qblk, n_ctx=n_ctx,
                             n_full=n_ctx, band=False, q_per_step=ctx_qblk, name="attn_ctx_sink")
            assert h_ctx is not None
            h_lat = post(h_lat, h_ctx, lat_off, n_tok // TOK_TILE, oa, ob, oac, obc)
            h_ctx, lat_off = None, ctx_tiles
        else:
            if h_ctx is not None:
                h_lat = post(h_lat, None, 0, n_lat // TOK_TILE, oa, ob, None, None)
            else:
                h_lat = post(h_lat, None, lat_off, n_lat // TOK_TILE, oa, ob, None, None)
    return h_lat
```

```python
import functools

import jax
import jax.numpy as jnp
from jax import lax
from jax.experimental import pallas as pl
from jax.experimental.pallas import tpu as pltpu

F32 = jnp.float32
BF16 = jnp.bfloat16

D_MODEL = 1024
HEAD_DIM = 64
N_HEADS = 8
N_KV = 2
GROUP = N_HEADS // N_KV
Q_COLS = N_HEADS * HEAD_DIM
KV_COLS = N_KV * HEAD_DIM
IN_COLS = 2 * (Q_COLS + 2 * KV_COLS)
D_FF = 4 * D_MODEL
GRID_W = 64
WINDOW = 128
Q_BLOCK = 128
ROPE_THETA = 10000.0
EPS = 1e-6
NEG_BIG = -1e30
N_MOD = 6
ROPE_SHIFT = HEAD_DIM // 4
LOG2_E = 1.4426950408889634
Q_SCALE = HEAD_DIM ** -0.5 * LOG2_E

LANES = 128
TOK_TILE = 256
BF16_SUBLANES = 16
PROJ_SUBS = (1, 3, 9)
POST_SUBS = (1, 2, 3, 4)
VT_ROWS = HEAD_DIM + BF16_SUBLANES
KEY_CHUNK = 4096
Q_PER_STEP_GLOBAL = 8
Q_PER_STEP_WINDOW = 16
LOGIT_BOUND_LIMIT = 48.0
BOUND_SLACK = 1.01
FF_CHUNK = 1024
ADA_ROWS = 32
ADA_COLS = 1536
VMEM_LIMIT = 56 * 1024 * 1024


def _rms(x, g):
    return x * lax.rsqrt(jnp.mean(x * x, axis=-1, keepdims=True) + EPS) * g


def _ada_kernel(c_ref, w_ref, b_ref, o_ref):
    c = c_ref[...]
    a = c * jax.nn.sigmoid(c)
    w = w_ref[...]
    a_hi = a.astype(BF16)
    a_lo = (a - a_hi.astype(F32)).astype(BF16)
    w_hi = w.astype(BF16)
    w_lo = (w - w_hi.astype(F32)).astype(BF16)
    acc = jnp.dot(a_hi, w_hi, preferred_element_type=F32)
    acc += jnp.dot(a_lo, w_hi, preferred_element_type=F32)
    acc += jnp.dot(a_hi, w_lo, preferred_element_type=F32)
    o_ref[...] = acc + b_ref[...]


def _ada(cc, w_ada, b_ada):
    depth = w_ada.shape[0]
    n_out = w_ada.shape[2]
    return pl.pallas_call(
        _ada_kernel,
        grid=(depth, n_out // ADA_COLS),
        in_specs=[
            pl.BlockSpec((ADA_ROWS, D_MODEL), lambda l, j: (0, 0)),
            pl.BlockSpec((None, D_MODEL, ADA_COLS), lambda l, j: (l, 0, j)),
            pl.BlockSpec((None, 1, ADA_COLS), lambda l, j: (l, 0, j)),
        ],
        out_specs=pl.BlockSpec((None, ADA_ROWS, ADA_COLS), lambda l, j: (l, 0, j)),
        out_shape=jax.ShapeDtypeStruct((depth, ADA_ROWS, n_out), F32),
        compiler_params=pltpu.CompilerParams(
            dimension_semantics=("arbitrary", "arbitrary"), vmem_limit_bytes=VMEM_LIMIT),
        name="ada",
    )(cc, w_ada, b_ada.reshape(depth, 1, n_out))


def _stream_specs(n_sub, block_cols, lat_tile_off, has_ctx):
    specs = []
    for i in range(n_sub):
        if has_ctx:
            idx = lambda b, t, i=i: (b, jnp.maximum(n_sub * t + i - 1, 0), 0)
        else:
            idx = lambda b, t, i=i: (b, lat_tile_off + n_sub * t + i, 0)
        specs.append(pl.BlockSpec((None, TOK_TILE, block_cols), idx))
    if has_ctx:
        specs.append(pl.BlockSpec((None, TOK_TILE, block_cols), lambda b, t: (b, 0, 0)))
    return specs


def _sub_tile(refs, ctx_ref, i):
    v = refs[i][...]
    if ctx_ref is not None and i == 0:
        v = jnp.where(pl.program_id(1) == 0, ctx_ref[...], v)
    return v


def _sub_mod(mod_ref, ctx_first, i):
    if ctx_first and i == 0:
        return jnp.where(pl.program_id(1) == 0, mod_ref[0], mod_ref[1])
    return mod_ref[1]


def _proj_kernel(*refs, n_sub, has_ctx):
    h_refs = refs[:n_sub]
    refs = refs[n_sub:]
    ctx_ref = None
    if has_ctx:
        ctx_ref, refs = refs[0], refs[1:]
    (mod_ref, gpre_ref, w_ref, gqt_ref, gk_ref, ek_ref, cos_ref, sa_ref, sb_ref, cost_ref, sat_ref, sbt_ref,
     qa_ref, kda_ref, vta_ref, qb_ref, kdb_ref, vtb_ref, nb_ref) = refs
    kv = 2 * KV_COLS
    sq_q, sq_k = [], []

    def prenorm(i):
        mod = _sub_mod(mod_ref, True, i)
        x = _sub_tile(h_refs, ctx_ref, i)
        return (_rms(x, gpre_ref[...]) * (1 + mod[1:2, :]) + mod[0:1, :]).astype(BF16)

    def stages(i, u):
        rows = slice(i * TOK_TILE, (i + 1) * TOK_TILE)
        cos = cos_ref[rows, :]
        sa = sa_ref[rows, :]
        sb = sb_ref[rows, :]

        def rope(t):
            up = pltpu.roll(t, LANES - ROPE_SHIFT, 1)
            dn = pltpu.roll(t, ROPE_SHIFT, 1)
            return t * cos + up * sa + dn * sb

        def head_norm(t, e_ref, g_ref):
            ss = jnp.dot((t * t).astype(BF16), e_ref[...], preferred_element_type=F32)
            return t * lax.rsqrt(ss * (1.0 / HEAD_DIM) + EPS) * g_ref[...]

        def store_q(q, qt_ref, head_normed):
            q_t = q.T.reshape(N_HEADS, HEAD_DIM, TOK_TILE)
            ss = jnp.sum(q_t * q_t, axis=1, keepdims=True)
            if head_normed:
                q_t = q_t * lax.rsqrt(ss * (1.0 / HEAD_DIM) + EPS) * gqt_ref[...][None]
            else:
                sq_q.append(jnp.max(ss, axis=0))
            up = jnp.concatenate([q_t[:, ROPE_SHIFT:, :], q_t[:, :ROPE_SHIFT, :]], axis=1)
            dn = jnp.concatenate([q_t[:, -ROPE_SHIFT:, :], q_t[:, :-ROPE_SHIFT, :]], axis=1)
            q_t = q_t * cost_ref[:, rows][None] + up * sat_ref[:, rows][None] + dn * sbt_ref[:, rows][None]
            qt_ref[:, :, rows] = (q_t * Q_SCALE).astype(BF16)

        def store_k(k, k_ref, report_norms=False):
            if report_norms:
                ss = jnp.dot((k * k).astype(BF16), ek_ref[...], preferred_element_type=F32)
                sq_k.append(jnp.max(ss, axis=0, keepdims=True))
            k_ref[rows, :] = rope(k).astype(BF16)

        def store_vt(v, vt_ref):
            v_t = v.T.astype(BF16)
            ones = jnp.ones((VT_ROWS - HEAD_DIM, TOK_TILE), BF16)
            for h in range(N_KV):
                vt_ref[h, 0:HEAD_DIM, rows] = v_t[h * HEAD_DIM:(h + 1) * HEAD_DIM, :]
                vt_ref[h, HEAD_DIM:VT_ROWS, rows] = ones

        def project(c0, n):
            return lambda: jnp.dot(u, w_ref[:, c0:c0 + n], preferred_element_type=F32)

        def epi_qa(z):
            store_q(z, qa_ref, head_normed=True)

        def epi_kva(z):
            store_k(head_norm(z[:, 0:KV_COLS], ek_ref, gk_ref), kda_ref)
            store_vt(z[:, KV_COLS:kv], vta_ref)

        def epi_qb(z):
            store_q(z, qb_ref, head_normed=False)

        def epi_kvb(z):
            store_k(z[:, 0:KV_COLS], kdb_ref, report_norms=True)
            store_vt(z[:, KV_COLS:kv], vtb_ref)

        return [(project(0, Q_COLS), epi_qa), (project(Q_COLS, kv), epi_kva),
                (project(Q_COLS + kv, Q_COLS), epi_qb), (project(2 * Q_COLS + kv, kv), epi_kvb)]

    u_next = prenorm(0)
    pending = []
    for i in range(n_sub):
        u = u_next
        results = []
        for g, (matmul, epilogue) in enumerate(stages(i, u)):
            results.append((epilogue, matmul()))
            if g == 0 and i + 1 < n_sub:
                u_next = prenorm(i + 1)
            if pending:
                epi, z = pending.pop(0)
                epi(z)
        for epi, z in pending:
            epi(z)
        pending = results
    for epi, z in pending:
        epi(z)
    q_max = functools.reduce(jnp.maximum, sq_q)
    k_max = functools.reduce(jnp.maximum, sq_k)
    nb_ref[...] = jnp.concatenate([q_max, jnp.concatenate([k_max, k_max], axis=1),
                                   jnp.zeros((6, TOK_TILE), F32)], axis=0)


def _proj(h_lat, h_ctx, lat_tile_off, n_tiles, mods, g_pre, w_in, layer, gq_t, gk, ek, rope, rope_t):
    bsz = h_lat.shape[0]
    has_ctx = h_ctx is not None
    n_sub = max(s for s in PROJ_SUBS if n_tiles % s == 0)
    n_tok = n_tiles * TOK_TILE
    rows = n_sub * TOK_TILE
    const = lambda b, t: (0, 0)
    tok = lambda b, t: (b, t, 0)
    q_shape = jax.ShapeDtypeStruct((bsz, N_HEADS, HEAD_DIM, n_tok), BF16)
    kd_shape = jax.ShapeDtypeStruct((bsz, n_tok, KV_COLS), BF16)
    vt_shape = jax.ShapeDtypeStruct((bsz, N_KV, VT_ROWS, n_tok), BF16)
    q_spec = pl.BlockSpec((None, N_HEADS, HEAD_DIM, rows), lambda b, t: (b, 0, 0, t))
    kd_spec = pl.BlockSpec((None, rows, KV_COLS), tok)
    vt_spec = pl.BlockSpec((None, N_KV, VT_ROWS, rows), lambda b, t: (b, 0, 0, t))
    rope_spec = pl.BlockSpec((rows, LANES), lambda b, t: (t, 0))
    rope_t_spec = pl.BlockSpec((HEAD_DIM, rows), lambda b, t: (0, t))
    nb_shape = jax.ShapeDtypeStruct((bsz, n_tiles // n_sub, 8, TOK_TILE), F32)
    nb_spec = pl.BlockSpec((None, None, 8, TOK_TILE), lambda b, t: (b, t, 0, 0))
    h_args = [h_lat] * n_sub + ([h_ctx] if has_ctx else [])
    return pl.pallas_call(
        functools.partial(_proj_kernel, n_sub=n_sub, has_ctx=has_ctx),
        grid=(bsz, n_tiles // n_sub),
        in_specs=_stream_specs(n_sub, D_MODEL, lat_tile_off, has_ctx) + [
            pl.BlockSpec((None, 2, N_MOD, D_MODEL), lambda b, t: (b, 0, 0, 0)),
            pl.BlockSpec((1, D_MODEL), const),
            pl.BlockSpec((None, D_MODEL, IN_COLS), lambda b, t: (layer, 0, 0)),
            pl.BlockSpec((HEAD_DIM, TOK_TILE), const),
            pl.BlockSpec((1, KV_COLS), const),
            pl.BlockSpec((KV_COLS, KV_COLS), const),
            rope_spec, rope_spec, rope_spec, rope_t_spec, rope_t_spec, rope_t_spec,
        ],
        out_specs=[q_spec, kd_spec, vt_spec, q_spec, kd_spec, vt_spec, nb_spec],
        out_shape=[q_shape, kd_shape, vt_shape, q_shape, kd_shape, vt_shape, nb_shape],
        compiler_params=pltpu.CompilerParams(
            dimension_semantics=("parallel", "parallel"), vmem_limit_bytes=VMEM_LIMIT),
        name="proj",
    )(*h_args, mods, g_pre, w_in, gq_t, gk, ek, *rope, *rope_t)


def _band_start(blk, n_ctx, n_tok):
    return jnp.minimum(n_ctx + (blk - 1) * Q_BLOCK, n_tok - (Q_BLOCK + 2 * WINDOW))


def _window_bias_variants(n_ctx, n_tok, n_lat_blk):
    n_band = Q_BLOCK + 2 * WINDOW
    kj = jnp.arange(n_band, dtype=jnp.int32)[:, None]
    qr = jnp.arange(GROUP * Q_BLOCK, dtype=jnp.int32)[None, :] % Q_BLOCK
    variants = []
    for blk in (0, 1, n_lat_blk - 1):
        kpos = kj + (_band_start(blk, n_ctx, n_tok) - n_ctx)
        qpos = qr + blk * Q_BLOCK
        valid = (jnp.abs(kpos - qpos) <= WINDOW) & (kpos >= 0)
        variants.append(jnp.where(valid, 0.0, NEG_BIG).astype(F32))
    return jnp.stack(variants)


def _attn_kernel(*refs, n_ctx, n_full, band, use_sink, n_tok, n_lat_blk, q_row0, q_per_step):
    refs = list(refs)
    qt_ref, k_ref, vt_ref = refs[:3]
    rest = refs[3:]
    sink_ref = rest.pop(0) if use_sink else None
    bias_ref = rest.pop(0) if band else None
    bound_ref, o_ref = rest
    nq = GROUP * Q_BLOCK
    n_band = Q_BLOCK + 2 * WINDOW if band else 0
    chunks = [(c0, min(c0 + KEY_CHUNK, n_full), False) for c0 in range(0, n_full, KEY_CHUNK)]
    if band:
        chunks.append((n_full, n_full + n_band, True))
    kv_head = pl.program_id(1)
    step = pl.program_id(2)

    def band_start(blk):
        return pl.multiple_of(_band_start(blk, n_ctx, n_tok), LANES)

    def extend_q(blk):
        cols = pl.ds(pl.multiple_of(q_row0 + blk * Q_BLOCK, Q_BLOCK), Q_BLOCK)
        top = jnp.concatenate([qt_ref[g, :, cols] for g in range(GROUP)], axis=1)
        zero = jnp.zeros_like(top)
        return jnp.where(kv_head == 0, jnp.concatenate([top, zero], axis=0),
                         jnp.concatenate([zero, top], axis=0))

    def logits(qm, blk, chunk):
        c0, c1, in_band = chunk
        if in_band:
            variant = jnp.where(blk == 0, 0, jnp.where(blk >= n_lat_blk - 1, 2, 1))
            return jnp.dot(k_ref[pl.ds(band_start(blk), n_band), :], qm,
                           preferred_element_type=F32) + bias_ref[variant]
        return jnp.dot(k_ref[c0:c1, :], qm, preferred_element_type=F32)

    def values_t(blk, chunk):
        c0, c1, in_band = chunk
        if in_band:
            return vt_ref[:, pl.ds(band_start(blk), n_band)]
        return vt_ref[:, c0:c1]

    def column_max(qm, blk):
        mx = None
        for chunk in chunks:
            s = logits(qm, blk, chunk)
            cm = jnp.max(s.reshape(s.shape[0] // 8, 8, nq), axis=0)
            mx = cm if mx is None else jnp.maximum(mx, cm)
        return jnp.max(mx, axis=0, keepdims=True)

    def finalize(t, acc, m):
        l = acc[HEAD_DIM:HEAD_DIM + 1, :]
        if use_sink:
            l = l + jnp.exp2(sink_ref[...] - m)
        o_t = acc[0:HEAD_DIM, :] / l
        o_t = jnp.concatenate([o_t[:, g * Q_BLOCK:(g + 1) * Q_BLOCK] for g in range(GROUP)], axis=0)
        o_ref[pl.ds(pl.multiple_of(t * Q_BLOCK, Q_BLOCK), Q_BLOCK), :] = o_t.T.astype(BF16)

    def attend(blocks):
        items = [(i, c) for i in range(len(blocks)) for c in range(len(chunks))]
        blks = [step * q_per_step + t for t, _ in blocks]
        qms = [extend_q(blk) for blk in blks]
        s_next = logits(qms[0], blks[0], chunks[0])
        acc = [None] * len(blocks)
        pv_prev = None

        def retire(prev):
            i, c, pv = prev
            acc[i] = pv if acc[i] is None else acc[i] + pv
            if c == len(chunks) - 1:
                finalize(blocks[i][0], acc[i], blocks[i][1])

        for n, (i, c) in enumerate(items):
            s = s_next
            if n + 1 < len(items):
                i2, c2 = items[n + 1]
                s_next = logits(qms[i2], blks[i2], chunks[c2])
            p = jnp.exp2(s - blocks[i][1]).astype(BF16)
            pv = jnp.dot(values_t(blks[i], chunks[c]), p, preferred_element_type=F32)
            if pv_prev is not None:
                retire(pv_prev)
            pv_prev = (i, c, pv)
        retire(pv_prev)

    def with_sink(m):
        return jnp.maximum(m, sink_ref[...]) if use_sink else m

    bound = bound_ref[pl.program_id(0)]
    one_pass = bound <= LOGIT_BOUND_LIMIT

    @pl.when(one_pass)
    def _():
        m = with_sink(jnp.full((1, nq), bound, F32))
        attend([(t, m) for t in range(q_per_step)])

    @pl.when(jnp.logical_not(one_pass))
    def _():
        def one_block(t, carry):
            blk = step * q_per_step + t
            attend([(t, with_sink(column_max(extend_q(blk), blk)))])
            return carry

        lax.fori_loop(0, q_per_step, one_block, 0)


def _attention(q, kd, vt, sink_rows, logit_bound, *, q_blk_off, n_qblk, n_ctx, n_full, band,
               q_per_step, name):
    bsz, n_tok, _ = kd.shape
    q_per_step = min(q_per_step, n_qblk)
    n_rows = n_tok if (band or n_full == n_tok) else max(n_full, (q_blk_off + n_qblk) * Q_BLOCK)
    assert n_qblk % q_per_step == 0
    use_sink = sink_rows is not None
    nq = GROUP * Q_BLOCK
    in_specs = [
        pl.BlockSpec((None, GROUP, HEAD_DIM, n_rows), lambda b, h, j: (b, h, 0, 0)),
        pl.BlockSpec((None, n_rows, KV_COLS), lambda b, h, j: (b, 0, 0)),
        pl.BlockSpec((None, None, VT_ROWS, n_rows), lambda b, h, j: (b, h, 0, 0)),
    ]
    args = [q, kd, vt]
    if use_sink:
        in_specs.append(pl.BlockSpec((None, 1, nq), lambda b, h, j: (h, 0, 0)))
        args.append(sink_rows)
    if band:
        bias = _window_bias_variants(n_ctx, n_tok, n_qblk)
        in_specs.append(pl.BlockSpec(bias.shape, lambda b, h, j: (0, 0, 0)))
        args.append(bias)
    in_specs.append(pl.BlockSpec(memory_space=pltpu.SMEM))
    args.append(logit_bound)
    kern = functools.partial(_attn_kernel, n_ctx=n_ctx, n_full=n_full, band=band, use_sink=use_sink,
                             n_tok=n_tok, n_lat_blk=n_qblk, q_row0=q_blk_off * Q_BLOCK,
                             q_per_step=q_per_step)
    rows = q_per_step * Q_BLOCK
    return pl.pallas_call(
        kern,
        grid=(bsz, N_KV, n_qblk // q_per_step),
        in_specs=in_specs,
        out_specs=pl.BlockSpec((None, rows, GROUP * HEAD_DIM), lambda b, h, j: (b, j, h)),
        out_shape=jax.ShapeDtypeStruct((bsz, n_qblk * Q_BLOCK, Q_COLS), BF16),
        compiler_params=pltpu.CompilerParams(
            dimension_semantics=("parallel", "parallel", "arbitrary"), vmem_limit_bytes=VMEM_LIMIT),
        name=name,
    )(*args)


def _post_kernel(*refs, n_sub, has_ctx):
    n_in = n_sub + (1 if has_ctx else 0)
    h_refs, ma_refs, mb_refs = refs[:n_sub], refs[n_in:n_in + n_sub], refs[2 * n_in:2 * n_in + n_sub]
    ctx_h = refs[n_sub] if has_ctx else None
    ctx_ma = refs[n_in + n_sub] if has_ctx else None
    ctx_mb = refs[2 * n_in + n_sub] if has_ctx else None
    mod_ref, wout_ref, gpm_ref, gpre_ref, wup_ref, wdn_ref, gpl_ref, o_ref = refs[3 * n_in:]
    mods = [_sub_mod(mod_ref, has_ctx, i) for i in range(n_sub)]

    def out_proj(i):
        mix = jnp.concatenate([_sub_tile(ma_refs, ctx_ma, i), _sub_tile(mb_refs, ctx_mb, i)], axis=1)
        return jnp.dot(mix, wout_ref[...], preferred_element_type=F32)

    def mixer_residual(i, mix):
        mod = mods[i]
        h = _sub_tile(h_refs, ctx_h, i) + mod[2:3, :] * _rms(mix, gpm_ref[...])
        u = (_rms(h, gpre_ref[...]) * (1 + mod[4:5, :]) + mod[3:4, :]).astype(BF16)
        return h, u

    def mlp_chunk(u, y, c0):
        a = jnp.dot(u, wup_ref[:, c0:c0 + FF_CHUNK], preferred_element_type=F32)
        a = jnp.square(jnp.maximum(a, 0.0)).astype(BF16)
        part = jnp.dot(a, wdn_ref[c0:c0 + FF_CHUNK, :], preferred_element_type=F32)
        return part if y is None else y + part

    def mlp_residual(i, h, y):
        o_ref[i * TOK_TILE:(i + 1) * TOK_TILE, :] = h + mods[i][5:6, :] * _rms(y, gpl_ref[...])

    hu = [None] * n_sub
    mix_prev = None
    for i in range(n_sub):
        mix = out_proj(i)
        if i > 0:
            hu[i - 1] = mixer_residual(i - 1, mix_prev)
        mix_prev = mix
    for i in range(n_sub):
        y = None
        for c, c0 in enumerate(range(0, D_FF, FF_CHUNK)):
            if i == 0 and c == 0:
                hu[n_sub - 1] = mixer_residual(n_sub - 1, mix_prev)
            y = mlp_chunk(hu[i][1], y, c0)
            if i > 0 and c == 0:
                mlp_residual(i - 1, hu[i - 1][0], y_prev)
        y_prev = y
    mlp_residual(n_sub - 1, hu[n_sub - 1][0], y_prev)


def _post(h_lat, h_ctx, lat_tile_off, n_tiles, mix_a, mix_b, ctx_mix_a, ctx_mix_b, mods, layer, w_out,
          g_post_mix, g_pre_mlp, w_up, w_down, g_post_mlp):
    bsz = h_lat.shape[0]
    has_ctx = h_ctx is not None
    n_sub = max(s for s in POST_SUBS if n_tiles % s == 0)
    const = lambda b, t: (0, 0)
    resident = lambda rows, cols: pl.BlockSpec((None, rows, cols), lambda b, t: (layer, 0, 0),
                                               pipeline_mode=pl.Buffered(1))
    ctx_args = lambda v: [v] if has_ctx else []
    args = ([h_lat] * n_sub + ctx_args(h_ctx) + [mix_a] * n_sub + ctx_args(ctx_mix_a)
            + [mix_b] * n_sub + ctx_args(ctx_mix_b))
    return pl.pallas_call(
        functools.partial(_post_kernel, n_sub=n_sub, has_ctx=has_ctx),
        grid=(bsz, n_tiles // n_sub),
        in_specs=(_stream_specs(n_sub, D_MODEL, lat_tile_off, has_ctx)
                  + _stream_specs(n_sub, Q_COLS, 0, has_ctx)
                  + _stream_specs(n_sub, Q_COLS, 0, has_ctx) + [
            pl.BlockSpec((None, 2, N_MOD, D_MODEL), lambda b, t: (b, 0, 0, 0)),
            resident(D_MODEL, D_MODEL),
            pl.BlockSpec((1, D_MODEL), const),
            pl.BlockSpec((1, D_MODEL), const),
            resident(D_MODEL, D_FF),
            resident(D_FF, D_MODEL),
            pl.BlockSpec((1, D_MODEL), const),
        ]),
        out_specs=pl.BlockSpec((None, n_sub * TOK_TILE, D_MODEL), lambda b, t: (b, t, 0)),
        out_shape=jax.ShapeDtypeStruct((bsz, n_tiles * TOK_TILE, D_MODEL), F32),
        compiler_params=pltpu.CompilerParams(
            dimension_semantics=("parallel", "parallel"), vmem_limit_bytes=VMEM_LIMIT),
        name="post",
    )(*args, mods, w_out, g_post_mix, g_pre_mlp, w_up, w_down, g_post_mlp)


def _rope_tables(n_ctx, n_lat):
    rows = n_lat // GRID_W
    row_ids = jnp.repeat(jnp.arange(rows, dtype=jnp.int32), GRID_W).astype(F32)
    col_ids = jnp.tile(jnp.arange(GRID_W, dtype=jnp.int32), rows).astype(F32)
    axis_dim = HEAD_DIM // 2
    inv = ROPE_THETA ** (-jnp.arange(0, axis_dim, 2, dtype=F32) / axis_dim)
    ang_r = row_ids[:, None] * inv[None, :]
    ang_c = col_ids[:, None] * inv[None, :]
    cos_r, sin_r, cos_c, sin_c = jnp.cos(ang_r), jnp.sin(ang_r), jnp.cos(ang_c), jnp.sin(ang_c)
    zero = jnp.zeros_like(sin_r)
    cos_h = jnp.concatenate([cos_r, cos_r, cos_c, cos_c], axis=-1)
    sa_h = jnp.concatenate([-sin_r, zero, -sin_c, zero], axis=-1)
    sb_h = jnp.concatenate([zero, sin_r, zero, sin_c], axis=-1)

    def full(t, ctx_value):
        t = jnp.tile(t, (1, LANES // HEAD_DIM))
        return jnp.concatenate([jnp.full((n_ctx, LANES), ctx_value, F32), t], axis=0)

    return full(cos_h, 1.0), full(sa_h, 0.0), full(sb_h, 0.0)


def _head_ones(n_cols):
    head = jnp.arange(n_cols, dtype=jnp.int32) // HEAD_DIM
    return (head[:, None] == head[None, :]).astype(BF16)


def kernel(x, c, ctx, c_ctx, w_ada, b_ada, g_pre_mix, g_post_mix, g_pre_mlp, g_post_mlp,
           w_in, q_norm, k_norm, sink, w_out, w_up, w_down):
    bsz, n_lat, d = x.shape
    n_ctx = ctx.shape[1]
    depth = w_in.shape[0]
    n_tok = n_ctx + n_lat
    assert d == D_MODEL and n_ctx == TOK_TILE and n_lat % TOK_TILE == 0 and bsz + 1 <= ADA_ROWS
    assert w_in.shape[2] == IN_COLS and w_up.shape[2] == D_FF

    cc = jnp.concatenate([c, c_ctx[None, :], jnp.zeros((ADA_ROWS - bsz - 1, d), F32)], axis=0)
    mod_all = _ada(cc, w_ada, b_ada)
    mod_all = mod_all.reshape(depth, ADA_ROWS, N_MOD, d)
    mod_ctx = jnp.broadcast_to(mod_all[:, bsz:bsz + 1], (depth, bsz, N_MOD, d))
    mods = jnp.stack([mod_ctx, mod_all[:, :bsz]], axis=2)

    rope = _rope_tables(n_ctx, n_lat)
    rope_t = tuple(t[:, :HEAD_DIM].T for t in rope)
    ek = _head_ones(KV_COLS)
    row = lambda v: v.reshape(1, -1)

    w_in_b = w_in.astype(BF16)
    w_out_b = w_out.astype(BF16)
    w_up_b = w_up.astype(BF16)
    w_down_b = w_down.astype(BF16)

    ctx_tiles = n_ctx // TOK_TILE
    assert depth >= 1 and ctx_tiles == 1
    ctx_qblk = n_ctx // Q_BLOCK
    lat_qblk = n_lat // Q_BLOCK

    h_lat, h_ctx = x, ctx
    lat_off = 0
    for l in range(depth):
        last = l == depth - 1
        gq_t = jnp.broadcast_to(q_norm[l][:, None], (HEAD_DIM, TOK_TILE))
        gk = row(jnp.tile(k_norm[l], N_KV))
        sink_rows = jnp.repeat(sink[l].reshape(N_KV, 1, GROUP) * LOG2_E, Q_BLOCK, axis=2)

        qa, kda, vta, qb, kdb, vtb, sq_b = _proj(h_lat, h_ctx, 0, n_tok // TOK_TILE, mods[l],
                                           row(g_pre_mix[l]), w_in_b, l, gq_t, gk, ek, rope, rope_t)
        bound_a = jnp.full((bsz,), HEAD_DIM * Q_SCALE * BOUND_SLACK, F32) * (
            jnp.max(jnp.abs(q_norm[l])) * jnp.max(jnp.abs(k_norm[l])))
        bound_b = (Q_SCALE * BOUND_SLACK) * jnp.sqrt(jnp.max(sq_b[:, :, 0, :], axis=(1, 2))
                                                     * jnp.max(sq_b[:, :, 1, :], axis=(1, 2)))
        oa = _attention(qa, kda, vta, None, bound_a, q_blk_off=ctx_qblk, n_qblk=lat_qblk,
                        n_ctx=n_ctx, n_full=n_tok, band=False, q_per_step=Q_PER_STEP_GLOBAL,
                        name="attn_global")
        ob = _attention(qb, kdb, vtb, sink_rows, bound_b, q_blk_off=ctx_qblk, n_qblk=lat_qblk,
                        n_ctx=n_ctx, n_full=n_ctx, band=True, q_per_step=Q_PER_STEP_WINDOW,
                        name="attn_window")
        post = functools.partial(_post, mods=mods[l], layer=l, w_out=w_out_b, g_post_mix=row(g_post_mix[l]),
                                 g_pre_mlp=row(g_pre_mlp[l]), w_up=w_up_b, w_down=w_down_b,
                                 g_post_mlp=row(g_post_mlp[l]))
        if not last:
            oac = _attention(qa, kda, vta, None, bound_a, q_blk_off=0, n_qblk=ctx_qblk, n_ctx=n_ctx,
                             n_full=n_ctx, band=False, q_per_step=ctx_qblk, name="attn_ctx_global")
            obc = _attention(qb, kdb, vtb, sink_rows, bound_b, q_blk_off=0, n_qblk=ctx_qblk, n_ctx=n_ctx,
                             n_full=n_ctx, band=False, q_per_step=ctx_qblk, name="attn_ctx_sink")
            assert h_ctx is not None
            h_lat = post(h_lat, h_ctx, lat_off, n_tok // TOK_TILE, oa, ob, oac, obc)
            h_ctx, lat_off = None, ctx_tiles
        else:
            if h_ctx is not None:
                h_lat = post(h_lat, None, 0, n_lat // TOK_TILE, oa, ob, None, None)
            else:
                h_lat = post(h_lat, None, lat_off, n_lat // TOK_TILE, oa, ob, None, None)
    return h_lat
```

```python
import functools

import jax
import jax.numpy as jnp
from jax import lax
from jax.experimental import pallas as pl
from jax.experimental.pallas import tpu as pltpu

F32 = jnp.float32
BF16 = jnp.bfloat16

D_MODEL = 1024
HEAD_DIM = 64
N_HEADS = 8
N_KV = 2
GROUP = N_HEADS // N_KV
Q_COLS = N_HEADS * HEAD_DIM
KV_COLS = N_KV * HEAD_DIM
IN_COLS = 2 * (Q_COLS + 2 * KV_COLS)
D_FF = 4 * D_MODEL
GRID_W = 64
WINDOW = 128
Q_BLOCK = 128
ROPE_THETA = 10000.0
EPS = 1e-6
NEG_BIG = -1e30
N_MOD = 6
ROPE_SHIFT = HEAD_DIM // 4
LOG2_E = 1.4426950408889634
Q_SCALE = HEAD_DIM ** -0.5 * LOG2_E

LANES = 128
TOK_TILE = 256
BF16_SUBLANES = 16
PROJ_SUBS = (1, 3, 9)
POST_SUBS = (1, 2, 3, 4)
VT_ROWS = HEAD_DIM + BF16_SUBLANES
KEY_CHUNK = 4096
Q_PER_STEP_GLOBAL = 8
Q_PER_STEP_WINDOW = 8
LOGIT_BOUND_LIMIT = 48.0
BOUND_SLACK = 1.01
FF_CHUNK = 1024
ADA_ROWS = 32
ADA_COLS = 1536
VMEM_LIMIT = 56 * 1024 * 1024


def _rms(x, g):
    return x * lax.rsqrt(jnp.mean(x * x, axis=-1, keepdims=True) + EPS) * g


def _ada_kernel(c_ref, w_ref, b_ref, o_ref):
    c = c_ref[...]
    a = c * jax.nn.sigmoid(c)
    w = w_ref[...]
    a_hi = a.astype(BF16)
    a_lo = (a - a_hi.astype(F32)).astype(BF16)
    w_hi = w.astype(BF16)
    w_lo = (w - w_hi.astype(F32)).astype(BF16)
    acc = jnp.dot(a_hi, w_hi, preferred_element_type=F32)
    acc += jnp.dot(a_lo, w_hi, preferred_element_type=F32)
    acc += jnp.dot(a_hi, w_lo, preferred_element_type=F32)
    o_ref[...] = acc + b_ref[...]


def _ada(cc, w_ada, b_ada):
    depth = w_ada.shape[0]
    n_out = w_ada.shape[2]
    return pl.pallas_call(
        _ada_kernel,
        grid=(depth, n_out // ADA_COLS),
        in_specs=[
            pl.BlockSpec((ADA_ROWS, D_MODEL), lambda l, j: (0, 0)),
            pl.BlockSpec((None, D_MODEL, ADA_COLS), lambda l, j: (l, 0, j)),
            pl.BlockSpec((None, 1, ADA_COLS), lambda l, j: (l, 0, j)),
        ],
        out_specs=pl.BlockSpec((None, ADA_ROWS, ADA_COLS), lambda l, j: (l, 0, j)),
        out_shape=jax.ShapeDtypeStruct((depth, ADA_ROWS, n_out), F32),
        compiler_params=pltpu.CompilerParams(
            dimension_semantics=("arbitrary", "arbitrary"), vmem_limit_bytes=VMEM_LIMIT),
        name="ada",
    )(cc, w_ada, b_ada.reshape(depth, 1, n_out))


def _stream_specs(n_sub, block_cols, lat_tile_off, has_ctx):
    specs = []
    for i in range(n_sub):
        if has_ctx:
            idx = lambda b, t, i=i: (b, jnp.maximum(n_sub * t + i - 1, 0), 0)
        else:
            idx = lambda b, t, i=i: (b, lat_tile_off + n_sub * t + i, 0)
        specs.append(pl.BlockSpec((None, TOK_TILE, block_cols), idx))
    if has_ctx:
        specs.append(pl.BlockSpec((None, TOK_TILE, block_cols), lambda b, t: (b, 0, 0)))
    return specs


def _sub_tile(refs, ctx_ref, i):
    v = refs[i][...]
    if ctx_ref is not None and i == 0:
        v = jnp.where(pl.program_id(1) == 0, ctx_ref[...], v)
    return v


def _sub_mod(mod_ref, ctx_first, i):
    if ctx_first and i == 0:
        return jnp.where(pl.program_id(1) == 0, mod_ref[0], mod_ref[1])
    return mod_ref[1]


def _proj_kernel(*refs, n_sub, has_ctx):
    h_refs = refs[:n_sub]
    refs = refs[n_sub:]
    ctx_ref = None
    if has_ctx:
        ctx_ref, refs = refs[0], refs[1:]
    (mod_ref, gpre_ref, w_ref, gqt_ref, gk_ref, ek_ref, cos_ref, sa_ref, sb_ref, cost_ref, sat_ref, sbt_ref,
     qa_ref, kda_ref, vta_ref, qb_ref, kdb_ref, vtb_ref, nb_ref) = refs
    kv = 2 * KV_COLS
    sq_q, sq_k = [], []

    def prenorm(i):
        mod = _sub_mod(mod_ref, True, i)
        x = _sub_tile(h_refs, ctx_ref, i)
        return (_rms(x, gpre_ref[...]) * (1 + mod[1:2, :]) + mod[0:1, :]).astype(BF16)

    def stages(i, u):
        rows = slice(i * TOK_TILE, (i + 1) * TOK_TILE)
        cos = cos_ref[rows, :]
        sa = sa_ref[rows, :]
        sb = sb_ref[rows, :]

        def rope(t):
            up = pltpu.roll(t, LANES - ROPE_SHIFT, 1)
            dn = pltpu.roll(t, ROPE_SHIFT, 1)
            return t * cos + up * sa + dn * sb

        def head_norm(t, e_ref, g_ref):
            ss = jnp.dot((t * t).astype(BF16), e_ref[...], preferred_element_type=F32)
            return t * lax.rsqrt(ss * (1.0 / HEAD_DIM) + EPS) * g_ref[...]

        def store_q(q, qt_ref, head_normed):
            q_t = q.T.reshape(N_HEADS, HEAD_DIM, TOK_TILE)
            ss = jnp.sum(q_t * q_t, axis=1, keepdims=True)
            if head_normed:
                q_t = q_t * lax.rsqrt(ss * (1.0 / HEAD_DIM) + EPS) * gqt_ref[...][None]
            else:
                sq_q.append(jnp.max(ss, axis=0))
            up = jnp.concatenate([q_t[:, ROPE_SHIFT:, :], q_t[:, :ROPE_SHIFT, :]], axis=1)
            dn = jnp.concatenate([q_t[:, -ROPE_SHIFT:, :], q_t[:, :-ROPE_SHIFT, :]], axis=1)
            q_t = q_t * cost_ref[:, rows][None] + up * sat_ref[:, rows][None] + dn * sbt_ref[:, rows][None]
            qt_ref[:, :, rows] = (q_t * Q_SCALE).astype(BF16)

        def store_k(k, k_ref, report_norms=False):
            if report_norms:
                ss = jnp.dot((k * k).astype(BF16), ek_ref[...], preferred_element_type=F32)
                sq_k.append(jnp.max(ss, axis=0, keepdims=True))
            k_ref[rows, :] = rope(k).astype(BF16)

        def store_vt(v, vt_ref):
            v_t = v.T.astype(BF16)
            ones = jnp.ones((VT_ROWS - HEAD_DIM, TOK_TILE), BF16)
            for h in range(N_KV):
                vt_ref[h, 0:HEAD_DIM, rows] = v_t[h * HEAD_DIM:(h + 1) * HEAD_DIM, :]
                vt_ref[h, HEAD_DIM:VT_ROWS, rows] = ones

        def project(c0, n):
            return lambda: jnp.dot(u, w_ref[:, c0:c0 + n], preferred_element_type=F32)

        def epi_qa(z):
            store_q(z, qa_ref, head_normed=True)

        def epi_kva(z):
            store_k(head_norm(z[:, 0:KV_COLS], ek_ref, gk_ref), kda_ref)
            store_vt(z[:, KV_COLS:kv], vta_ref)

        def epi_qb(z):
            store_q(z, qb_ref, head_normed=False)

        def epi_kvb(z):
            store_k(z[:, 0:KV_COLS], kdb_ref, report_norms=True)
            store_vt(z[:, KV_COLS:kv], vtb_ref)

        return [(project(0, Q_COLS), epi_qa), (project(Q_COLS, kv), epi_kva),
                (project(Q_COLS + kv, Q_COLS), epi_qb), (project(2 * Q_COLS + kv, kv), epi_kvb)]

    u_next = prenorm(0)
    pending = []
    for i in range(n_sub):
        u = u_next
        results = []
        for g, (matmul, epilogue) in enumerate(stages(i, u)):
            results.append((epilogue, matmul()))
            if g == 0 and i + 1 < n_sub:
                u_next = prenorm(i + 1)
            if pending:
                epi, z = pending.pop(0)
                epi(z)
        for epi, z in pending:
            epi(z)
        pending = results
    for epi, z in pending:
        epi(z)
    q_max = functools.reduce(jnp.maximum, sq_q)
    k_max = functools.reduce(jnp.maximum, sq_k)
    nb_ref[...] = jnp.concatenate([q_max, jnp.concatenate([k_max, k_max], axis=1),
                                   jnp.zeros((6, TOK_TILE), F32)], axis=0)


def _proj(h_lat, h_ctx, lat_tile_off, n_tiles, mods, g_pre, w_in, layer, gq_t, gk, ek, rope, rope_t):
    bsz = h_lat.shape[0]
    has_ctx = h_ctx is not None
    n_sub = max(s for s in PROJ_SUBS if n_tiles % s == 0)
    n_tok = n_tiles * TOK_TILE
    rows = n_sub * TOK_TILE
    const = lambda b, t: (0, 0)
    tok = lambda b, t: (b, t, 0)
    q_shape = jax.ShapeDtypeStruct((bsz, N_HEADS, HEAD_DIM, n_tok), BF16)
    kd_shape = jax.ShapeDtypeStruct((bsz, n_tok, KV_COLS), BF16)
    vt_shape = jax.ShapeDtypeStruct((bsz, N_KV, VT_ROWS, n_tok), BF16)
    q_spec = pl.BlockSpec((None, N_HEADS, HEAD_DIM, rows), lambda b, t: (b, 0, 0, t))
    kd_spec = pl.BlockSpec((None, rows, KV_COLS), tok)
    vt_spec = pl.BlockSpec((None, N_KV, VT_ROWS, rows), lambda b, t: (b, 0, 0, t))
    rope_spec = pl.BlockSpec((rows, LANES), lambda b, t: (t, 0))
    rope_t_spec = pl.BlockSpec((HEAD_DIM, rows), lambda b, t: (0, t))
    nb_shape = jax.ShapeDtypeStruct((bsz, n_tiles // n_sub, 8, TOK_TILE), F32)
    nb_spec = pl.BlockSpec((None, None, 8, TOK_TILE), lambda b, t: (b, t, 0, 0))
    h_args = [h_lat] * n_sub + ([h_ctx] if has_ctx else [])
    return pl.pallas_call(
        functools.partial(_proj_kernel, n_sub=n_sub, has_ctx=has_ctx),
        grid=(bsz, n_tiles // n_sub),
        in_specs=_stream_specs(n_sub, D_MODEL, lat_tile_off, has_ctx) + [
            pl.BlockSpec((None, 2, N_MOD, D_MODEL), lambda b, t: (b, 0, 0, 0)),
            pl.BlockSpec((1, D_MODEL), const),
            pl.BlockSpec((None, D_MODEL, IN_COLS), lambda b, t: (layer, 0, 0)),
            pl.BlockSpec((HEAD_DIM, TOK_TILE), const),
            pl.BlockSpec((1, KV_COLS), const),
            pl.BlockSpec((KV_COLS, KV_COLS), const),
            rope_spec, rope_spec, rope_spec, rope_t_spec, rope_t_spec, rope_t_spec,
        ],
        out_specs=[q_spec, kd_spec, vt_spec, q_spec, kd_spec, vt_spec, nb_spec],
        out_shape=[q_shape, kd_shape, vt_shape, q_shape, kd_shape, vt_shape, nb_shape],
        compiler_params=pltpu.CompilerParams(
            dimension_semantics=("parallel", "parallel"), vmem_limit_bytes=VMEM_LIMIT),
        name="proj",
    )(*h_args, mods, g_pre, w_in, gq_t, gk, ek, *rope, *rope_t)


def _band_start(blk, n_ctx, n_tok):
    return jnp.minimum(n_ctx + (blk - 1) * Q_BLOCK, n_tok - (Q_BLOCK + 2 * WINDOW))


def _window_bias_variants(n_ctx, n_tok, n_lat_blk):
    n_band = Q_BLOCK + 2 * WINDOW
    kj = jnp.arange(n_band, dtype=jnp.int32)[:, None]
    qr = jnp.arange(GROUP * Q_BLOCK, dtype=jnp.int32)[None, :] % Q_BLOCK
    variants = []
    for blk in (0, 1, n_lat_blk - 1):
        kpos = kj + (_band_start(blk, n_ctx, n_tok) - n_ctx)
        qpos = qr + blk * Q_BLOCK
        valid = (jnp.abs(kpos - qpos) <= WINDOW) & (kpos >= 0)
        variants.append(jnp.where(valid, 0.0, NEG_BIG).astype(F32))
    return jnp.stack(variants)


def _attn_kernel(*refs, n_ctx, n_full, band, use_sink, n_tok, n_lat_blk, q_row0, q_per_step):
    refs = list(refs)
    qt_ref, k_ref, vt_ref = refs[:3]
    rest = refs[3:]
    sink_ref = rest.pop(0) if use_sink else None
    bias_ref = rest.pop(0) if band else None
    bound_ref, o_ref = rest
    nq = GROUP * Q_BLOCK
    n_band = Q_BLOCK + 2 * WINDOW if band else 0
    chunks = [(c0, min(c0 + KEY_CHUNK, n_full), False) for c0 in range(0, n_full, KEY_CHUNK)]
    if band:
        chunks.append((n_full, n_full + n_band, True))
    kv_head = pl.program_id(1)
    step = pl.program_id(2)

    def band_start(blk):
        return pl.multiple_of(_band_start(blk, n_ctx, n_tok), LANES)

    def extend_q(blk):
        cols = pl.ds(pl.multiple_of(q_row0 + blk * Q_BLOCK, Q_BLOCK), Q_BLOCK)
        top = jnp.concatenate([qt_ref[g, :, cols] for g in range(GROUP)], axis=1)
        zero = jnp.zeros_like(top)
        return jnp.where(kv_head == 0, jnp.concatenate([top, zero], axis=0),
                         jnp.concatenate([zero, top], axis=0))

    def logits(qm, blk, chunk):
        c0, c1, in_band = chunk
        if in_band:
            variant = jnp.where(blk == 0, 0, jnp.where(blk >= n_lat_blk - 1, 2, 1))
            return jnp.dot(k_ref[pl.ds(band_start(blk), n_band), :], qm,
                           preferred_element_type=F32) + bias_ref[variant]
        return jnp.dot(k_ref[c0:c1, :], qm, preferred_element_type=F32)

    def values_t(blk, chunk):
        c0, c1, in_band = chunk
        if in_band:
            return vt_ref[:, pl.ds(band_start(blk), n_band)]
        return vt_ref[:, c0:c1]

    def column_max(qm, blk):
        mx = None
        for chunk in chunks:
            s = logits(qm, blk, chunk)
            cm = jnp.max(s.reshape(s.shape[0] // 8, 8, nq), axis=0)
            mx = cm if mx is None else jnp.maximum(mx, cm)
        return jnp.max(mx, axis=0, keepdims=True)

    def finalize(t, acc, m):
        l = acc[HEAD_DIM:HEAD_DIM + 1, :]
        if use_sink:
            l = l + jnp.exp2(sink_ref[...] - m)
        o_t = acc[0:HEAD_DIM, :] / l
        o_t = jnp.concatenate([o_t[:, g * Q_BLOCK:(g + 1) * Q_BLOCK] for g in range(GROUP)], axis=0)
        o_ref[pl.ds(pl.multiple_of(t * Q_BLOCK, Q_BLOCK), Q_BLOCK), :] = o_t.T.astype(BF16)

    def attend(blocks):
        items = [(i, c) for i in range(len(blocks)) for c in range(len(chunks))]
        blks = [step * q_per_step + t for t, _ in blocks]
        qms = [extend_q(blk) for blk in blks]
        s_next = logits(qms[0], blks[0], chunks[0])
        acc = [None] * len(blocks)
        pv_prev = None

        def retire(prev):
            i, c, pv = prev
            acc[i] = pv if acc[i] is None else acc[i] + pv
            if c == len(chunks) - 1:
                finalize(blocks[i][0], acc[i], blocks[i][1])

        for n, (i, c) in enumerate(items):
            s = s_next
            if n + 1 < len(items):
                i2, c2 = items[n + 1]
                s_next = logits(qms[i2], blks[i2], chunks[c2])
            p = jnp.exp2(s - blocks[i][1]).astype(BF16)
            pv = jnp.dot(values_t(blks[i], chunks[c]), p, preferred_element_type=F32)
            if pv_prev is not None:
                retire(pv_prev)
            pv_prev = (i, c, pv)
        retire(pv_prev)

    def with_sink(m):
        return jnp.maximum(m, sink_ref[...]) if use_sink else m

    bound = bound_ref[pl.program_id(0)]
    one_pass = bound <= LOGIT_BOUND_LIMIT

    @pl.when(one_pass)
    def _():
        m = with_sink(jnp.full((1, nq), bound, F32))
        attend([(t, m) for t in range(q_per_step)])

    @pl.when(jnp.logical_not(one_pass))
    def _():
        def one_block(t, carry):
            blk = step * q_per_step + t
            attend([(t, with_sink(column_max(extend_q(blk), blk)))])
            return carry

        lax.fori_loop(0, q_per_step, one_block, 0)


def _attention(q, kd, vt, sink_rows, logit_bound, *, q_blk_off, n_qblk, n_ctx, n_full, band,
               q_per_step, name):
    bsz, n_tok, _ = kd.shape
    q_per_step = min(q_per_step, n_qblk)
    n_rows = n_tok if (band or n_full == n_tok) else max(n_full, (q_blk_off + n_qblk) * Q_BLOCK)
    assert n_qblk % q_per_step == 0
    use_sink = sink_rows is not None
    nq = GROUP * Q_BLOCK
    in_specs = [
        pl.BlockSpec((None, GROUP, HEAD_DIM, n_rows), lambda b, h, j: (b, h, 0, 0)),
        pl.BlockSpec((None, n_rows, KV_COLS), lambda b, h, j: (b, 0, 0)),
        pl.BlockSpec((None, None, VT_ROWS, n_rows), lambda b, h, j: (b, h, 0, 0)),
    ]
    args = [q, kd, vt]
    if use_sink:
        in_specs.append(pl.BlockSpec((None, 1, nq), lambda b, h, j: (h, 0, 0)))
        args.append(sink_rows)
    if band:
        bias = _window_bias_variants(n_ctx, n_tok, n_qblk)
        in_specs.append(pl.BlockSpec(bias.shape, lambda b, h, j: (0, 0, 0)))
        args.append(bias)
    in_specs.append(pl.BlockSpec(memory_space=pltpu.SMEM))
    args.append(logit_bound)
    kern = functools.partial(_attn_kernel, n_ctx=n_ctx, n_full=n_full, band=band, use_sink=use_sink,
                             n_tok=n_tok, n_lat_blk=n_qblk, q_row0=q_blk_off * Q_BLOCK,
                             q_per_step=q_per_step)
    rows = q_per_step * Q_BLOCK
    return pl.pallas_call(
        kern,
        grid=(bsz, N_KV, n_qblk // q_per_step),
        in_specs=in_specs,
        out_specs=pl.BlockSpec((None, rows, GROUP * HEAD_DIM), lambda b, h, j: (b, j, h)),
        out_shape=jax.ShapeDtypeStruct((bsz, n_qblk * Q_BLOCK, Q_COLS), BF16),
        compiler_params=pltpu.CompilerParams(
            dimension_semantics=("parallel", "parallel", "arbitrary"), vmem_limit_bytes=VMEM_LIMIT),
        name=name,
    )(*args)


def _post_kernel(*refs, n_sub, has_ctx):
    n_in = n_sub + (1 if has_ctx else 0)
    h_refs, ma_refs, mb_refs = refs[:n_sub], refs[n_in:n_in + n_sub], refs[2 * n_in:2 * n_in + n_sub]
    ctx_h = refs[n_sub] if has_ctx else None
    ctx_ma = refs[n_in + n_sub] if has_ctx else None
    ctx_mb = refs[2 * n_in + n_sub] if has_ctx else None
    mod_ref, wout_ref, gpm_ref, gpre_ref, wup_ref, wdn_ref, gpl_ref, o_ref = refs[3 * n_in:]
    mods = [_sub_mod(mod_ref, has_ctx, i) for i in range(n_sub)]

    def out_proj(i):
        mix = jnp.concatenate([_sub_tile(ma_refs, ctx_ma, i), _sub_tile(mb_refs, ctx_mb, i)], axis=1)
        return jnp.dot(mix, wout_ref[...], preferred_element_type=F32)

    def mixer_residual(i, mix):
        mod = mods[i]
        h = _sub_tile(h_refs, ctx_h, i) + mod[2:3, :] * _rms(mix, gpm_ref[...])
        u = (_rms(h, gpre_ref[...]) * (1 + mod[4:5, :]) + mod[3:4, :]).astype(BF16)
        return h, u

    def mlp_chunk(u, y, c0):
        a = jnp.dot(u, wup_ref[:, c0:c0 + FF_CHUNK], preferred_element_type=F32)
        a = jnp.square(jnp.maximum(a, 0.0)).astype(BF16)
        part = jnp.dot(a, wdn_ref[c0:c0 + FF_CHUNK, :], preferred_element_type=F32)
        return part if y is None else y + part

    def mlp_residual(i, h, y):
        o_ref[i * TOK_TILE:(i + 1) * TOK_TILE, :] = h + mods[i][5:6, :] * _rms(y, gpl_ref[...])

    hu = [None] * n_sub
    mix_prev = None
    for i in range(n_sub):
        mix = out_proj(i)
        if i > 0:
            hu[i - 1] = mixer_residual(i - 1, mix_prev)
        mix_prev = mix
    for i in range(n_sub):
        y = None
        for c, c0 in enumerate(range(0, D_FF, FF_CHUNK)):
            if i == 0 and c == 0:
                hu[n_sub - 1] = mixer_residual(n_sub - 1, mix_prev)
            y = mlp_chunk(hu[i][1], y, c0)
            if i > 0 and c == 0:
                mlp_residual(i - 1, hu[i - 1][0], y_prev)
        y_prev = y
    mlp_residual(n_sub - 1, hu[n_sub - 1][0], y_prev)


def _post(h_lat, h_ctx, lat_tile_off, n_tiles, mix_a, mix_b, ctx_mix_a, ctx_mix_b, mods, layer, w_out,
          g_post_mix, g_pre_mlp, w_up, w_down, g_post_mlp):
    bsz = h_lat.shape[0]
    has_ctx = h_ctx is not None
    n_sub = max(s for s in POST_SUBS if n_tiles % s == 0)
    const = lambda b, t: (0, 0)
    resident = lambda rows, cols: pl.BlockSpec((None, rows, cols), lambda b, t: (layer, 0, 0),
                                               pipeline_mode=pl.Buffered(1))
    ctx_args = lambda v: [v] if has_ctx else []
    args = ([h_lat] * n_sub + ctx_args(h_ctx) + [mix_a] * n_sub + ctx_args(ctx_mix_a)
            + [mix_b] * n_sub + ctx_args(ctx_mix_b))
    return pl.pallas_call(
        functools.partial(_post_kernel, n_sub=n_sub, has_ctx=has_ctx),
        grid=(bsz, n_tiles // n_sub),
        in_specs=(_stream_specs(n_sub, D_MODEL, lat_tile_off, has_ctx)
                  + _stream_specs(n_sub, Q_COLS, 0, has_ctx)
                  + _stream_specs(n_sub, Q_COLS, 0, has_ctx) + [
            pl.BlockSpec((None, 2, N_MOD, D_MODEL), lambda b, t: (b, 0, 0, 0)),
            resident(D_MODEL, D_MODEL),
            pl.BlockSpec((1, D_MODEL), const),
            pl.BlockSpec((1, D_MODEL), const),
            resident(D_MODEL, D_FF),
            resident(D_FF, D_MODEL),
            pl.BlockSpec((1, D_MODEL), const),
        ]),
        out_specs=pl.BlockSpec((None, n_sub * TOK_TILE, D_MODEL), lambda b, t: (b, t, 0)),
        out_shape=jax.ShapeDtypeStruct((bsz, n_tiles * TOK_TILE, D_MODEL), F32),
        compiler_params=pltpu.CompilerParams(
            dimension_semantics=("parallel", "parallel"), vmem_limit_bytes=VMEM_LIMIT),
        name="post",
    )(*args, mods, w_out, g_post_mix, g_pre_mlp, w_up, w_down, g_post_mlp)


def _rope_tables(n_ctx, n_lat):
    rows = n_lat // GRID_W
    row_ids = jnp.repeat(jnp.arange(rows, dtype=jnp.int32), GRID_W).astype(F32)
    col_ids = jnp.tile(jnp.arange(GRID_W, dtype=jnp.int32), rows).astype(F32)
    axis_dim = HEAD_DIM // 2
    inv = ROPE_THETA ** (-jnp.arange(0, axis_dim, 2, dtype=F32) / axis_dim)
    ang_r = row_ids[:, None] * inv[None, :]
    ang_c = col_ids[:, None] * inv[None, :]
    cos_r, sin_r, cos_c, sin_c = jnp.cos(ang_r), jnp.sin(ang_r), jnp.cos(ang_c), jnp.sin(ang_c)
    zero = jnp.zeros_like(sin_r)
    cos_h = jnp.concatenate([cos_r, cos_r, cos_c, cos_c], axis=-1)
    sa_h = jnp.concatenate([-sin_r, zero, -sin_c, zero], axis=-1)
    sb_h = jnp.concatenate([zero, sin_r, zero, sin_c], axis=-1)

    def full(t, ctx_value):
        t = jnp.tile(t, (1, LANES // HEAD_DIM))
        return jnp.concatenate([jnp.full((n_ctx, LANES), ctx_value, F32), t], axis=0)

    return full(cos_h, 1.0), full(sa_h, 0.0), full(sb_h, 0.0)


def _head_ones(n_cols):
    head = jnp.arange(n_cols, dtype=jnp.int32) // HEAD_DIM
    return (head[:, None] == head[None, :]).astype(BF16)


def kernel(x, c, ctx, c_ctx, w_ada, b_ada, g_pre_mix, g_post_mix, g_pre_mlp, g_post_mlp,
           w_in, q_norm, k_norm, sink, w_out, w_up, w_down):
    bsz, n_lat, d = x.shape
    n_ctx = ctx.shape[1]
    depth = w_in.shape[0]
    n_tok = n_ctx + n_lat
    assert d == D_MODEL and n_ctx == TOK_TILE and n_lat % TOK_TILE == 0 and bsz + 1 <= ADA_ROWS
    assert w_in.shape[2] == IN_COLS and w_up.shape[2] == D_FF

    cc = jnp.concatenate([c, c_ctx[None, :], jnp.zeros((ADA_ROWS - bsz - 1, d), F32)], axis=0)
    mod_all = _ada(cc, w_ada, b_ada)
    mod_all = mod_all.reshape(depth, ADA_ROWS, N_MOD, d)
    mod_ctx = jnp.broadcast_to(mod_all[:, bsz:bsz + 1], (depth, bsz, N_MOD, d))
    mods = jnp.stack([mod_ctx, mod_all[:, :bsz]], axis=2)

    rope = _rope_tables(n_ctx, n_lat)
    rope_t = tuple(t[:, :HEAD_DIM].T for t in rope)
    ek = _head_ones(KV_COLS)
    row = lambda v: v.reshape(1, -1)

    w_in_b = w_in.astype(BF16)
    w_out_b = w_out.astype(BF16)
    w_up_b = w_up.astype(BF16)
    w_down_b = w_down.astype(BF16)

    ctx_tiles = n_ctx // TOK_TILE
    assert depth >= 1 and ctx_tiles == 1
    ctx_qblk = n_ctx // Q_BLOCK
    lat_qblk = n_lat // Q_BLOCK

    h_lat, h_ctx = x, ctx
    lat_off = 0
    for l in range(depth):
        last = l == depth - 1
        gq_t = jnp.broadcast_to(q_norm[l][:, None], (HEAD_DIM, TOK_TILE))
        gk = row(jnp.tile(k_norm[l], N_KV))
        sink_rows = jnp.repeat(sink[l].reshape(N_KV, 1, GROUP) * LOG2_E, Q_BLOCK, axis=2)

        qa, kda, vta, qb, kdb, vtb, sq_b = _proj(h_lat, h_ctx, 0, n_tok // TOK_TILE, mods[l],
                                           row(g_pre_mix[l]), w_in_b, l, gq_t, gk, ek, rope, rope_t)
        bound_a = jnp.full((bsz,), HEAD_DIM * Q_SCALE * BOUND_SLACK, F32) * (
            jnp.max(jnp.abs(q_norm[l])) * jnp.max(jnp.abs(k_norm[l])))
        bound_b = (Q_SCALE * BOUND_SLACK) * jnp.sqrt(jnp.max(sq_b[:, :, 0, :], axis=(1, 2))
                                                     * jnp.max(sq_b[:, :, 1, :], axis=(1, 2)))
        oa = _attention(qa, kda, vta, None, bound_a, q_blk_off=ctx_qblk, n_qblk=lat_qblk,
                        n_ctx=n_ctx, n_full=n_tok, band=False, q_per_step=Q_PER_STEP_GLOBAL,
                        name="attn_global")
        ob = _attention(qb, kdb, vtb, sink_rows, bound_b, q_blk_off=ctx_qblk, n_qblk=lat_qblk,
                        n_ctx=n_ctx, n_full=n_ctx, band=True, q_per_step=Q_PER_STEP_WINDOW,
                        name="attn_window")
        post = functools.partial(_post, mods=mods[l], layer=l, w_out=w_out_b, g_post_mix=row(g_post_mix[l]),
                                 g_pre_mlp=row(g_pre_mlp[l]), w_up=w_up_b, w_down=w_down_b,
                                 g_post_mlp=row(g_post_mlp[l]))
        if not last:
            oac = _attention(qa, kda, vta, None, bound_a, q_blk_off=0, n_qblk=ctx_qblk, n_ctx=n_ctx,
                             n_full=n_ctx, band=False, q_per_step=ctx_qblk, name="attn_ctx_global")
            obc = _attention(qb, kdb, vtb, sink_rows, bound_b, q_blk_off=0, n_qblk=ctx_qblk, n_ctx=n_ctx,
                             n_full=n_ctx, band=False, q_per_step=ctx_qblk, name="attn_ctx_sink")
            assert h_ctx is not None
            h_lat = post(h_lat, h_ctx, lat_off, n_tok // TOK_TILE, oa, ob, oac, obc)
            h_ctx, lat_off = None, ctx_tiles
        else:
            if h_ctx is not None:
                h_lat = post(h_lat, None, 0, n_lat // TOK_TILE, oa, ob, None, None)
            else:
                h_lat = post(h_lat, None, lat_off, n_lat // TOK_TILE, oa, ob, None, None)
    return h_lat
```

```python
import functools

import jax
import jax.numpy as jnp
from jax import lax
from jax.experimental import pallas as pl
from jax.experimental.pallas import tpu as pltpu

F32 = jnp.float32
BF16 = jnp.bfloat16

D_MODEL = 1024
HEAD_DIM = 64
N_HEADS = 8
N_KV = 2
GROUP = N_HEADS // N_KV
Q_COLS = N_HEADS * HEAD_DIM
KV_COLS = N_KV * HEAD_DIM
IN_COLS = 2 * (Q_COLS + 2 * KV_COLS)
D_FF = 4 * D_MODEL
GRID_W = 64
WINDOW = 128
Q_BLOCK = 128
ROPE_THETA = 10000.0
EPS = 1e-6
NEG_BIG = -1e30
N_MOD = 6
ROPE_SHIFT = HEAD_DIM // 4
LOG2_E = 1.4426950408889634
Q_SCALE = HEAD_DIM ** -0.5 * LOG2_E

LANES = 128
TOK_TILE = 256
BF16_SUBLANES = 16
PROJ_SUBS = (1, 3, 9)
POST_SUBS = (1, 2, 3, 4)
VT_ROWS = HEAD_DIM + BF16_SUBLANES
KEY_CHUNK = 4096
Q_PER_STEP_GLOBAL = 8
Q_PER_STEP_WINDOW = 16
LOGIT_BOUND_LIMIT = 48.0
BOUND_SLACK = 1.01
FF_CHUNK = 1024
ADA_ROWS = 32
ADA_COLS = 1536
VMEM_LIMIT = 56 * 1024 * 1024


def _rms(x, g):
    return x * lax.rsqrt(jnp.mean(x * x, axis=-1, keepdims=True) + EPS) * g


def _ada_kernel(c_ref, w_ref, b_ref, o_ref):
    c = c_ref[...]
    a = c * jax.nn.sigmoid(c)
    w = w_ref[...]
    a_hi = a.astype(BF16)
    a_lo = (a - a_hi.astype(F32)).astype(BF16)
    w_hi = w.astype(BF16)
    w_lo = (w - w_hi.astype(F32)).astype(BF16)
    acc = jnp.dot(a_hi, w_hi, preferred_element_type=F32)
    acc += jnp.dot(a_lo, w_hi, preferred_element_type=F32)
    acc += jnp.dot(a_hi, w_lo, preferred_element_type=F32)
    o_ref[...] = acc + b_ref[...]


def _ada(cc, w_ada, b_ada):
    depth = w_ada.shape[0]
    n_out = w_ada.shape[2]
    return pl.pallas_call(
        _ada_kernel,
        grid=(depth, n_out // ADA_COLS),
        in_specs=[
            pl.BlockSpec((ADA_ROWS, D_MODEL), lambda l, j: (0, 0)),
            pl.BlockSpec((None, D_MODEL, ADA_COLS), lambda l, j: (l, 0, j)),
            pl.BlockSpec((None, 1, ADA_COLS), lambda l, j: (l, 0, j)),
        ],
        out_specs=pl.BlockSpec((None, ADA_ROWS, ADA_COLS), lambda l, j: (l, 0, j)),
        out_shape=jax.ShapeDtypeStruct((depth, ADA_ROWS, n_out), F32),
        compiler_params=pltpu.CompilerParams(
            dimension_semantics=("arbitrary", "arbitrary"), vmem_limit_bytes=VMEM_LIMIT),
        name="ada",
    )(cc, w_ada, b_ada.reshape(depth, 1, n_out))


def _stream_specs(n_sub, block_cols, lat_tile_off, has_ctx):
    specs = []
    for i in range(n_sub):
        if has_ctx:
            idx = lambda b, t, i=i: (b, jnp.maximum(n_sub * t + i - 1, 0), 0)
        else:
            idx = lambda b, t, i=i: (b, lat_tile_off + n_sub * t + i, 0)
        specs.append(pl.BlockSpec((None, TOK_TILE, block_cols), idx))
    if has_ctx:
        specs.append(pl.BlockSpec((None, TOK_TILE, block_cols), lambda b, t: (b, 0, 0)))
    return specs


def _sub_tile(refs, ctx_ref, i):
    v = refs[i][...]
    if ctx_ref is not None and i == 0:
        v = jnp.where(pl.program_id(1) == 0, ctx_ref[...], v)
    return v


def _sub_mod(mod_ref, ctx_first, i):
    if ctx_first and i == 0:
        return jnp.where(pl.program_id(1) == 0, mod_ref[0], mod_ref[1])
    return mod_ref[1]


def _proj_kernel(*refs, n_sub, has_ctx):
    h_refs = refs[:n_sub]
    refs = refs[n_sub:]
    ctx_ref = None
    if has_ctx:
        ctx_ref, refs = refs[0], refs[1:]
    (mod_ref, gpre_ref, w_ref, gqt_ref, gk_ref, ek_ref, cos_ref, sa_ref, sb_ref, cost_ref, sat_ref, sbt_ref,
     qa_ref, kda_ref, vta_ref, qb_ref, kdb_ref, vtb_ref, nb_ref) = refs
    kv = 2 * KV_COLS
    sq_q, sq_k = [], []

    def prenorm(i):
        mod = _sub_mod(mod_ref, True, i)
        x = _sub_tile(h_refs, ctx_ref, i)
        return (_rms(x, gpre_ref[...]) * (1 + mod[1:2, :]) + mod[0:1, :]).astype(BF16)

    def stages(i, u):
        rows = slice(i * TOK_TILE, (i + 1) * TOK_TILE)
        cos = cos_ref[rows, :]
        sa = sa_ref[rows, :]
        sb = sb_ref[rows, :]

        def rope(t):
            up = pltpu.roll(t, LANES - ROPE_SHIFT, 1)
            dn = pltpu.roll(t, ROPE_SHIFT, 1)
            return t * cos + up * sa + dn * sb

        def head_norm(t, e_ref, g_ref):
            ss = jnp.dot((t * t).astype(BF16), e_ref[...], preferred_element_type=F32)
            return t * lax.rsqrt(ss * (1.0 / HEAD_DIM) + EPS) * g_ref[...]

        def store_q(q, qt_ref, head_normed):
            q_t = q.T.reshape(N_HEADS, HEAD_DIM, TOK_TILE)
            ss = jnp.sum(q_t * q_t, axis=1, keepdims=True)
            if head_normed:
                q_t = q_t * lax.rsqrt(ss * (1.0 / HEAD_DIM) + EPS) * gqt_ref[...][None]
            else:
                sq_q.append(jnp.max(ss, axis=0))
            up = jnp.concatenate([q_t[:, ROPE_SHIFT:, :], q_t[:, :ROPE_SHIFT, :]], axis=1)
            dn = jnp.concatenate([q_t[:, -ROPE_SHIFT:, :], q_t[:, :-ROPE_SHIFT, :]], axis=1)
            q_t = q_t * cost_ref[:, rows][None] + up * sat_ref[:, rows][None] + dn * sbt_ref[:, rows][None]
            qt_ref[:, :, rows] = (q_t * Q_SCALE).astype(BF16)

        def store_k(k, k_ref, report_norms=False):
            if report_norms:
                ss = jnp.dot((k * k).astype(BF16), ek_ref[...], preferred_element_type=F32)
                sq_k.append(jnp.max(ss, axis=0, keepdims=True))
            k_ref[rows, :] = rope(k).astype(BF16)

        def store_vt(v, vt_ref):
            v_t = v.T.astype(BF16)
            ones = jnp.ones((VT_ROWS - HEAD_DIM, TOK_TILE), BF16)
            for h in range(N_KV):
                vt_ref[h, 0:HEAD_DIM, rows] = v_t[h * HEAD_DIM:(h + 1) * HEAD_DIM, :]
                vt_ref[h, HEAD_DIM:VT_ROWS, rows] = ones

        def project(c0, n):
            return lambda: jnp.dot(u, w_ref[:, c0:c0 + n], preferred_element_type=F32)

        def epi_qa(z):
            store_q(z, qa_ref, head_normed=True)

        def epi_kva(z):
            store_k(head_norm(z[:, 0:KV_COLS], ek_ref, gk_ref), kda_ref)
            store_vt(z[:, KV_COLS:kv], vta_ref)

        def epi_qb(z):
            store_q(z, qb_ref, head_normed=False)

        def epi_kvb(z):
            store_k(z[:, 0:KV_COLS], kdb_ref, report_norms=True)
            store_vt(z[:, KV_COLS:kv], vtb_ref)

        return [(project(0, Q_COLS), epi_qa), (project(Q_COLS, kv), epi_kva),
                (project(Q_COLS + kv, Q_COLS), epi_qb), (project(2 * Q_COLS + kv, kv), epi_kvb)]

    u_next = prenorm(0)
    pending = []
    for i in range(n_sub):
        u = u_next
        results = []
        for g, (matmul, epilogue) in enumerate(stages(i, u)):
            results.append((epilogue, matmul()))
            if g == 0 and i + 1 < n_sub:
                u_next = prenorm(i + 1)
            if pending:
                epi, z = pending.pop(0)
                epi(z)
        for epi, z in pending:
            epi(z)
        pending = results
    for epi, z in pending:
        epi(z)
    q_max = functools.reduce(jnp.maximum, sq_q)
    k_max = functools.reduce(jnp.maximum, sq_k)
    nb_ref[...] = jnp.concatenate([q_max, jnp.concatenate([k_max, k_max], axis=1),
                                   jnp.zeros((6, TOK_TILE), F32)], axis=0)


def _proj(h_lat, h_ctx, lat_tile_off, n_tiles, mods, g_pre, w_in, layer, gq_t, gk, ek, rope, rope_t):
    bsz = h_lat.shape[0]
    has_ctx = h_ctx is not None
    n_sub = max(s for s in PROJ_SUBS if n_tiles % s == 0)
    n_tok = n_tiles * TOK_TILE
    rows = n_sub * TOK_TILE
    const = lambda b, t: (0, 0)
    tok = lambda b, t: (b, t, 0)
    q_shape = jax.ShapeDtypeStruct((bsz, N_HEADS, HEAD_DIM, n_tok), BF16)
    kd_shape = jax.ShapeDtypeStruct((bsz, n_tok, KV_COLS), BF16)
    vt_shape = jax.ShapeDtypeStruct((bsz, N_KV, VT_ROWS, n_tok), BF16)
    q_spec = pl.BlockSpec((None, N_HEADS, HEAD_DIM, rows), lambda b, t: (b, 0, 0, t))
    kd_spec = pl.BlockSpec((None, rows, KV_COLS), tok)
    vt_spec = pl.BlockSpec((None, N_KV, VT_ROWS, rows), lambda b, t: (b, 0, 0, t))
    rope_spec = pl.BlockSpec((rows, LANES), lambda b, t: (t, 0))
    rope_t_spec = pl.BlockSpec((HEAD_DIM, rows), lambda b, t: (0, t))
    nb_shape = jax.ShapeDtypeStruct((bsz, n_tiles // n_sub, 8, TOK_TILE), F32)
    nb_spec = pl.BlockSpec((None, None, 8, TOK_TILE), lambda b, t: (b, t, 0, 0))
    h_args = [h_lat] * n_sub + ([h_ctx] if has_ctx else [])
    return pl.pallas_call(
        functools.partial(_proj_kernel, n_sub=n_sub, has_ctx=has_ctx),
        grid=(bsz, n_tiles // n_sub),
        in_specs=_stream_specs(n_sub, D_MODEL, lat_tile_off, has_ctx) + [
            pl.BlockSpec((None, 2, N_MOD, D_MODEL), lambda b, t: (b, 0, 0, 0)),
            pl.BlockSpec((1, D_MODEL), const),
            pl.BlockSpec((None, D_MODEL, IN_COLS), lambda b, t: (layer, 0, 0)),
            pl.BlockSpec((HEAD_DIM, TOK_TILE), const),
            pl.BlockSpec((1, KV_COLS), const),
            pl.BlockSpec((KV_COLS, KV_COLS), const),
            rope_spec, rope_spec, rope_spec, rope_t_spec, rope_t_spec, rope_t_spec,
        ],
        out_specs=[q_spec, kd_spec, vt_spec, q_spec, kd_spec, vt_spec, nb_spec],
        out_shape=[q_shape, kd_shape, vt_shape, q_shape, kd_shape, vt_shape, nb_shape],
        compiler_params=pltpu.CompilerParams(
            dimension_semantics=("parallel", "parallel"), vmem_limit_bytes=VMEM_LIMIT),
        name="proj",
    )(*h_args, mods, g_pre, w_in, gq_t, gk, ek, *rope, *rope_t)


def _band_start(blk, n_ctx, n_tok):
    return jnp.minimum(n_ctx + (blk - 1) * Q_BLOCK, n_tok - (Q_BLOCK + 2 * WINDOW))


def _window_bias_variants(n_ctx, n_tok, n_lat_blk):
    n_band = Q_BLOCK + 2 * WINDOW
    kj = jnp.arange(n_band, dtype=jnp.int32)[:, None]
    qr = jnp.arange(GROUP * Q_BLOCK, dtype=jnp.int32)[None, :] % Q_BLOCK
    variants = []
    for blk in (0, 1, n_lat_blk - 1):
        kpos = kj + (_band_start(blk, n_ctx, n_tok) - n_ctx)
        qpos = qr + blk * Q_BLOCK
        valid = (jnp.abs(kpos - qpos) <= WINDOW) & (kpos >= 0)
        variants.append(jnp.where(valid, 0.0, NEG_BIG).astype(F32))
    return jnp.stack(variants)


def _attn_kernel(*refs, n_ctx, n_full, band, use_sink, n_tok, n_lat_blk, q_row0, q_per_step, n_cast):
    refs = list(refs)
    qt_ref, k_ref, vt_ref = refs[:3]
    rest = refs[3:]
    sink_ref = rest.pop(0) if use_sink else None
    bias_ref = rest.pop(0) if band else None
    bound_ref = rest.pop(0)
    cast_in, (o_ref, *cast_out) = rest[:n_cast], rest[n_cast:]
    for src, dst in zip(cast_in, cast_out):
        dst[...] = src[...].astype(BF16)
    nq = GROUP * Q_BLOCK
    n_band = Q_BLOCK + 2 * WINDOW if band else 0
    chunks = [(c0, min(c0 + KEY_CHUNK, n_full), False) for c0 in range(0, n_full, KEY_CHUNK)]
    if band:
        chunks.append((n_full, n_full + n_band, True))
    kv_head = pl.program_id(1)
    step = pl.program_id(2)

    def band_start(blk):
        return pl.multiple_of(_band_start(blk, n_ctx, n_tok), LANES)

    def extend_q(blk):
        cols = pl.ds(pl.multiple_of(q_row0 + blk * Q_BLOCK, Q_BLOCK), Q_BLOCK)
        top = jnp.concatenate([qt_ref[g, :, cols] for g in range(GROUP)], axis=1)
        zero = jnp.zeros_like(top)
        return jnp.where(kv_head == 0, jnp.concatenate([top, zero], axis=0),
                         jnp.concatenate([zero, top], axis=0))

    def logits(qm, blk, chunk):
        c0, c1, in_band = chunk
        if in_band:
            variant = jnp.where(blk == 0, 0, jnp.where(blk >= n_lat_blk - 1, 2, 1))
            return jnp.dot(k_ref[pl.ds(band_start(blk), n_band), :], qm,
                           preferred_element_type=F32) + bias_ref[variant]
        return jnp.dot(k_ref[c0:c1, :], qm, preferred_element_type=F32)

    def values_t(blk, chunk):
        c0, c1, in_band = chunk
        if in_band:
            return vt_ref[:, pl.ds(band_start(blk), n_band)]
        return vt_ref[:, c0:c1]

    def column_max(qm, blk):
        mx = None
        for chunk in chunks:
            s = logits(qm, blk, chunk)
            cm = jnp.max(s.reshape(s.shape[0] // 8, 8, nq), axis=0)
            mx = cm if mx is None else jnp.maximum(mx, cm)
        return jnp.max(mx, axis=0, keepdims=True)

    def finalize(t, acc, m):
        l = acc[HEAD_DIM:HEAD_DIM + 1, :]
        if use_sink:
            l = l + jnp.exp2(sink_ref[...] - m)
        o_t = acc[0:HEAD_DIM, :] / l
        o_t = jnp.concatenate([o_t[:, g * Q_BLOCK:(g + 1) * Q_BLOCK] for g in range(GROUP)], axis=0)
        o_ref[pl.ds(pl.multiple_of(t * Q_BLOCK, Q_BLOCK), Q_BLOCK), :] = o_t.T.astype(BF16)

    def attend(blocks):
        items = [(i, c) for i in range(len(blocks)) for c in range(len(chunks))]
        blks = [step * q_per_step + t for t, _ in blocks]
        qms = [extend_q(blk) for blk in blks]
        s_next = logits(qms[0], blks[0], chunks[0])
        acc = [None] * len(blocks)
        pv_prev = None

        def retire(prev):
            i, c, pv = prev
            acc[i] = pv if acc[i] is None else acc[i] + pv
            if c == len(chunks) - 1:
                finalize(blocks[i][0], acc[i], blocks[i][1])

        for n, (i, c) in enumerate(items):
            s = s_next
            if n + 1 < len(items):
                i2, c2 = items[n + 1]
                s_next = logits(qms[i2], blks[i2], chunks[c2])
            p = jnp.exp2(s - blocks[i][1]).astype(BF16)
            pv = jnp.dot(values_t(blks[i], chunks[c]), p, preferred_element_type=F32)
            if pv_prev is not None:
                retire(pv_prev)
            pv_prev = (i, c, pv)
        retire(pv_prev)

    def with_sink(m):
        return jnp.maximum(m, sink_ref[...]) if use_sink else m

    bound = bound_ref[pl.program_id(0)]
    one_pass = bound <= LOGIT_BOUND_LIMIT

    @pl.when(one_pass)
    def _():
        m = with_sink(jnp.full((1, nq), bound, F32))
        attend([(t, m) for t in range(q_per_step)])

    @pl.when(jnp.logical_not(one_pass))
    def _():
        def one_block(t, carry):
            blk = step * q_per_step + t
            attend([(t, with_sink(column_max(extend_q(blk), blk)))])
            return carry

        lax.fori_loop(0, q_per_step, one_block, 0)


def _attention(q, kd, vt, sink_rows, logit_bound, *, q_blk_off, n_qblk, n_ctx, n_full, band,
               q_per_step, name, cast_srcs=(), cast_layer=0):
    bsz, n_tok, _ = kd.shape
    q_per_step = min(q_per_step, n_qblk)
    n_rows = n_tok if (band or n_full == n_tok) else max(n_full, (q_blk_off + n_qblk) * Q_BLOCK)
    assert n_qblk % q_per_step == 0
    use_sink = sink_rows is not None
    nq = GROUP * Q_BLOCK
    in_specs = [
        pl.BlockSpec((None, GROUP, HEAD_DIM, n_rows), lambda b, h, j: (b, h, 0, 0)),
        pl.BlockSpec((None, n_rows, KV_COLS), lambda b, h, j: (b, 0, 0)),
        pl.BlockSpec((None, None, VT_ROWS, n_rows), lambda b, h, j: (b, h, 0, 0)),
    ]
    args = [q, kd, vt]
    if use_sink:
        in_specs.append(pl.BlockSpec((None, 1, nq), lambda b, h, j: (h, 0, 0)))
        args.append(sink_rows)
    if band:
        bias = _window_bias_variants(n_ctx, n_tok, n_qblk)
        in_specs.append(pl.BlockSpec(bias.shape, lambda b, h, j: (0, 0, 0)))
        args.append(bias)
    in_specs.append(pl.BlockSpec(memory_space=pltpu.SMEM))
    args.append(logit_bound)
    kern = functools.partial(_attn_kernel, n_ctx=n_ctx, n_full=n_full, band=band, use_sink=use_sink,
                             n_tok=n_tok, n_lat_blk=n_qblk, q_row0=q_blk_off * Q_BLOCK,
                             q_per_step=q_per_step, n_cast=len(cast_srcs))
    rows = q_per_step * Q_BLOCK
    n_steps = n_qblk // q_per_step
    out_specs = [pl.BlockSpec((None, rows, GROUP * HEAD_DIM), lambda b, h, j: (b, j, h))]
    out_shape = [jax.ShapeDtypeStruct((bsz, n_qblk * Q_BLOCK, Q_COLS), BF16)]
    flat_step = lambda b, h, j: (b * N_KV + h) * n_steps + j
    for w in cast_srcs:
        slab, rem = divmod(w.shape[1], bsz * N_KV * n_steps)
        assert rem == 0 and slab % BF16_SUBLANES == 0
        in_specs.append(pl.BlockSpec((None, slab, w.shape[2]),
                                     lambda b, h, j: (cast_layer, flat_step(b, h, j), 0)))
        args.append(w)
        out_specs.append(pl.BlockSpec((slab, w.shape[2]), lambda b, h, j: (flat_step(b, h, j), 0)))
        out_shape.append(jax.ShapeDtypeStruct(w.shape[1:], BF16))
    outs = pl.pallas_call(
        kern,
        grid=(bsz, N_KV, n_steps),
        in_specs=in_specs,
        out_specs=out_specs,
        out_shape=out_shape,
        compiler_params=pltpu.CompilerParams(
            dimension_semantics=("parallel", "parallel", "arbitrary"), vmem_limit_bytes=VMEM_LIMIT),
        name=name,
    )(*args)
    return outs if cast_srcs else outs[0]


def _post_kernel(*refs, n_sub, has_ctx):
    n_in = n_sub + (1 if has_ctx else 0)
    h_refs, ma_refs, mb_refs = refs[:n_sub], refs[n_in:n_in + n_sub], refs[2 * n_in:2 * n_in + n_sub]
    ctx_h = refs[n_sub] if has_ctx else None
    ctx_ma = refs[n_in + n_sub] if has_ctx else None
    ctx_mb = refs[2 * n_in + n_sub] if has_ctx else None
    mod_ref, wout_ref, gpm_ref, gpre_ref, wup_ref, wdn_ref, gpl_ref, o_ref = refs[3 * n_in:]
    mods = [_sub_mod(mod_ref, has_ctx, i) for i in range(n_sub)]

    def out_proj(i):
        mix = jnp.concatenate([_sub_tile(ma_refs, ctx_ma, i), _sub_tile(mb_refs, ctx_mb, i)], axis=1)
        return jnp.dot(mix, wout_ref[...], preferred_element_type=F32)

    def mixer_residual(i, mix):
        mod = mods[i]
        h = _sub_tile(h_refs, ctx_h, i) + mod[2:3, :] * _rms(mix, gpm_ref[...])
        u = (_rms(h, gpre_ref[...]) * (1 + mod[4:5, :]) + mod[3:4, :]).astype(BF16)
        return h, u

    def mlp_chunk(u, y, c0):
        a = jnp.dot(u, wup_ref[:, c0:c0 + FF_CHUNK], preferred_element_type=F32)
        a = jnp.square(jnp.maximum(a, 0.0)).astype(BF16)
        part = jnp.dot(a, wdn_ref[c0:c0 + FF_CHUNK, :], preferred_element_type=F32)
        return part if y is None else y + part

    def mlp_residual(i, h, y):
        o_ref[i * TOK_TILE:(i + 1) * TOK_TILE, :] = h + mods[i][5:6, :] * _rms(y, gpl_ref[...])

    hu = [None] * n_sub
    mix_prev = None
    for i in range(n_sub):
        mix = out_proj(i)
        if i > 0:
            hu[i - 1] = mixer_residual(i - 1, mix_prev)
        mix_prev = mix
    for i in range(n_sub):
        y = None
        for c, c0 in enumerate(range(0, D_FF, FF_CHUNK)):
            if i == 0 and c == 0:
                hu[n_sub - 1] = mixer_residual(n_sub - 1, mix_prev)
            y = mlp_chunk(hu[i][1], y, c0)
            if i > 0 and c == 0:
                mlp_residual(i - 1, hu[i - 1][0], y_prev)
        y_prev = y
    mlp_residual(n_sub - 1, hu[n_sub - 1][0], y_prev)


def _post(h_lat, h_ctx, lat_tile_off, n_tiles, mix_a, mix_b, ctx_mix_a, ctx_mix_b, mods, w_out,
          g_post_mix, g_pre_mlp, w_up, w_down, g_post_mlp):
    bsz = h_lat.shape[0]
    has_ctx = h_ctx is not None
    n_sub = max(s for s in POST_SUBS if n_tiles % s == 0)
    const = lambda b, t: (0, 0)
    resident = lambda rows, cols: pl.BlockSpec((rows, cols), const, pipeline_mode=pl.Buffered(1))
    ctx_args = lambda v: [v] if has_ctx else []
    args = ([h_lat] * n_sub + ctx_args(h_ctx) + [mix_a] * n_sub + ctx_args(ctx_mix_a)
            + [mix_b] * n_sub + ctx_args(ctx_mix_b))
    return pl.pallas_call(
        functools.partial(_post_kernel, n_sub=n_sub, has_ctx=has_ctx),
        grid=(bsz, n_tiles // n_sub),
        in_specs=(_stream_specs(n_sub, D_MODEL, lat_tile_off, has_ctx)
                  + _stream_specs(n_sub, Q_COLS, 0, has_ctx)
                  + _stream_specs(n_sub, Q_COLS, 0, has_ctx) + [
            pl.BlockSpec((None, 2, N_MOD, D_MODEL), lambda b, t: (b, 0, 0, 0)),
            resident(D_MODEL, D_MODEL),
            pl.BlockSpec((1, D_MODEL), const),
            pl.BlockSpec((1, D_MODEL), const),
            resident(D_MODEL, D_FF),
            resident(D_FF, D_MODEL),
            pl.BlockSpec((1, D_MODEL), const),
        ]),
        out_specs=pl.BlockSpec((None, n_sub * TOK_TILE, D_MODEL), lambda b, t: (b, t, 0)),
        out_shape=jax.ShapeDtypeStruct((bsz, n_tiles * TOK_TILE, D_MODEL), F32),
        compiler_params=pltpu.CompilerParams(
            dimension_semantics=("parallel", "parallel"), vmem_limit_bytes=VMEM_LIMIT),
        name="post",
    )(*args, mods, w_out, g_post_mix, g_pre_mlp, w_up, w_down, g_post_mlp)


def _rope_tables(n_ctx, n_lat):
    rows = n_lat // GRID_W
    row_ids = jnp.repeat(jnp.arange(rows, dtype=jnp.int32), GRID_W).astype(F32)
    col_ids = jnp.tile(jnp.arange(GRID_W, dtype=jnp.int32), rows).astype(F32)
    axis_dim = HEAD_DIM // 2
    inv = ROPE_THETA ** (-jnp.arange(0, axis_dim, 2, dtype=F32) / axis_dim)
    ang_r = row_ids[:, None] * inv[None, :]
    ang_c = col_ids[:, None] * inv[None, :]
    cos_r, sin_r, cos_c, sin_c = jnp.cos(ang_r), jnp.sin(ang_r), jnp.cos(ang_c), jnp.sin(ang_c)
    zero = jnp.zeros_like(sin_r)
    cos_h = jnp.concatenate([cos_r, cos_r, cos_c, cos_c], axis=-1)
    sa_h = jnp.concatenate([-sin_r, zero, -sin_c, zero], axis=-1)
    sb_h = jnp.concatenate([zero, sin_r, zero, sin_c], axis=-1)

    def full(t, ctx_value):
        t = jnp.tile(t, (1, LANES // HEAD_DIM))
        return jnp.concatenate([jnp.full((n_ctx, LANES), ctx_value, F32), t], axis=0)

    return full(cos_h, 1.0), full(sa_h, 0.0), full(sb_h, 0.0)


def _head_ones(n_cols):
    head = jnp.arange(n_cols, dtype=jnp.int32) // HEAD_DIM
    return (head[:, None] == head[None, :]).astype(BF16)


def kernel(x, c, ctx, c_ctx, w_ada, b_ada, g_pre_mix, g_post_mix, g_pre_mlp, g_post_mlp,
           w_in, q_norm, k_norm, sink, w_out, w_up, w_down):
    bsz, n_lat, d = x.shape
    n_ctx = ctx.shape[1]
    depth = w_in.shape[0]
    n_tok = n_ctx + n_lat
    assert d == D_MODEL and n_ctx == TOK_TILE and n_lat % TOK_TILE == 0 and bsz + 1 <= ADA_ROWS
    assert w_in.shape[2] == IN_COLS and w_up.shape[2] == D_FF

    cc = jnp.concatenate([c, c_ctx[None, :], jnp.zeros((ADA_ROWS - bsz - 1, d), F32)], axis=0)
    mod_all = _ada(cc, w_ada, b_ada)
    mod_all = mod_all.reshape(depth, ADA_ROWS, N_MOD, d)
    mod_ctx = jnp.broadcast_to(mod_all[:, bsz:bsz + 1], (depth, bsz, N_MOD, d))
    mods = jnp.stack([mod_ctx, mod_all[:, :bsz]], axis=2)

    rope = _rope_tables(n_ctx, n_lat)
    rope_t = tuple(t[:, :HEAD_DIM].T for t in rope)
    ek = _head_ones(KV_COLS)
    row = lambda v: v.reshape(1, -1)

    w_in_b = w_in.astype(BF16)

    ctx_tiles = n_ctx // TOK_TILE
    assert depth >= 1 and ctx_tiles == 1
    ctx_qblk = n_ctx // Q_BLOCK
    lat_qblk = n_lat // Q_BLOCK

    h_lat, h_ctx = x, ctx
    lat_off = 0
    for l in range(depth):
        last = l == depth - 1
        gq_t = jnp.broadcast_to(q_norm[l][:, None], (HEAD_DIM, TOK_TILE))
        gk = row(jnp.tile(k_norm[l], N_KV))
        sink_rows = jnp.repeat(sink[l].reshape(N_KV, 1, GROUP) * LOG2_E, Q_BLOCK, axis=2)

        qa, kda, vta, qb, kdb, vtb, sq_b = _proj(h_lat, h_ctx, 0, n_tok // TOK_TILE, mods[l],
                                           row(g_pre_mix[l]), w_in_b, l, gq_t, gk, ek, rope, rope_t)
        bound_a = jnp.full((bsz,), HEAD_DIM * Q_SCALE * BOUND_SLACK, F32) * (
            jnp.max(jnp.abs(q_norm[l])) * jnp.max(jnp.abs(k_norm[l])))
        bound_b = (Q_SCALE * BOUND_SLACK) * jnp.sqrt(jnp.max(sq_b[:, :, 0, :], axis=(1, 2))
                                                     * jnp.max(sq_b[:, :, 1, :], axis=(1, 2)))
        oa, w_out_l, w_up_l, w_down_l = _attention(
            qa, kda, vta, None, bound_a, q_blk_off=ctx_qblk, n_qblk=lat_qblk, n_ctx=n_ctx,
            n_full=n_tok, band=False, q_per_step=Q_PER_STEP_GLOBAL, name="attn_global",
            cast_srcs=(w_out, w_up, w_down), cast_layer=l)
        ob = _attention(qb, kdb, vtb, sink_rows, bound_b, q_blk_off=ctx_qblk, n_qblk=lat_qblk,
                        n_ctx=n_ctx, n_full=n_ctx, band=True, q_per_step=Q_PER_STEP_WINDOW,
                        name="attn_window")
        post = functools.partial(_post, mods=mods[l], w_out=w_out_l, g_post_mix=row(g_post_mix[l]),
                                 g_pre_mlp=row(g_pre_mlp[l]), w_up=w_up_l, w_down=w_down_l,
                                 g_post_mlp=row(g_post_mlp[l]))
        if not last:
            oac = _attention(qa, kda, vta, None, bound_a, q_blk_off=0, n_qblk=ctx_qblk, n_ctx=n_ctx,
                             n_full=n_ctx, band=False, q_per_step=ctx_qblk, name="attn_ctx_global")
            obc = _attention(qb, kdb, vtb, sink_rows, bound_b, q_blk_off=0, n_qblk=ctx_qblk, n_ctx=n_ctx,
                             n_full=n_ctx, band=False, q_per_step=ctx_qblk, name="attn_ctx_sink")
            assert h_ctx is not None
            h_lat = post(h_lat, h_ctx, lat_off, n_tok // TOK_TILE, oa, ob, oac, obc)
            h_ctx, lat_off = None, ctx_tiles
        else:
            if h_ctx is not None:
                h_lat = post(h_lat, None, 0, n_lat // TOK_TILE, oa, ob, None, None)
            else:
                h_lat = post(h_lat, None, lat_off, n_lat // TOK_TILE, oa, ob, None, None)
    return h_lat
```
